```python
import math
import jax
import jax.numpy as jnp
from jax import lax
import numpy as np

D_MODEL = 1024
BATCH = 2
SEQ = 8192
DEPTH = 2

GRID_W = 64
CTX_LEN = 256
N_EVEN = (DEPTH + 1) // 2
N_ODD = DEPTH // 2
EPS = 1e-6
CHUNK = 64
Q_BLOCK = 128
ROPE_THETA = 10000.0

MLSTM_HEADS = 4
MLSTM_DH = 128
MLSTM_W = MLSTM_HEADS * MLSTM_DH
HYENA_W = 512
HYENA_ORDER = 2
HYENA_SHORT = 3
HYENA_EMB = 33
HYENA_BANDS = (HYENA_EMB - 1) // 2
HYENA_FFN = 64
HYENA_NFILT = HYENA_ORDER * 2 * HYENA_W
HYENA_FAST_DECAY = 0.3
HYENA_SLOW_DECAY = 1.5
HYENA_TARGET = 1e-2
GDN_HEADS = 4
GDN_DK = 128
GDN_DV = 128
GDN_CONV = 3
MLA_HEADS = 4
MLA_NOPE = 128
MLA_ROPE = 64
MLA_QK = MLA_NOPE + MLA_ROPE
MLA_V = 128
MLA_Q_RANK = 384
MLA_KV_RANK = 256
D_FF = ((8 * D_MODEL // 3 + 127) // 128) * 128
FFN_CONV = 3

EVEN_LAYOUT = (('mq', MLSTM_W), ('mk', MLSTM_W), ('mv', MLSTM_W), ('mo', MLSTM_W),
               ('mi', 2 * MLSTM_HEADS), ('mf', 2 * MLSTM_HEADS), ('hy', (HYENA_ORDER + 1) * HYENA_W))
EVEN_CTX_STATE = ('mk', 'mv', 'mi', 'mf')
EVEN_IN = sum(w for _, w in EVEN_LAYOUT)
EVEN_MIX = MLSTM_W + HYENA_W
ODD_LAYOUT = (('gq', GDN_HEADS * GDN_DK), ('gk', GDN_HEADS * GDN_DK), ('gv', GDN_HEADS * GDN_DV),
              ('gg', GDN_HEADS * GDN_DV), ('gb', 2 * GDN_HEADS), ('ga', 2 * GDN_HEADS),
              ('lq', MLA_Q_RANK), ('lkv', MLA_KV_RANK), ('lkr', MLA_ROPE))
ODD_CTX_STATE = ('gk', 'gv', 'gb', 'ga', 'lkv', 'lkr')
ODD_IN = sum(w for _, w in ODD_LAYOUT)
ODD_MIX = GDN_HEADS * GDN_DV + MLA_HEADS * MLA_V

kernel_name = "hybrid_mlstm_hyena_gdn_mla_prefix_dit"

F32 = jnp.float32


def rms_norm(x):
    xf = x.astype(F32)
    return (xf * lax.rsqrt(jnp.mean(xf * xf, axis=-1, keepdims=True) + EPS)).astype(x.dtype)


def l2_norm(x):
    xf = x.astype(F32)
    return (xf * lax.rsqrt(jnp.sum(xf * xf, axis=-1, keepdims=True) + EPS)).astype(x.dtype)


def modulate(x, shift, scale):
    return rms_norm(x) * (1.0 + scale) + shift


def dwconv(x, w, b):
    k, ch = w.shape
    y = lax.conv_general_dilated(x, w[:, None, :], window_strides=(1,), padding=[((k - 1) // 2, k // 2)],
                                 dimension_numbers=('NWC', 'WIO', 'NWC'), feature_group_count=ch)
    return y + b


def to_heads(t, n_heads):
    b, l, _ = t.shape
    return t.reshape(b, l, n_heads, -1).transpose(0, 2, 1, 3)


def from_heads(t):
    b, h, l, d = t.shape
    return t.transpose(0, 2, 1, 3).reshape(b, l, h * d)


def dir_heads(t, n_heads):
    b, l, _ = t.shape
    return t.reshape(b, l, 2, n_heads).transpose(2, 0, 3, 1)


def flip_seq(t):
    return None if t is None else jnp.flip(t, axis=2)


def keep_seq(t):
    return t


def chunked(t):
    b, h, l = t.shape[:3]
    return t.reshape(b, h, l // CHUNK, CHUNK, *t.shape[3:])


def unchunk(t):
    nc, b, h, cs = t.shape[:4]
    return jnp.moveaxis(t, 0, 2).reshape(b, h, nc * cs, *t.shape[4:])


def project(h, w_in, layout, names=None):
    offsets, start = {}, 0
    for name, width in layout:
        offsets[name] = (start, width)
        start += width
    if names is None:
        names, w = tuple(n for n, _ in layout), w_in
    else:
        w = jnp.concatenate([w_in[:, offsets[n][0]:offsets[n][0] + offsets[n][1]] for n in names], axis=1)
    z = h @ w
    out, start = {}, 0
    for n in names:
        width = offsets[n][1]
        out[n] = z[..., start:start + width]
        start += width
    return out


def axial_rope(rows, dtype):
    row = jnp.repeat(jnp.arange(rows), GRID_W).astype(F32)
    col = jnp.tile(jnp.arange(GRID_W), rows).astype(F32)
    n_freq = MLA_ROPE // 4
    inv = ROPE_THETA ** (-jnp.arange(n_freq, dtype=F32) / n_freq)
    ang = jnp.concatenate([row[:, None] * inv, col[:, None] * inv], axis=-1)
    return jnp.cos(ang).astype(dtype), jnp.sin(ang).astype(dtype)


def rope_tail(x, cos, sin):
    half = MLA_ROPE // 2
    xp, x1, x2 = x[..., :-MLA_ROPE], x[..., -MLA_ROPE:-half], x[..., -half:]
    return jnp.concatenate([xp, x1 * cos - x2 * sin, x1 * sin + x2 * cos], axis=-1)


def block_attention(q, k, v):
    b, h, lq, dh = q.shape
    qb = jnp.moveaxis(q.reshape(b, h, lq // Q_BLOCK, Q_BLOCK, dh), 2, 0)
    scale = dh ** -0.5

    def one(qblk):
        s = jnp.einsum('bhqd,bhkd->bhqk', qblk, k).astype(F32) * scale
        p = jax.nn.softmax(s, axis=-1).astype(v.dtype)
        return jnp.einsum('bhqk,bhkd->bhqd', p, v)

    o = lax.map(one, qb)
    return jnp.moveaxis(o, 0, 2).reshape(b, h, lq, v.shape[-1])


def mlstm_scan(k, v, log_i, log_f, state, q=None):
    k, v, li, lf = chunked(k), chunked(v), chunked(log_i), chunked(log_f)
    b = jnp.cumsum(lf, axis=-1)
    b_last = b[..., -1]
    a = b_last[..., None] - b + li
    m_loc = jnp.max(a, axis=-1)
    wa = jnp.exp(a - m_loc[..., None])
    xs = (k, v, b_last, wa, m_loc)
    if q is not None:
        q = chunked(q)
        tri = jnp.tril(jnp.ones((CHUNK, CHUNK), dtype=bool))
        dmat = jnp.where(tri, b[..., :, None] - b[..., None, :] + li[..., None, :], -jnp.inf)
        m_in = jnp.max(dmat, axis=-1)
        s = jnp.einsum('bhcid,bhcjd->bhcij', q, k) * jnp.exp(dmat - m_in[..., None])
        xs = xs + (q, b, s @ v, jnp.sum(s, axis=-1), m_in)
    xs = tuple(jnp.moveaxis(t, 2, 0) for t in xs)

    def step(carry, xc):
        c_mat, n_vec, m = carry
        k_c, v_c, bl_c, wa_c, ml_c = xc[:5]
        h = None
        if q is not None:
            q_c, b_c, num_in, den_in, m_in_c = xc[5:]
            m_inter = b_c + m[..., None]
            m_t = jnp.maximum(m_inter, m_in_c)
            s_inter, s_in = jnp.exp(m_inter - m_t), jnp.exp(m_in_c - m_t)
            num = s_inter[..., None] * jnp.einsum('bhvk,bhik->bhiv', c_mat, q_c) + s_in[..., None] * num_in
            den = s_inter * jnp.einsum('bhk,bhik->bhi', n_vec, q_c) + s_in * den_in
            h = num / jnp.maximum(jnp.abs(den), jnp.exp(-m_t))[..., None]
        m_new = jnp.maximum(bl_c + m, ml_c)
        f_old, f_loc = jnp.exp(bl_c + m - m_new), jnp.exp(ml_c - m_new)
        kw = k_c * wa_c[..., None]
        c_mat = f_old[..., None, None] * c_mat + f_loc[..., None, None] * jnp.einsum('bhiv,bhik->bhvk', v_c, kw)
        n_vec = f_old[..., None] * n_vec + f_loc[..., None] * jnp.sum(kw, axis=-2)
        return (c_mat, n_vec, m_new), h

    state, hs = lax.scan(step, state, xs)
    return (unchunk(hs) if q is not None else None), state


def mlstm_prep(z, i_bias, f_bias, with_q):
    k = to_heads(z['mk'], MLSTM_HEADS).astype(F32) * MLSTM_DH ** -0.5
    v = to_heads(z['mv'], MLSTM_HEADS).astype(F32)
    log_i = dir_heads(z['mi'].astype(F32) + i_bias.astype(F32), MLSTM_HEADS)
    log_f = jax.nn.log_sigmoid(dir_heads(z['mf'].astype(F32) + f_bias.astype(F32), MLSTM_HEADS))
    q = to_heads(z['mq'], MLSTM_HEADS).astype(F32) if with_q else None
    return q, k, v, log_i, log_f


def mlstm_mixer(zc, zx, i_bias, f_bias, norm_g, need_ctx):
    qc, kc, vc, ic, fc = mlstm_prep(zc, i_bias, f_bias, need_ctx)
    qx, kx, vx, ix, fx = mlstm_prep(zx, i_bias, f_bias, True)
    bsz = kx.shape[0]
    zero = (jnp.zeros((bsz, MLSTM_HEADS, MLSTM_DH, MLSTM_DH), F32),
            jnp.zeros((bsz, MLSTM_HEADS, MLSTM_DH), F32), jnp.zeros((bsz, MLSTM_HEADS), F32))
    h_ctx, h_lat = [], []
    for d in range(2):
        fl = flip_seq if d == 1 else keep_seq
        hc, st = mlstm_scan(fl(kc), fl(vc), fl(ic[d]), fl(fc[d]), zero, fl(qc))
        hx, _ = mlstm_scan(fl(kx), fl(vx), fl(ix[d]), fl(fx[d]), st, fl(qx))
        h_lat.append(fl(hx))
        if need_ctx:
            h_ctx.append(fl(hc))

    def finish(hs, z):
        h = from_heads(rms_norm(hs[0] + hs[1])) * norm_g.astype(F32)
        return (h * jax.nn.sigmoid(z['mo'].astype(F32))).astype(z['mo'].dtype)

    return (finish(h_ctx, zc) if need_ctx else None), finish(h_lat, zx)


def hyena_filters(length, f_w1, f_b1, f_w2, f_b2, f_w3, f_b3, f_w4, f_freq):
    t = jnp.arange(length, dtype=F32)
    t_norm = t / (length - 1)
    w = 2.0 * math.pi * t / length
    bands = jnp.linspace(1e-4, HYENA_BANDS - 1, HYENA_BANDS, dtype=F32)
    ang = w[:, None] * bands
    feat = jnp.concatenate([t_norm[:, None], jnp.cos(ang), -jnp.sin(ang)], axis=-1)
    freq = f_freq.astype(F32)
    hdn = jnp.sin(freq * (feat @ f_w1.astype(F32) + f_b1.astype(F32)))
    hdn = jnp.sin(freq * (hdn @ f_w2.astype(F32) + f_b2.astype(F32)))
    hdn = jnp.sin(freq * (hdn @ f_w3.astype(F32) + f_b3.astype(F32)))
    h = hdn @ f_w4.astype(F32)
    deltas = jnp.linspace(math.log(HYENA_TARGET) / HYENA_SLOW_DECAY, math.log(HYENA_TARGET) / HYENA_FAST_DECAY,
                          HYENA_NFILT, dtype=F32)
    h = (h * jnp.exp(-t_norm[:, None] * jnp.abs(deltas))).reshape(length, HYENA_ORDER, 2, HYENA_W)
    h_fwd, h_bwd = h[:, :, 0], h[1:, :, 1]
    l1 = jnp.sum(jnp.abs(h_fwd), axis=0) + jnp.sum(jnp.abs(h_bwd), axis=0)
    gap = jnp.zeros((1, HYENA_ORDER, HYENA_W), F32)
    return jnp.concatenate([h_fwd, gap, jnp.flip(h_bwd, axis=0)], axis=0) / l1


def fft_long_conv(u, k_freq, d_skip):
    length = u.shape[1]
    uf = jnp.fft.rfft(u.astype(F32), n=2 * length, axis=1)
    y = jnp.fft.irfft(uf * k_freq, n=2 * length, axis=1)[:, :length]
    return (y + u.astype(F32) * d_skip.astype(F32)).astype(u.dtype)


def hyena_mixer(z, conv_w, conv_b, f_w1, f_b1, f_w2, f_b2, f_w3, f_b3, f_w4, f_freq, d_skip):
    u = dwconv(z, conv_w, conv_b)
    v, x1, x2 = u[..., :HYENA_W], u[..., HYENA_W:2 * HYENA_W], u[..., 2 * HYENA_W:]
    k_freq = jnp.fft.rfft(hyena_filters(z.shape[1], f_w1, f_b1, f_w2, f_b2, f_w3, f_b3, f_w4, f_freq), axis=0)
    y = x1 * fft_long_conv(v, k_freq[:, 0], d_skip[0])
    return x2 * fft_long_conv(y, k_freq[:, 1], d_skip[1])


def even_mixer(hc, hx, w_in, i_bias, f_bias, m_norm_g, h_conv_w, h_conv_b, f_w1, f_b1, f_w2, f_b2, f_w3, f_b3,
               f_w4, f_freq, h_d, w_out, need_ctx):
    zx = project(hx, w_in, EVEN_LAYOUT)
    zc = project(hc, w_in, EVEN_LAYOUT, None if need_ctx else EVEN_CTX_STATE)
    mc, mx = mlstm_mixer(zc, zx, i_bias, f_bias, m_norm_g, need_ctx)
    filt = (f_w1, f_b1, f_w2, f_b2, f_w3, f_b3, f_w4, f_freq)
    yx = hyena_mixer(zx['hy'], h_conv_w, h_conv_b, *filt, h_d)
    out_x = jnp.concatenate([mx, yx], axis=-1) @ w_out
    out_c = None
    if need_ctx:
        yc = hyena_mixer(zc['hy'], h_conv_w, h_conv_b, *filt, h_d)
        out_c = jnp.concatenate([mc, yc], axis=-1) @ w_out
    return out_c, out_x


def gdn_scan(k, v, log_a, beta, state, q=None):
    k, v, la, bt = chunked(k), chunked(v), chunked(log_a), chunked(beta)
    g = jnp.cumsum(la, axis=-1)
    idx = jnp.arange(CHUNK)
    lower = idx[:, None] >= idx[None, :]
    strict = idx[:, None] > idx[None, :]
    decay = jnp.exp(jnp.where(lower, g[..., :, None] - g[..., None, :], -jnp.inf))
    kb = k * bt[..., None]
    a_mat = jnp.where(strict, jnp.einsum('bhcid,bhcjd->bhcij', kb, k) * decay, 0.0)
    eye = jnp.eye(CHUNK, dtype=a_mat.dtype)
    tmat = lax.linalg.triangular_solve(a_mat + eye, jnp.broadcast_to(eye, a_mat.shape), left_side=True,
                                       lower=True, unit_diagonal=True)
    u = tmat @ (v * bt[..., None])
    w = tmat @ (kb * jnp.exp(g)[..., None])
    g_last = g[..., -1]
    kd = k * jnp.exp(g_last[..., None] - g)[..., None]
    xs = (u, w, kd, g_last)
    if q is not None:
        q = chunked(q)
        attn = jnp.einsum('bhcid,bhcjd->bhcij', q, k) * decay
        xs = xs + (attn, q * jnp.exp(g)[..., None])
    xs = tuple(jnp.moveaxis(t, 2, 0) for t in xs)

    def step(s_mat, xc):
        u_c, w_c, kd_c, gl_c = xc[:4]
        v_new = u_c - jnp.einsum('bhik,bhkv->bhiv', w_c, s_mat)
        o = None
        if q is not None:
            attn_c, qg_c = xc[4:]
            o = jnp.einsum('bhik,bhkv->bhiv', qg_c, s_mat) + jnp.einsum('bhij,bhjv->bhiv', attn_c, v_new)
        s_mat = s_mat * jnp.exp(gl_c)[..., None, None] + jnp.einsum('bhik,bhiv->bhkv', kd_c, v_new)
        return s_mat, o

    state, os_ = lax.scan(step, state, xs)
    return (unchunk(os_) if q is not None else None), state


def gdn_prep(z, conv_w, conv_b, a_log, dt_bias, with_q):
    wk, wv = GDN_HEADS * GDN_DK, GDN_HEADS * GDN_DV
    conv = lambda t, lo, hi: jax.nn.silu(dwconv(t, conv_w[:, lo:hi], conv_b[lo:hi]))
    k = l2_norm(to_heads(conv(z['gk'], wk, 2 * wk), GDN_HEADS)).astype(F32)
    v = to_heads(conv(z['gv'], 2 * wk, 2 * wk + wv), GDN_HEADS).astype(F32)
    beta = jax.nn.sigmoid(dir_heads(z['gb'].astype(F32), GDN_HEADS))
    b, l, _ = z['ga'].shape
    a = z['ga'].astype(F32).reshape(b, l, 2, GDN_HEADS)
    log_a = (-jnp.exp(a_log.astype(F32)) * jax.nn.softplus(a + dt_bias.astype(F32))).transpose(2, 0, 3, 1)
    q = None
    if with_q:
        q = l2_norm(to_heads(conv(z['gq'], 0, wk), GDN_HEADS)).astype(F32) * GDN_DK ** -0.5
    return q, k, v, log_a, beta


def gdn_mixer(zc, zx, conv_w, conv_b, a_log, dt_bias, norm_g, need_ctx):
    qc, kc, vc, ac, bc = gdn_prep(zc, conv_w, conv_b, a_log, dt_bias, need_ctx)
    qx, kx, vx, ax, bx = gdn_prep(zx, conv_w, conv_b, a_log, dt_bias, True)
    zero = jnp.zeros((kx.shape[0], GDN_HEADS, GDN_DK, GDN_DV), F32)
    o_ctx, o_lat = [], []
    for d in range(2):
        fl = flip_seq if d == 1 else keep_seq
        oc, st = gdn_scan(fl(kc), fl(vc), fl(ac[d]), fl(bc[d]), zero, fl(qc))
        ox, _ = gdn_scan(fl(kx), fl(vx), fl(ax[d]), fl(bx[d]), st, fl(qx))
        o_lat.append(fl(ox))
        if need_ctx:
            o_ctx.append(fl(oc))

    def finish(os_, z):
        o = from_heads(rms_norm(os_[0] + os_[1]) * norm_g.astype(F32))
        return (o * jax.nn.silu(z['gg'].astype(F32))).astype(z['gg'].dtype)

    return (finish(o_ctx, zc) if need_ctx else None), finish(o_lat, zx)


def mla_mixer(zc, zx, q_norm_g, w_q_up, kv_norm_g, w_kv_up, qn_g, kn_g, rope_cos, rope_sin, need_ctx):
    def keys_values(z, rope):
        ckv = rms_norm(z['lkv']) * kv_norm_g
        kv = to_heads(ckv @ w_kv_up, MLA_HEADS)
        k_rope = jnp.broadcast_to(z['lkr'][:, None], kv.shape[:3] + (MLA_ROPE,))
        k = rms_norm(jnp.concatenate([kv[..., :MLA_NOPE], k_rope], axis=-1)) * kn_g
        return (k if rope is None else rope_tail(k, *rope)), kv[..., MLA_NOPE:]

    def queries(z, rope):
        cq = rms_norm(z['lq']) * q_norm_g
        q = rms_norm(to_heads(cq @ w_q_up, MLA_HEADS)) * qn_g
        return q if rope is None else rope_tail(q, *rope)

    rope = (rope_cos, rope_sin)
    kc, vc = keys_values(zc, None)
    kx, vx = keys_values(zx, rope)
    keys = jnp.concatenate([kc, kx], axis=2)
    vals = jnp.concatenate([vc, vx], axis=2)
    out_x = from_heads(block_attention(queries(zx, rope), keys, vals))
    out_c = from_heads(block_attention(queries(zc, None), kc, vc)) if need_ctx else None
    return out_c, out_x


def odd_mixer(hc, hx, w_in, conv_w, conv_b, a_log, dt_bias, gdn_norm_g, q_norm_g, w_q_up, kv_norm_g, w_kv_up,
              qn_g, kn_g, w_out, rope_cos, rope_sin, need_ctx):
    zx = project(hx, w_in, ODD_LAYOUT)
    zc = project(hc, w_in, ODD_LAYOUT, None if need_ctx else ODD_CTX_STATE)
    gc, gx = gdn_mixer(zc, zx, conv_w, conv_b, a_log, dt_bias, gdn_norm_g, need_ctx)
    ac, ax = mla_mixer(zc, zx, q_norm_g, w_q_up, kv_norm_g, w_kv_up, qn_g, kn_g, rope_cos, rope_sin, need_ctx)
    out_x = jnp.concatenate([gx, ax], axis=-1) @ w_out
    out_c = jnp.concatenate([gc, ac], axis=-1) @ w_out if need_ctx else None
    return out_c, out_x


def conv_ffn(h, w_up, conv_w, conv_b, w_down):
    u = dwconv(h @ w_up, conv_w, conv_b)
    return (jax.nn.silu(u[..., :D_FF]) * u[..., D_FF:]) @ w_down


def setup_inputs(seed: int = 0) -> dict:
    key = jax.random.key(seed)
    ks = iter(jax.random.split(key, 48))

    def nrm(shape, scale=1.0):
        return scale * jax.random.normal(next(ks), shape, F32)

    def gain(shape):
        return 1.0 + nrm(shape, 0.1)

    D = D_MODEL
    H = MLSTM_HEADS
    f_bias = jnp.linspace(3.0, 6.0, 2 * H, dtype=F32)[None] + nrm((N_EVEN, 2 * H), 0.1)
    a_log = jnp.log(jax.random.uniform(next(ks), (N_ODD, 2, GDN_HEADS), F32, 1.0, 16.0))
    dt = jnp.exp(jax.random.uniform(next(ks), (N_ODD, 2, GDN_HEADS), F32, math.log(1e-3), math.log(1e-1)))
    dt_bias = dt + jnp.log(-jnp.expm1(-dt))
    return {
        'x': nrm((BATCH, SEQ, D)),
        'c': nrm((BATCH, D)),
        'ctx': nrm((BATCH, CTX_LEN, D)),
        'c_ctx': nrm((D,)),
        'mod_w': nrm((DEPTH, D, 6 * D), D ** -0.5),
        'mod_b': nrm((DEPTH, 6 * D), 0.02),
        'a_w_in': nrm((N_EVEN, D, EVEN_IN), D ** -0.5),
        'a_i_bias': nrm((N_EVEN, 2 * H), 0.1),
        'a_f_bias': f_bias,
        'a_norm_g': gain((N_EVEN, MLSTM_W)),
        'b_conv_w': nrm((N_EVEN, HYENA_SHORT, (HYENA_ORDER + 1) * HYENA_W), HYENA_SHORT ** -0.5),
        'b_conv_b': nrm((N_EVEN, (HYENA_ORDER + 1) * HYENA_W), 0.02),
        'b_f_w1': nrm((N_EVEN, HYENA_EMB, HYENA_FFN), HYENA_EMB ** -0.5),
        'b_f_b1': nrm((N_EVEN, HYENA_FFN), 0.1),
        'b_f_w2': nrm((N_EVEN, HYENA_FFN, HYENA_FFN), HYENA_FFN ** -0.5),
        'b_f_b2': nrm((N_EVEN, HYENA_FFN), 0.1),
        'b_f_w3': nrm((N_EVEN, HYENA_FFN, HYENA_FFN), HYENA_FFN ** -0.5),
        'b_f_b3': nrm((N_EVEN, HYENA_FFN), 0.1),
        'b_f_w4': nrm((N_EVEN, HYENA_FFN, HYENA_NFILT), HYENA_FFN ** -0.5),
        'b_f_freq': gain((N_EVEN, HYENA_FFN)),
        'b_d': nrm((N_EVEN, HYENA_ORDER, HYENA_W), 0.1),
        'ab_w_out': nrm((N_EVEN, EVEN_MIX, D), EVEN_MIX ** -0.5),
        'cd_w_in': nrm((N_ODD, D, ODD_IN), D ** -0.5),
        'c_conv_w': nrm((N_ODD, GDN_CONV, GDN_HEADS * (2 * GDN_DK + GDN_DV)), GDN_CONV ** -0.5),
        'c_conv_b': nrm((N_ODD, GDN_HEADS * (2 * GDN_DK + GDN_DV)), 0.02),
        'c_a_log': a_log,
        'c_dt_bias': dt_bias,
        'c_norm_g': gain((N_ODD, GDN_DV)),
        'd_q_norm_g': gain((N_ODD, MLA_Q_RANK)),
        'd_w_q_up': nrm((N_ODD, MLA_Q_RANK, MLA_HEADS * MLA_QK), MLA_Q_RANK ** -0.5),
        'd_kv_norm_g': gain((N_ODD, MLA_KV_RANK)),
        'd_w_kv_up': nrm((N_ODD, MLA_KV_RANK, MLA_HEADS * (MLA_NOPE + MLA_V)), MLA_KV_RANK ** -0.5),
        'd_qn_g': gain((N_ODD, MLA_QK)),
        'd_kn_g': gain((N_ODD, MLA_QK)),
        'cd_w_out': nrm((N_ODD, ODD_MIX, D), ODD_MIX ** -0.5),
        'ffn_w_up': nrm((DEPTH, D, 2 * D_FF), D ** -0.5),
        'ffn_conv_w': nrm((DEPTH, FFN_CONV, 2 * D_FF), FFN_CONV ** -0.5),
        'ffn_conv_b': nrm((DEPTH, 2 * D_FF), 0.02),
        'ffn_w_down': nrm((DEPTH, D_FF, D), D_FF ** -0.5),
    }


def reference(x, c, ctx, c_ctx, mod_w, mod_b, a_w_in, a_i_bias, a_f_bias, a_norm_g, b_conv_w, b_conv_b,
              b_f_w1, b_f_b1, b_f_w2, b_f_b2, b_f_w3, b_f_b3, b_f_w4, b_f_freq, b_d, ab_w_out,
              cd_w_in, c_conv_w, c_conv_b, c_a_log, c_dt_bias, c_norm_g, d_q_norm_g, d_w_q_up, d_kv_norm_g,
              d_w_kv_up, d_qn_g, d_kn_g, cd_w_out, ffn_w_up, ffn_conv_w, ffn_conv_b, ffn_w_down):
    rows = x.shape[1] // GRID_W
    rope_cos, rope_sin = axial_rope(rows, x.dtype)
    s_c = jax.nn.silu(c)
    s_ctx = jax.nn.silu(c_ctx)
    for layer in range(DEPTH):
        last = layer == DEPTH - 1
        need_ctx = not last
        mods_x = jnp.split((s_c @ mod_w[layer] + mod_b[layer])[:, None, :], 6, axis=-1)
        mods_c = jnp.split(s_ctx @ mod_w[layer] + mod_b[layer], 6, axis=-1)
        hx = modulate(x, mods_x[0], mods_x[1])
        hc = modulate(ctx, mods_c[0], mods_c[1])
        e = layer // 2
        if layer % 2 == 0:
            out_c, out_x = even_mixer(hc, hx, a_w_in[e], a_i_bias[e], a_f_bias[e], a_norm_g[e], b_conv_w[e],
                                      b_conv_b[e], b_f_w1[e], b_f_b1[e], b_f_w2[e], b_f_b2[e], b_f_w3[e],
                                      b_f_b3[e], b_f_w4[e], b_f_freq[e], b_d[e], ab_w_out[e], need_ctx)
        else:
            out_c, out_x = odd_mixer(hc, hx, cd_w_in[e], c_conv_w[e], c_conv_b[e], c_a_log[e], c_dt_bias[e],
                                     c_norm_g[e], d_q_norm_g[e], d_w_q_up[e], d_kv_norm_g[e], d_w_kv_up[e],
                                     d_qn_g[e], d_kn_g[e], cd_w_out[e], rope_cos, rope_sin, need_ctx)
        x = x + mods_x[2] * out_x
        x = x + mods_x[5] * conv_ffn(modulate(x, mods_x[3], mods_x[4]), ffn_w_up[layer], ffn_conv_w[layer],
                                     ffn_conv_b[layer], ffn_w_down[layer])
        if need_ctx:
            ctx = ctx + mods_c[2] * out_c
            ctx = ctx + mods_c[5] * conv_ffn(modulate(ctx, mods_c[3], mods_c[4]), ffn_w_up[layer],
                                             ffn_conv_w[layer], ffn_conv_b[layer], ffn_w_down[layer])
    return x
```

```python
import functools
import math

import numpy as np
import jax
import jax.numpy as jnp
from jax import lax
from jax.experimental import pallas as pl
from jax.experimental.pallas import tpu as pltpu

F32 = jnp.float32
BF16 = jnp.bfloat16
HI = lax.Precision.HIGHEST

D_MODEL = 1024
GRID_W = 64
EPS = 1e-6
CHUNK = 64
ROPE_THETA = 10000.0
N_HEADS = 4
DH = 128
MIX_W = N_HEADS * DH
HYENA_W = 512
HYENA_EMB = 33
HYENA_BANDS = 16
HYENA_FFN = 64
HYENA_NFILT = 4 * HYENA_W
HYENA_FAST_DECAY = 0.3
HYENA_SLOW_DECAY = 1.5
HYENA_TARGET = 1e-2
MLA_NOPE = 128
MLA_ROPE = 64
MLA_QK = 192
MLA_QK_PAD = 256
MLA_Q_RANK = 384
MLA_KV_RANK = 256
D_FF = 2816
FF_TILE = 256
N_FF_TILES = D_FF // FF_TILE
FFT_N1 = 128
VMEM_LIMIT = 56 * 1024 * 1024

E_QKV, E_HY, E_MO, E_GATE, E_N = 0, 1536, 3072, 3584, 3712
O_QKV, O_GG, O_LKV, O_LQ, O_LKR, O_GATE, O_N = 0, 1536, 2048, 2304, 2688, 2816, 2944


def _cparams(sem):
    return pltpu.CompilerParams(dimension_semantics=sem, vmem_limit_bytes=VMEM_LIMIT)


def _dot(a, b):
    return jnp.dot(a.astype(BF16), b.astype(BF16), preferred_element_type=F32)


def _dot_nt(a, b):
    return lax.dot_general(a.astype(BF16), b.astype(BF16), (((1,), (1,)), ((), ())), preferred_element_type=F32)


def _dot_tn(a, b):
    return lax.dot_general(a.astype(BF16), b.astype(BF16), (((0,), (0,)), ((), ())), preferred_element_type=F32)


def _dot_hi(a, b):
    return jnp.dot(a, b, precision=HI, preferred_element_type=F32)


def _rms(x, n=None):
    n = x.shape[-1] if n is None else n
    return x * lax.rsqrt(jnp.sum(x * x, axis=-1, keepdims=True) * (1.0 / n) + EPS)


def _sigmoid(x):
    return 1.0 / (1.0 + jnp.exp(-x))


def _silu(x):
    return x * _sigmoid(x)


def _softplus(x):
    return jnp.maximum(x, 0.0) + jnp.log(1.0 + jnp.exp(-jnp.abs(x)))


def _log_sigmoid(x):
    return -_softplus(-x)


def _mods_body(c_ref, w_ref, b_ref, o_ref):
    o_ref[0] = _dot_hi(_silu(c_ref[...]), w_ref[0]) + b_ref[0]


def mods_call(cvec, mod_w, mod_b):
    depth, d, n = mod_w.shape
    tn = 1536
    return pl.pallas_call(
        _mods_body,
        grid=(depth, n // tn),
        in_specs=[pl.BlockSpec((8, d), lambda l, j: (0, 0)),
                  pl.BlockSpec((1, d, tn), lambda l, j: (l, 0, j)),
                  pl.BlockSpec((1, 1, tn), lambda l, j: (l, 0, j))],
        out_specs=pl.BlockSpec((1, 8, tn), lambda l, j: (l, 0, j)),
        out_shape=jax.ShapeDtypeStruct((depth, 8, n), F32),
        compiler_params=_cparams(("parallel", "parallel")),
        name="mods",
    )(cvec, mod_w, mod_b.reshape(depth, 1, n))


def _inproj_body(x_ref, sh_ref, sc_ref, w_ref, o_ref):
    h = _rms(x_ref[0]) * (1.0 + sc_ref[0]) + sh_ref[0]
    o_ref[0] = _dot(h, w_ref[...])


def inproj_call(x, shift, scale, w):
    bn, t, d = x.shape
    n = w.shape[1]
    tm = min(256, t)
    return pl.pallas_call(
        _inproj_body,
        grid=(bn, t // tm),
        in_specs=[pl.BlockSpec((1, tm, d), lambda b, i: (b, i, 0)),
                  pl.BlockSpec((1, 1, d), lambda b, i: (b, 0, 0)),
                  pl.BlockSpec((1, 1, d), lambda b, i: (b, 0, 0)),
                  pl.BlockSpec((d, n), lambda b, i: (0, 0))],
        out_specs=pl.BlockSpec((1, tm, n), lambda b, i: (b, i, 0)),
        out_shape=jax.ShapeDtypeStruct((bn, t, n), F32),
        compiler_params=_cparams(("parallel", "parallel")),
        name="inproj",
    )(x, shift, scale, w)


def _shift_rows(u, prev_row, next_row):
    tm = u.shape[0]
    rows = lax.broadcasted_iota(jnp.int32, (tm, 1), 0)
    um = jnp.where(rows == 0, prev_row, pltpu.roll(u, 1, 0))
    up = jnp.where(rows == tm - 1, next_row, pltpu.roll(u, tm - 1, 0))
    return um, up


def _dwconv_body(gdn, z_ref, zp_ref, zn_ref, w_ref, b_ref, o_ref):
    i, nt = pl.program_id(1), pl.num_programs(1)
    u = z_ref[0]
    prev_row = zp_ref[0][7:8, :] * (i > 0).astype(F32)
    next_row = zn_ref[0][0:1, :] * (i < nt - 1).astype(F32)
    um, up = _shift_rows(u, prev_row, next_row)
    y = um * w_ref[0:1, :] + u * w_ref[1:2, :] + up * w_ref[2:3, :] + b_ref[...]
    if not gdn:
        o_ref[0] = y
        return
    y = _silu(y)
    for h in range(2 * N_HEADS):
        s = y[:, h * DH:(h + 1) * DH]
        s = s * lax.rsqrt(jnp.sum(s * s, axis=-1, keepdims=True) + EPS)
        if h < N_HEADS:
            s = s * DH ** -0.5
        o_ref[0, :, h * DH:(h + 1) * DH] = s
    o_ref[0, :, 2 * MIX_W:] = y[:, 2 * MIX_W:]


def dwconv_call(z, col_block, w, b, gdn):
    bn, t, _ = z.shape
    c = w.shape[1]
    tm = min(512, t)
    r8 = tm // 8
    return pl.pallas_call(
        functools.partial(_dwconv_body, gdn),
        grid=(bn, t // tm),
        in_specs=[pl.BlockSpec((1, tm, c), lambda b_, i: (b_, i, col_block)),
                  pl.BlockSpec((1, 8, c), lambda b_, i: (b_, jnp.maximum(i * r8 - 1, 0), col_block)),
                  pl.BlockSpec((1, 8, c), lambda b_, i: (b_, jnp.minimum((i + 1) * r8, t // 8 - 1), col_block)),
                  pl.BlockSpec((3, c), lambda b_, i: (0, 0)),
                  pl.BlockSpec((1, c), lambda b_, i: (0, 0))],
        out_specs=pl.BlockSpec((1, tm, c), lambda b_, i: (b_, i, 0)),
        out_shape=jax.ShapeDtypeStruct((bn, t, c), F32),
        compiler_params=_cparams(("parallel", "parallel")),
        name="dwconv_gdn" if gdn else "dwconv",
    )(z, z, z, w, b.reshape(1, c))


def _scan_masks(d):
    ii = lax.broadcasted_iota(jnp.int32, (CHUNK, CHUNK), 0)
    jj = lax.broadcasted_iota(jnp.int32, (CHUNK, CHUNK), 1)
    sgn = 1 - 2 * d
    return (ii - jj) * sgn >= 0, (jj - ii) * sgn >= 0, ii == jj


def _mlstm_chunk(q, k, v, li_c, lf_c, li_r, lf_r, vis, vis_t, c_mat, n_vec, m):
    visf, vistf = vis.astype(F32), vis_t.astype(F32)
    b_c = jnp.sum(visf * lf_r, axis=1, keepdims=True)
    b_r = jnp.sum(vistf * lf_c, axis=0, keepdims=True)
    b_last = jnp.sum(lf_r, axis=1, keepdims=True)
    a_c = b_last - b_c + li_c
    m_loc = jnp.max(a_c, axis=0, keepdims=True)
    wa_c = jnp.exp(a_c - m_loc)
    dmat = jnp.where(vis, b_c - b_r + li_r, -jnp.inf)
    m_in = jnp.max(dmat, axis=1, keepdims=True)
    s = _dot_nt(q, k) * jnp.exp(dmat - m_in)
    num_in = _dot(s, v)
    den_in = jnp.sum(s, axis=1, keepdims=True)
    m_inter = b_c + m
    m_t = jnp.maximum(m_inter, m_in)
    s_inter, s_in = jnp.exp(m_inter - m_t), jnp.exp(m_in - m_t)
    num = s_inter * _dot_nt(q, c_mat) + s_in * num_in
    den = s_inter * jnp.sum(q * n_vec, axis=1, keepdims=True) + s_in * den_in
    h = num / jnp.maximum(jnp.abs(den), jnp.exp(-m_t))
    m_new = jnp.maximum(b_last + m, m_loc)
    f_old, f_loc = jnp.exp(b_last + m - m_new), jnp.exp(m_loc - m_new)
    kw = k * wa_c
    c_mat = f_old * c_mat + f_loc * _dot_tn(v, kw)
    n_vec = f_old * n_vec + f_loc * jnp.sum(kw, axis=0, keepdims=True)
    return h, c_mat, n_vec, m_new


def _mlstm_body(q_ref, k_ref, v_ref, gc_ref, gr_ref, bias_ref, c0_ref, n0_ref, m0_ref,
                h_ref, c_ref, n_ref, m_ref):
    d, i = pl.program_id(1), pl.program_id(2)
    ncb = q_ref.shape[1] // CHUNK

    @pl.when(i == 0)
    def _():
        c_ref[...] = c0_ref[...]
        n_ref[...] = n0_ref[...]
        m_ref[...] = m0_ref[...]

    vis, vis_t, _ = _scan_masks(d)

    def chunk_step(cc, carry):
        c = cc + d * (ncb - 1 - 2 * cc)
        r0 = pl.multiple_of(c * CHUNK, CHUNK)
        gc = gc_ref[0, 0, pl.ds(r0, CHUNK), :]
        gr = gr_ref[0, 0, c]
        for h in range(N_HEADS):
            hs = slice(h * DH, (h + 1) * DH)
            bi = bias_ref[0, :, h:h + 1]
            bf = bias_ref[0, :, N_HEADS + h:N_HEADS + h + 1]
            li_c = gc[:, h:h + 1] + bi
            lf_c = _log_sigmoid(gc[:, N_HEADS + h:N_HEADS + h + 1] + bf)
            li_r = gr[h:h + 1, :] + bi
            lf_r = _log_sigmoid(gr[N_HEADS + h:N_HEADS + h + 1, :] + bf)
            q = q_ref[0, pl.ds(r0, CHUNK), hs]
            k = k_ref[0, pl.ds(r0, CHUNK), hs] * DH ** -0.5
            v = v_ref[0, pl.ds(r0, CHUNK), hs]
            hv, c_new, n_new, m_new = _mlstm_chunk(q, k, v, li_c, lf_c, li_r, lf_r, vis, vis_t,
                                                   c_ref[0, 0, h], n_ref[0, 0, h], m_ref[0, 0, h][:, 0:1])
            h_ref[0, 0, pl.ds(r0, CHUNK), hs] = hv
            c_ref[0, 0, h] = c_new
            n_ref[0, 0, h] = n_new
            m_ref[0, 0, h] = jnp.broadcast_to(m_new, (1, DH))
        return carry

    lax.fori_loop(0, ncb, chunk_step, 0)


def _gdn_chunk(q, k, v, bt_c, la_c, la_r, vis, vis_t, eye, s_mat):
    visf, vistf = vis.astype(F32), vis_t.astype(F32)
    g_c = jnp.sum(visf * la_r, axis=1, keepdims=True)
    g_r = jnp.sum(vistf * la_c, axis=0, keepdims=True)
    g_last = jnp.sum(la_r, axis=1, keepdims=True)
    decay = jnp.exp(jnp.where(vis, g_c - g_r, -jnp.inf))
    kb = k * bt_c
    a_mat = jnp.where(eye, 0.0, _dot_nt(kb, k) * decay)
    ii = lax.broadcasted_iota(jnp.int32, a_mat.shape, 0)
    jj = lax.broadcasted_iota(jnp.int32, a_mat.shape, 1)
    t_mat = eye.astype(F32)
    for lvl in range(CHUNK.bit_length() - 1):
        off = jnp.logical_and((ii >> (lvl + 1)) == (jj >> (lvl + 1)), (ii >> lvl) != (jj >> lvl))
        t_mat = t_mat - _dot(_dot(t_mat, jnp.where(off, a_mat, 0.0)), t_mat)
    eg = jnp.exp(g_c)
    u = _dot(t_mat, v * bt_c)
    w = _dot(t_mat, kb * eg)
    kd = k * jnp.exp(g_last - g_c)
    attn = _dot_nt(q, k) * decay
    v_new = u - _dot(w, s_mat)
    o = _dot(q * eg, s_mat) + _dot(attn, v_new)
    s_mat = s_mat * jnp.exp(g_last) + _dot_tn(kd, v_new)
    return o, s_mat


def _gdn_body(q_ref, k_ref, v_ref, gc_ref, gr_ref, dtb_ref, alog_ref, s0_ref, o_ref, s_ref):
    d, i = pl.program_id(1), pl.program_id(2)
    ncb = q_ref.shape[1] // CHUNK

    @pl.when(i == 0)
    def _():
        s_ref[...] = s0_ref[...]

    vis, vis_t, eye = _scan_masks(d)

    def chunk_step(cc, carry):
        c = cc + d * (ncb - 1 - 2 * cc)
        r0 = pl.multiple_of(c * CHUNK, CHUNK)
        gc = gc_ref[0, 0, pl.ds(r0, CHUNK), :]
        gr = gr_ref[0, 0, c]
        for h in range(N_HEADS):
            hs = slice(h * DH, (h + 1) * DH)
            dtb = dtb_ref[0, :, h:h + 1]
            neg_a = -jnp.exp(alog_ref[0, :, h:h + 1])
            bt_c = _sigmoid(gc[:, h:h + 1])
            la_c = neg_a * _softplus(gc[:, N_HEADS + h:N_HEADS + h + 1] + dtb)
            la_r = neg_a * _softplus(gr[N_HEADS + h:N_HEADS + h + 1, :] + dtb)
            q = q_ref[0, pl.ds(r0, CHUNK), hs]
            k = k_ref[0, pl.ds(r0, CHUNK), hs]
            v = v_ref[0, pl.ds(r0, CHUNK), hs]
            ov, s_new = _gdn_chunk(q, k, v, bt_c, la_c, la_r, vis, vis_t, eye, s_ref[0, 0, h])
            o_ref[0, 0, pl.ds(r0, CHUNK), hs] = ov
            s_ref[0, 0, h] = s_new
        return carry

    lax.fori_loop(0, ncb, chunk_step, 0)


def _scan_specs(t, tb, col0):
    nb = t // tb
    blk = lambda d, i: i + d * (nb - 1 - 2 * i)
    qkv = [pl.BlockSpec((1, tb, MIX_W), lambda b, d, i, j=j: (b, blk(d, i), col0 + j)) for j in range(3)]
    gates = [pl.BlockSpec((1, 1, tb, 8), lambda b, d, i: (b, d, blk(d, i), 0)),
             pl.BlockSpec((1, 1, tb // CHUNK, 8, CHUNK), lambda b, d, i: (b, d, blk(d, i), 0, 0))]
    out = pl.BlockSpec((1, 1, tb, MIX_W), lambda b, d, i: (d, b, blk(d, i), 0))
    return nb, qkv, gates, out


def _state_spec(*tail):
    return pl.BlockSpec((1, 1, N_HEADS) + tail, lambda b, d, i: (b, d, 0) + (0,) * len(tail))


def _gate_views(graw):
    bsz, t, _ = graw.shape
    gcol = graw.reshape(bsz, t, 2, 2, N_HEADS).transpose(0, 3, 1, 2, 4).reshape(bsz, 2, t, 8)
    grow = gcol.reshape(bsz, 2, t // CHUNK, CHUNK, 8).transpose(0, 1, 2, 4, 3)
    return gcol, grow


def mlstm_call(z, bias, state):
    bsz, t, _ = z.shape
    tb = min(256, t)
    nb, qkv, gates, out = _scan_specs(t, tb, E_QKV // MIX_W)
    gcol, grow = _gate_views(z[..., E_GATE:E_GATE + 16])
    st_specs = [_state_spec(DH, DH), _state_spec(1, DH), _state_spec(1, DH)]
    st_shapes = [jax.ShapeDtypeStruct((bsz, 2, N_HEADS, DH, DH), F32),
                 jax.ShapeDtypeStruct((bsz, 2, N_HEADS, 1, DH), F32),
                 jax.ShapeDtypeStruct((bsz, 2, N_HEADS, 1, DH), F32)]
    res = pl.pallas_call(
        _mlstm_body,
        grid=(bsz, 2, nb),
        in_specs=qkv + gates + [pl.BlockSpec((1, 1, 8), lambda b, d, i: (d, 0, 0))] + st_specs,
        out_specs=[out] + st_specs,
        out_shape=[jax.ShapeDtypeStruct((2, bsz, t, MIX_W), F32)] + st_shapes,
        compiler_params=_cparams(("parallel", "parallel", "arbitrary")),
        name="mlstm_scan",
    )(z, z, z, gcol, grow, bias, *state)
    return res[0], tuple(res[1:])


def gdn_call(qkv_arr, graw, dtb, alog, state):
    bsz, t, _ = qkv_arr.shape
    tb = min(256, t)
    nb, qkv, gates, out = _scan_specs(t, tb, 0)
    gcol, grow = _gate_views(graw)
    st_spec = _state_spec(DH, DH)
    par = pl.BlockSpec((1, 1, 8), lambda b, d, i: (d, 0, 0))
    o, s = pl.pallas_call(
        _gdn_body,
        grid=(bsz, 2, nb),
        in_specs=qkv + gates + [par, par, st_spec],
        out_specs=[out, st_spec],
        out_shape=[jax.ShapeDtypeStruct((2, bsz, t, MIX_W), F32),
                   jax.ShapeDtypeStruct((bsz, 2, N_HEADS, DH, DH), F32)],
        compiler_params=_cparams(("parallel", "parallel", "arbitrary")),
        name="gdn_scan",
    )(qkv_arr, qkv_arr, qkv_arr, gcol, grow, dtb, alog, state)
    return o, s


def _outproj_body(act, hf_ref, hb_ref, gate_ref, y_ref, g_ref, w_ref, x_ref, mg_ref, o_ref):
    hs = hf_ref[0, 0] + hb_ref[0, 0]
    hn = jnp.concatenate([_rms(hs[:, h * DH:(h + 1) * DH]) for h in range(N_HEADS)], axis=-1) * g_ref[...]
    gate = gate_ref[0]
    hn = hn * (_sigmoid(gate) if act == "sigmoid" else _silu(gate))
    out = _dot(hn, w_ref[0:MIX_W, :]) + _dot(y_ref[0], w_ref[MIX_W:, :])
    o_ref[0] = x_ref[0] + mg_ref[0] * out


def outproj_call(act, hdir, z, gate_block, y, g, w, x, mgate):
    bn, t, d = x.shape
    tm = min(512, t)
    return pl.pallas_call(
        functools.partial(_outproj_body, act),
        grid=(bn, t // tm),
        in_specs=[pl.BlockSpec((1, 1, tm, MIX_W), lambda b, i: (0, b, i, 0)),
                  pl.BlockSpec((1, 1, tm, MIX_W), lambda b, i: (1, b, i, 0)),
                  pl.BlockSpec((1, tm, MIX_W), lambda b, i: (b, i, gate_block)),
                  pl.BlockSpec((1, tm, MIX_W), lambda b, i: (b, i, 0)),
                  pl.BlockSpec((1, MIX_W), lambda b, i: (0, 0)),
                  pl.BlockSpec((2 * MIX_W, d), lambda b, i: (0, 0)),
                  pl.BlockSpec((1, tm, d), lambda b, i: (b, i, 0)),
                  pl.BlockSpec((1, 1, d), lambda b, i: (b, 0, 0))],
        out_specs=pl.BlockSpec((1, tm, d), lambda b, i: (b, i, 0)),
        out_shape=jax.ShapeDtypeStruct((bn, t, d), F32),
        compiler_params=_cparams(("parallel", "parallel")),
        name="outproj_" + act,
    )(hdir, hdir, z, y, g, w, x, mgate)


def _ffn_body(x_ref, xp_ref, xn_ref, sh_ref, sc_ref, mg_ref, wa_ref, wb_ref, cwa_ref, cwb_ref, wd_ref, o_ref):
    i, nt = pl.program_id(1), pl.num_programs(1)
    sc, sh = 1.0 + sc_ref[0], sh_ref[0]
    x = x_ref[0]
    h = (_rms(x) * sc + sh).astype(BF16)
    hp = (_rms(xp_ref[0]) * sc + sh).astype(BF16)
    hn = (_rms(xn_ref[0]) * sc + sh).astype(BF16)
    pv, nv = (i > 0).astype(F32), (i < nt - 1).astype(F32)

    def conv(w_ref, cw_ref, j):
        w = w_ref[j]
        u = jnp.dot(h, w, preferred_element_type=F32)
        prev_row = jnp.dot(hp, w, preferred_element_type=F32)[7:8, :] * pv
        next_row = jnp.dot(hn, w, preferred_element_type=F32)[0:1, :] * nv
        um, up = _shift_rows(u, prev_row, next_row)
        cw = cw_ref[j]
        return um * cw[0:1, :] + u * cw[1:2, :] + up * cw[2:3, :] + cw[3:4, :]

    def tile(j, acc):
        a, b = conv(wa_ref, cwa_ref, j), conv(wb_ref, cwb_ref, j)
        return acc + _dot(_silu(a) * b, wd_ref[j])

    acc = lax.fori_loop(0, N_FF_TILES, tile, jnp.zeros(x.shape, F32))
    o_ref[0] = x + mg_ref[0] * acc


def ffn_call(x, shift, scale, mgate, wa, wb, cwa, cwb, wd):
    bn, t, d = x.shape
    tm = min(512, t)
    r8 = tm // 8
    full = lambda a: pl.BlockSpec(a.shape, lambda b, i: (0,) * a.ndim)
    vec = pl.BlockSpec((1, 1, d), lambda b, i: (b, 0, 0))
    return pl.pallas_call(
        _ffn_body,
        grid=(bn, t // tm),
        in_specs=[pl.BlockSpec((1, tm, d), lambda b, i: (b, i, 0)),
                  pl.BlockSpec((1, 8, d), lambda b, i: (b, jnp.maximum(i * r8 - 1, 0), 0)),
                  pl.BlockSpec((1, 8, d), lambda b, i: (b, jnp.minimum((i + 1) * r8, t // 8 - 1), 0)),
                  vec, vec, vec, full(wa), full(wb), full(cwa), full(cwb), full(wd)],
        out_specs=pl.BlockSpec((1, tm, d), lambda b, i: (b, i, 0)),
        out_shape=jax.ShapeDtypeStruct((bn, t, d), F32),
        compiler_params=_cparams(("parallel", "parallel")),
        name="conv_ffn",
    )(x, x, x, shift, scale, mgate, wa, wb, cwa, cwb, wd)


def ffn_weights(w_up, conv_w, conv_b, w_down):
    d = w_up.shape[0]
    tiles = lambda a: a.reshape(a.shape[0], N_FF_TILES, FF_TILE).transpose(1, 0, 2)
    cw = jnp.concatenate([conv_w, conv_b[None]], axis=0)
    return (tiles(w_up[:, :D_FF]).astype(BF16), tiles(w_up[:, D_FF:]).astype(BF16),
            tiles(cw[:, :D_FF]), tiles(cw[:, D_FF:]),
            w_down.reshape(N_FF_TILES, FF_TILE, d).astype(BF16))


def _hy_filter_body(lf, w1t_ref, w1c_ref, w1s_ref, b1_ref, w2_ref, b2_ref, w3_ref, b3_ref, w4_ref, fr_ref,
                    band_ref, dl_ref, o_ref, l1_ref):
    ph, r = pl.program_id(0), pl.program_id(1)
    tr = o_ref.shape[0]
    n = (r * tr + lax.broadcasted_iota(jnp.int32, (tr, 1), 0))
    t = jnp.where(n < lf, n, 2 * lf - n).astype(F32)
    t_norm = t / (lf - 1)
    ang = (2.0 * math.pi * t / lf) * band_ref[...]
    fr = fr_ref[...]
    pre = t_norm * w1t_ref[...] + _dot_hi(jnp.cos(ang), w1c_ref[...]) - _dot_hi(jnp.sin(ang), w1s_ref[...])
    hdn = jnp.sin(fr * (pre + b1_ref[...]))
    hdn = jnp.sin(fr * (_dot_hi(hdn, w2_ref[...]) + b2_ref[...]))
    hdn = jnp.sin(fr * (_dot_hi(hdn, w3_ref[...]) + b3_ref[...]))
    hval = _dot_hi(hdn, w4_ref[0]) * jnp.exp(-t_norm * jnp.abs(dl_ref[0]))
    hval = jnp.where(n == lf, 0.0, hval)

    @pl.when(jnp.logical_and(ph == 0, r == 0))
    def _():
        l1_ref[...] = jnp.zeros_like(l1_ref)

    @pl.when(ph == 0)
    def _():
        l1_ref[...] += jnp.sum(jnp.abs(hval), axis=0, keepdims=True)

    @pl.when(ph == 1)
    def _():
        o_ref[...] = hval / l1_ref[...]


def hy_filter_call(lf, f_w1, f_b1, f_w2, f_b2, f_w3, f_b3, f_w4, f_freq):
    tr = min(512, lf)
    nt = 2 * lf // tr
    half = lf // tr
    ncol = 2 * HYENA_W
    pad = lambda a: jnp.concatenate([a, jnp.zeros((DH - a.shape[0], a.shape[1]), F32)], axis=0)
    w4 = f_w4.reshape(HYENA_FFN, 2, 2, HYENA_W).transpose(2, 0, 1, 3).reshape(2, HYENA_FFN, ncol)
    deltas = jnp.linspace(math.log(HYENA_TARGET) / HYENA_SLOW_DECAY, math.log(HYENA_TARGET) / HYENA_FAST_DECAY,
                          HYENA_NFILT, dtype=F32)
    deltas = deltas.reshape(2, 2, HYENA_W).transpose(1, 0, 2).reshape(2, 1, ncol)
    bands = jnp.linspace(1e-4, HYENA_BANDS - 1, HYENA_BANDS, dtype=F32)
    bands = jnp.concatenate([bands, jnp.zeros((DH - HYENA_BANDS,), F32)]).reshape(1, DH)
    row = lambda a: a.reshape(1, -1)
    args = (f_w1[0:1], pad(f_w1[1:1 + HYENA_BANDS]), pad(f_w1[1 + HYENA_BANDS:]), row(f_b1), f_w2, row(f_b2),
            f_w3, row(f_b3), w4, row(f_freq), bands, deltas)
    full = lambda a: pl.BlockSpec(a.shape, lambda p, r: (0,) * a.ndim)
    by_dir = lambda a: pl.BlockSpec((1,) + a.shape[1:], lambda p, r: (r // half, 0, 0))
    in_specs = [by_dir(a) if a.ndim == 3 else full(a) for a in args]
    return pl.pallas_call(
        functools.partial(_hy_filter_body, lf),
        grid=(2, nt),
        in_specs=in_specs,
        out_specs=pl.BlockSpec((tr, ncol), lambda p, r: (r * p, 0)),
        out_shape=jax.ShapeDtypeStruct((2 * lf, ncol), F32),
        scratch_shapes=[pltpu.VMEM((1, ncol), F32)],
        compiler_params=_cparams(("arbitrary", "arbitrary")),
        name="hyena_filter",
    )(*args)


def _dft_consts(lf):
    n1 = FFT_N1
    n = 2 * lf
    assert n == n1 * n1
    idx = np.arange(n1)
    f = np.exp(-2j * np.pi * np.outer(idx, idx) / n1)
    tw = np.exp(-2j * np.pi * np.outer(idx, idx) / n)
    fh = f[:, :n1 // 2]
    blk = lambda a: np.block([[a.real, -a.imag], [a.imag, a.real]])
    f_sig = blk(fh)
    f_flt = np.concatenate([f.real, f.imag], axis=0)
    c_inv = blk(np.conj(f)[:n1 // 2, :] / n)
    c = lambda a: jnp.asarray(a, dtype=F32)
    return dict(f_sig=c(f_sig), f_flt=c(f_flt), c_inv=c(c_inv), fr=c(f.real), fi=c(f.imag),
                twr=c(tw.real.reshape(n1, 1, n1)), twi=c(tw.imag.reshape(n1, 1, n1)))


def _lead_mm_body(m_ref, x_ref, o_ref):
    o_ref[...] = _dot_hi(m_ref[...], x_ref[...])


def lead_mm_call(mat, x2d, name):
    r, k = mat.shape
    nl = x2d.shape[1]
    tn = 2048
    return pl.pallas_call(
        _lead_mm_body,
        grid=(nl // tn,),
        in_specs=[pl.BlockSpec((r, k), lambda j: (0, 0)), pl.BlockSpec((k, tn), lambda j: (0, j))],
        out_specs=pl.BlockSpec((r, tn), lambda j: (0, j)),
        out_shape=jax.ShapeDtypeStruct((r, nl), F32),
        compiler_params=_cparams(("parallel",)),
        name=name,
    )(mat, x2d)


def _stage2_mat(fr, fi, twr, twi):
    hr = fr * twr - fi * twi
    hi = fr * twi + fi * twr
    return hr, hi


def _blk(ar, ai):
    return jnp.concatenate([jnp.concatenate([ar, -ai], axis=1), jnp.concatenate([ai, ar], axis=1)], axis=0)


def _spec_filter_body(a_ref, fr_ref, fi_ref, twr_ref, twi_ref, o_ref):
    hr, hi = _stage2_mat(fr_ref[...], fi_ref[...], twr_ref[0], twi_ref[0])
    n1 = hr.shape[0]
    o_ref[0] = _dot_hi(_blk(hr, hi), a_ref[...].reshape(2 * n1, a_ref.shape[-1]))


def _spec_conv_body(a_ref, k_ref, fr_ref, fi_ref, twr_ref, twi_ref, o_ref):
    hr, hi = _stage2_mat(fr_ref[...], fi_ref[...], twr_ref[0], twi_ref[0])
    n1 = hr.shape[0]
    x = _dot_hi(_blk(hr, hi), a_ref[...].reshape(2 * n1, a_ref.shape[-1]))
    xr, xi = x[:n1], x[n1:]
    kr, ki = k_ref[0, :n1], k_ref[0, n1:]
    y = jnp.concatenate([xr * kr - xi * ki, xr * ki + xi * kr], axis=0)
    o_ref[0] = _dot_hi(_blk(hr.T, -(hi.T)), y)


def spec_call(a2d, kspec, order, cst):
    n1 = FFT_N1
    c = a2d.shape[1] // n1
    ct = 512
    a4 = a2d.reshape(2, n1, n1, c)
    sq = pl.BlockSpec((n1, n1), lambda k, j: (0, 0))
    tw = pl.BlockSpec((1, 1, n1), lambda k, j: (k, 0, 0))
    a_spec = pl.BlockSpec((2, 1, n1, ct), lambda k, j: (0, k, 0, j))
    o_spec = pl.BlockSpec((1, 2 * n1, ct), lambda k, j: (k, 0, j))
    common = dict(grid=(n1, c // ct), out_specs=o_spec, out_shape=jax.ShapeDtypeStruct((n1, 2 * n1, c), F32),
                  compiler_params=_cparams(("parallel", "parallel")))
    if kspec is None:
        return pl.pallas_call(_spec_filter_body, in_specs=[a_spec, sq, sq, tw, tw], name="hyena_spec_filter",
                              **common)(a4, cst["fr"], cst["fi"], cst["twr"], cst["twi"])
    k_spec = pl.BlockSpec((1, 2 * n1, ct), lambda k, j: (k, 0, order * (c // ct) + j))
    return pl.pallas_call(_spec_conv_body, in_specs=[a_spec, k_spec, sq, sq, tw, tw], name="hyena_spec_conv",
                          **common)(a4, kspec, cst["fr"], cst["fi"], cst["twr"], cst["twi"])


def _inv_gate_body(m_ref, br_ref, bi_ref, v_ref, x_ref, d_ref, o_ref):
    y = _dot_hi(m_ref[...], jnp.concatenate([br_ref[...], bi_ref[...]], axis=0))
    o_ref[...] = x_ref[...] * (y + v_ref[...] * d_ref[...])


def inv_gate_call(b3, v2d, x2d, dvec, cst):
    n1 = FFT_N1
    nl = v2d.shape[1]
    tn = 2048
    nb = nl // tn
    b2d = b3.reshape(n1, 2 * nl)
    tile = pl.BlockSpec((n1, tn), lambda j: (0, j))
    return pl.pallas_call(
        _inv_gate_body,
        grid=(nb,),
        in_specs=[pl.BlockSpec((n1, 2 * n1), lambda j: (0, 0)), tile,
                  pl.BlockSpec((n1, tn), lambda j: (0, nb + j)), tile, tile,
                  pl.BlockSpec((1, tn), lambda j: (0, j))],
        out_specs=tile,
        out_shape=jax.ShapeDtypeStruct((n1, nl), F32),
        compiler_params=_cparams(("parallel",)),
        name="hyena_inv_gate",
    )(cst["c_inv"], b2d, b2d, v2d, x2d, dvec)


def hyena_long(u, kf, d_skip):
    bsz, lf, _ = u.shape
    assert bsz == 2
    n1 = FFT_N1
    cst = _dft_consts(lf)
    rb = lambda a: a.reshape(bsz * n1 // 2, n1 * HYENA_W)
    v, x1, x2 = (rb(u[..., j * HYENA_W:(j + 1) * HYENA_W]) for j in range(3))
    kspec = spec_call(lead_mm_call(cst["f_flt"], kf.reshape(n1, n1 * 2 * HYENA_W), "hyena_lead_filter"),
                      None, 0, cst)
    y = v
    for o, xg in enumerate((x1, x2)):
        a = lead_mm_call(cst["f_sig"], y, "hyena_lead_signal")
        b3 = spec_call(a, kspec, o, cst)
        y = inv_gate_call(b3, y, xg, jnp.tile(d_skip[o], n1).reshape(1, -1), cst)
    return y.reshape(bsz, lf, HYENA_W)


def _hyena_small_body(u_ref, kf_ref, fs_ref, ff_ref, ci_ref, d_ref, o_ref):
    w = HYENA_W
    nk = ff_ref.shape[0] // 2
    kspec = _dot_hi(ff_ref[...], kf_ref[...])
    ucat = jnp.concatenate([u_ref[0], u_ref[1]], axis=0)
    y = ucat[:, :w]
    for o in range(2):
        x = _dot_hi(fs_ref[...], y)
        xr, xi = x[:nk], x[nk:]
        kr, ki = kspec[:nk, o * w:(o + 1) * w], kspec[nk:, o * w:(o + 1) * w]
        prod = jnp.concatenate([xr * kr - xi * ki, xr * ki + xi * kr], axis=0)
        conv = _dot_hi(ci_ref[...], prod)
        y = ucat[:, (o + 1) * w:(o + 2) * w] * (conv + y * d_ref[o:o + 1, :])
    lf = u_ref.shape[1]
    o_ref[0] = y[:lf]
    o_ref[1] = y[lf:]


def hyena_small(u, kf, d_skip):
    bsz, lf, _ = u.shape
    assert bsz == 2
    n = 2 * lf
    idx = np.arange(n)
    f = np.exp(-2j * np.pi * np.outer(idx, idx) / n)
    blk = lambda a: np.block([[a.real, -a.imag], [a.imag, a.real]])
    fs = jnp.asarray(blk(f[:, :lf]), F32)
    ff = jnp.asarray(np.concatenate([f.real, f.imag], axis=0), F32)
    ci = jnp.asarray(blk(np.conj(f)[:lf, :] / n), F32)
    return pl.pallas_call(
        _hyena_small_body,
        out_shape=jax.ShapeDtypeStruct((bsz, lf, HYENA_W), F32),
        compiler_params=pltpu.CompilerParams(vmem_limit_bytes=VMEM_LIMIT),
        name="hyena_small",
    )(u, kf, fs, ff, ci, d_skip)


def _rope(x, cos_ref, sa_ref, sb_ref):
    w = x.shape[-1]
    half = MLA_ROPE // 2
    return x * cos_ref[...] + pltpu.roll(x, w - half, 1) * sa_ref[...] + pltpu.roll(x, half, 1) * sb_ref[...]


def _kv_body(rope, lkv_ref, lkr_ref, g_ref, w_ref, kg_ref, cos_ref, sa_ref, sb_ref, k_ref, v_ref):
    kv = _dot(_rms(lkv_ref[0]) * g_ref[...], w_ref[...])
    kr = lkr_ref[0]
    for h in range(N_HEADS):
        base = h * (MLA_NOPE + DH)
        k = jnp.concatenate([kv[:, base:base + MLA_NOPE], kr], axis=-1)
        k = _rms(k, MLA_QK) * kg_ref[...]
        if rope:
            k = _rope(k, cos_ref, sa_ref, sb_ref)
        k_ref[0, h] = k.astype(BF16)
        v_ref[0, h] = kv[:, base + MLA_NOPE:base + MLA_NOPE + DH].astype(BF16)


def kv_call(z, rope, g, w, kg, tabs):
    bn, t, _ = z.shape
    tm = min(512, t)
    tab = pl.BlockSpec((tm, MLA_QK_PAD), lambda b, i: (i, 0))
    return pl.pallas_call(
        functools.partial(_kv_body, rope),
        grid=(bn, t // tm),
        in_specs=[pl.BlockSpec((1, tm, MLA_KV_RANK), lambda b, i: (b, i, O_LKV // MLA_KV_RANK)),
                  pl.BlockSpec((1, tm, DH), lambda b, i: (b, i, O_LKR // DH)),
                  pl.BlockSpec((1, MLA_KV_RANK), lambda b, i: (0, 0)),
                  pl.BlockSpec(w.shape, lambda b, i: (0, 0)),
                  pl.BlockSpec((1, MLA_QK_PAD), lambda b, i: (0, 0)), tab, tab, tab],
        out_specs=[pl.BlockSpec((1, N_HEADS, tm, MLA_QK_PAD), lambda b, i: (b, 0, i, 0)),
                   pl.BlockSpec((1, N_HEADS, tm, DH), lambda b, i: (b, 0, i, 0))],
        out_shape=[jax.ShapeDtypeStruct((bn, N_HEADS, t, MLA_QK_PAD), BF16),
                   jax.ShapeDtypeStruct((bn, N_HEADS, t, DH), BF16)],
        compiler_params=_cparams(("parallel", "parallel")),
        name="mla_kv_rope" if rope else "mla_kv",
    )(z, z, g, w, kg, *tabs)


def _q_body(lq_ref, g_ref, w_ref, qg_ref, cos_ref, sa_ref, sb_ref, q_ref):
    q = _dot(_rms(lq_ref[0]) * g_ref[...], w_ref[...])
    for h in range(N_HEADS):
        qh = _rms(q[:, h * MLA_QK_PAD:(h + 1) * MLA_QK_PAD], MLA_QK) * qg_ref[...]
        qh = _rope(qh, cos_ref, sa_ref, sb_ref) * MLA_QK ** -0.5
        q_ref[0, h] = qh.astype(BF16)


def q_call(z, g, w, qg, tabs):
    bn, t, _ = z.shape
    tm = min(512, t)
    tab = pl.BlockSpec((tm, MLA_QK_PAD), lambda b, i: (i, 0))
    return pl.pallas_call(
        _q_body,
        grid=(bn, t // tm),
        in_specs=[pl.BlockSpec((1, tm, MLA_Q_RANK), lambda b, i: (b, i, O_LQ // MLA_Q_RANK)),
                  pl.BlockSpec((1, MLA_Q_RANK), lambda b, i: (0, 0)),
                  pl.BlockSpec(w.shape, lambda b, i: (0, 0)),
                  pl.BlockSpec((1, MLA_QK_PAD), lambda b, i: (0, 0)), tab, tab, tab],
        out_specs=pl.BlockSpec((1, N_HEADS, tm, MLA_QK_PAD), lambda b, i: (b, 0, i, 0)),
        out_shape=jax.ShapeDtypeStruct((bn, N_HEADS, t, MLA_QK_PAD), BF16),
        compiler_params=_cparams(("parallel", "parallel")),
        name="mla_q",
    )(z, g, w, qg, *tabs)


def _attn_body(q_ref, kc_ref, vc_ref, kx_ref, vx_ref, o_ref, m_sc, l_sc, acc_sc):
    j, nj = pl.program_id(3), pl.num_programs(3)

    def update(k, v):
        s = lax.dot_general(q_ref[0, 0], k, (((1,), (1,)), ((), ())), preferred_element_type=F32)
        m_prev = m_sc[...]
        m_new = jnp.maximum(m_prev, jnp.max(s, axis=1, keepdims=True))
        alpha = jnp.exp(m_prev - m_new)
        p = jnp.exp(s - m_new)
        l_sc[...] = alpha * l_sc[...] + jnp.sum(p, axis=1, keepdims=True)
        acc_sc[...] = alpha * acc_sc[...] + jnp.dot(p.astype(BF16), v, preferred_element_type=F32)
        m_sc[...] = m_new

    @pl.when(j == 0)
    def _():
        m_sc[...] = jnp.full_like(m_sc, -jnp.inf)
        l_sc[...] = jnp.zeros_like(l_sc)
        acc_sc[...] = jnp.zeros_like(acc_sc)
        update(kc_ref[0, 0], vc_ref[0, 0])

    @pl.when(j > 0)
    def _():
        update(kx_ref[0, 0], vx_ref[0, 0])

    @pl.when(j == nj - 1)
    def _():
        o_ref[0] = acc_sc[...] / l_sc[...]


def attn_call(q, kc, vc, kx, vx):
    bn, nh, t, dq = q.shape
    tc = kc.shape[2]
    tq, tk = 512, 512
    nk = t // tk
    xblk = lambda j: jnp.maximum(j - 1, 0)
    return pl.pallas_call(
        _attn_body,
        grid=(bn, nh, t // tq, nk + 1),
        in_specs=[pl.BlockSpec((1, 1, tq, dq), lambda b, h, i, j: (b, h, i, 0)),
                  pl.BlockSpec((1, 1, tc, dq), lambda b, h, i, j: (b, h, 0, 0)),
                  pl.BlockSpec((1, 1, tc, DH), lambda b, h, i, j: (b, h, 0, 0)),
                  pl.BlockSpec((1, 1, tk, dq), lambda b, h, i, j: (b, h, xblk(j), 0)),
                  pl.BlockSpec((1, 1, tk, DH), lambda b, h, i, j: (b, h, xblk(j), 0))],
        out_specs=pl.BlockSpec((1, tq, DH), lambda b, h, i, j: (b, i, h)),
        out_shape=jax.ShapeDtypeStruct((bn, t, nh * DH), F32),
        scratch_shapes=[pltpu.VMEM((tq, 1), F32), pltpu.VMEM((tq, 1), F32), pltpu.VMEM((tq, DH), F32)],
        compiler_params=_cparams(("parallel", "parallel", "parallel", "arbitrary")),
        name="mla_attention",
    )(q, kc, vc, kx, vx)


def rope_tables(seq):
    rows = seq // GRID_W
    row = jnp.repeat(jnp.arange(rows), GRID_W).astype(F32)
    col = jnp.tile(jnp.arange(GRID_W), rows).astype(F32)
    n_freq = MLA_ROPE // 4
    inv = ROPE_THETA ** (-jnp.arange(n_freq, dtype=F32) / n_freq)
    ang = jnp.concatenate([row[:, None] * inv, col[:, None] * inv], axis=-1)
    cos, sin = jnp.cos(ang), jnp.sin(ang)
    half = MLA_ROPE // 2
    ones = jnp.ones((seq, MLA_NOPE), F32)
    zn = jnp.zeros((seq, MLA_NOPE), F32)
    zh = jnp.zeros((seq, half), F32)
    zp = jnp.zeros((seq, MLA_QK_PAD - MLA_QK), F32)
    cos_t = jnp.concatenate([ones, cos, cos, zp], axis=-1)
    sin_a = jnp.concatenate([zn, -sin, zh, zp], axis=-1)
    sin_b = jnp.concatenate([zn, zh, sin, zp], axis=-1)
    return cos_t, sin_a, sin_b


def _padcols(w, n):
    return jnp.concatenate([w, jnp.zeros((w.shape[0], n - w.shape[1]), w.dtype)], axis=1)


def pack_even(w):
    q, k, v, o = (w[:, j * MIX_W:(j + 1) * MIX_W] for j in range(4))
    gates = w[:, 4 * MIX_W:4 * MIX_W + 16]
    hy = w[:, 4 * MIX_W + 16:]
    return jnp.concatenate([q, k, v, hy, o, _padcols(gates, DH)], axis=1).astype(BF16)


def pack_odd(w):
    q, k, v, gg = (w[:, j * MIX_W:(j + 1) * MIX_W] for j in range(4))
    c0 = 4 * MIX_W
    gates = w[:, c0:c0 + 16]
    lq = w[:, c0 + 16:c0 + 16 + MLA_Q_RANK]
    lkv = w[:, c0 + 16 + MLA_Q_RANK:c0 + 16 + MLA_Q_RANK + MLA_KV_RANK]
    lkr = w[:, c0 + 16 + MLA_Q_RANK + MLA_KV_RANK:]
    return jnp.concatenate([q, k, v, gg, lkv, lq, _padcols(lkr, DH), _padcols(gates, DH)], axis=1).astype(BF16)


def kernel(x, c, ctx, c_ctx, mod_w, mod_b, a_w_in, a_i_bias, a_f_bias, a_norm_g, b_conv_w, b_conv_b, b_f_w1, b_f_b1, b_f_w2, b_f_b2, b_f_w3, b_f_b3, b_f_w4, b_f_freq, b_d, ab_w_out, cd_w_in, c_conv_w, c_conv_b, c_a_log, c_dt_bias, c_norm_g, d_q_norm_g, d_w_q_up, d_kv_norm_g, d_w_kv_up, d_qn_g, d_kn_g, cd_w_out, ffn_w_up, ffn_conv_w, ffn_conv_b, ffn_w_down):
    bsz, seq, d = x.shape
    ctx_len = ctx.shape[1]
    cvec = jnp.concatenate([c, c_ctx[None], jnp.zeros((8 - bsz - 1, d), F32)], axis=0)
    mods = mods_call(cvec, mod_w, mod_b)
    mx = lambda l, j: mods[l, :bsz, None, j * d:(j + 1) * d]
    mc = lambda l, j: jnp.broadcast_to(mods[l, bsz, j * d:(j + 1) * d], (bsz, 1, d))

    def ffn(l, h, m):
        return ffn_call(h, m(l, 3), m(l, 4), m(l, 5),
                        *ffn_weights(ffn_w_up[l], ffn_conv_w[l], ffn_conv_b[l], ffn_w_down[l]))

    w_in = pack_even(a_w_in[0])
    zx = inproj_call(x, mx(0, 0), mx(0, 1), w_in)
    zc = inproj_call(ctx, mc(0, 0), mc(0, 1), w_in)
    bias = jnp.concatenate([a_i_bias[0].reshape(2, N_HEADS), a_f_bias[0].reshape(2, N_HEADS)], axis=-1)[:, None, :]
    zero_state = (jnp.zeros((bsz, 2, N_HEADS, DH, DH), F32), jnp.zeros((bsz, 2, N_HEADS, 1, DH), F32),
                  jnp.zeros((bsz, 2, N_HEADS, 1, DH), F32))
    hc, st = mlstm_call(zc, bias, zero_state)
    hx, _ = mlstm_call(zx, bias, st)
    filt = (b_f_w1[0], b_f_b1[0], b_f_w2[0], b_f_b2[0], b_f_w3[0], b_f_b3[0], b_f_w4[0], b_f_freq[0])
    ux = dwconv_call(zx, E_HY // (3 * HYENA_W), b_conv_w[0], b_conv_b[0], False)
    uc = dwconv_call(zc, E_HY // (3 * HYENA_W), b_conv_w[0], b_conv_b[0], False)
    yx = hyena_long(ux, hy_filter_call(seq, *filt), b_d[0])
    yc = hyena_small(uc, hy_filter_call(ctx_len, *filt), b_d[0])
    w_out = ab_w_out[0].astype(BF16)
    g = a_norm_g[0].reshape(1, MIX_W)
    x = outproj_call("sigmoid", hx, zx, E_MO // MIX_W, yx, g, w_out, x, mx(0, 2))
    ctx = outproj_call("sigmoid", hc, zc, E_MO // MIX_W, yc, g, w_out, ctx, mc(0, 2))
    x = ffn(0, x, mx)
    ctx = ffn(0, ctx, mc)

    w_in = pack_odd(cd_w_in[0])
    zx = inproj_call(x, mx(1, 0), mx(1, 1), w_in)
    zc = inproj_call(ctx, mc(1, 0), mc(1, 1), w_in)
    qkv_x = dwconv_call(zx, 0, c_conv_w[0], c_conv_b[0], True)
    qkv_c = dwconv_call(zc, 0, c_conv_w[0], c_conv_b[0], True)
    zeros4 = jnp.zeros((2, 1, N_HEADS), F32)
    dtb = jnp.concatenate([c_dt_bias[0][:, None, :], zeros4], axis=-1)
    alog = jnp.concatenate([c_a_log[0][:, None, :], zeros4], axis=-1)
    _, s_ctx = gdn_call(qkv_c, zc[..., O_GATE:O_GATE + 16], dtb, alog, jnp.zeros((bsz, 2, N_HEADS, DH, DH), F32))
    ox, _ = gdn_call(qkv_x, zx[..., O_GATE:O_GATE + 16], dtb, alog, s_ctx)
    tabs = rope_tables(seq)
    kvg = d_kv_norm_g[0].reshape(1, -1)
    w_kv = d_w_kv_up[0].astype(BF16)
    kn_g = _padcols(d_kn_g[0].reshape(1, -1), MLA_QK_PAD)
    kc, vc = kv_call(zc, False, kvg, w_kv, kn_g, tuple(t[:ctx_len] for t in tabs))
    kx, vx = kv_call(zx, True, kvg, w_kv, kn_g, tabs)
    w_q = jnp.concatenate([_padcols(d_w_q_up[0][:, h * MLA_QK:(h + 1) * MLA_QK], MLA_QK_PAD)
                           for h in range(N_HEADS)], axis=1).astype(BF16)
    qx = q_call(zx, d_q_norm_g[0].reshape(1, -1), w_q, _padcols(d_qn_g[0].reshape(1, -1), MLA_QK_PAD), tabs)
    ax = attn_call(qx, kc, vc, kx, vx)
    g = jnp.tile(c_norm_g[0], N_HEADS).reshape(1, MIX_W)
    x = outproj_call("silu", ox, zx, O_GG // MIX_W, ax, g, cd_w_out[0].astype(BF16), x, mx(1, 2))
    x = ffn(1, x, mx)
    return x
```

```python
import functools
import math

import numpy as np
import jax
import jax.numpy as jnp
from jax import lax
from jax.experimental import pallas as pl
from jax.experimental.pallas import tpu as pltpu

F32 = jnp.float32
BF16 = jnp.bfloat16
HI = lax.Precision.HIGHEST
LOG2E = math.log2(math.e)

D_MODEL = 1024
GRID_W = 64
EPS = 1e-6
CHUNK = 64
ROPE_THETA = 10000.0
N_HEADS = 4
DH = 128
MIX_W = N_HEADS * DH
HYENA_W = 512
HYENA_EMB = 33
HYENA_BANDS = 16
HYENA_FFN = 64
HYENA_NFILT = 4 * HYENA_W
HYENA_FAST_DECAY = 0.3
HYENA_SLOW_DECAY = 1.5
HYENA_TARGET = 1e-2
MLA_NOPE = 128
MLA_ROPE = 64
MLA_QK = 192
MLA_QK_PAD = 256
MLA_Q_RANK = 384
MLA_KV_RANK = 256
D_FF = 2816
FF_TILE = 256
N_FF_TILES = D_FF // FF_TILE
FFT_N1 = 128
SCAN_TB = 256
VMEM_LIMIT = 56 * 1024 * 1024

E_QKV, E_HY, E_MO, E_GATE, E_N = 0, 1536, 3072, 3584, 3712
O_QKV, O_GG, O_LKV, O_LQ, O_LKR, O_GATE, O_N = 0, 1536, 2048, 2304, 2688, 2816, 2944


def _cparams(sem):
    return pltpu.CompilerParams(dimension_semantics=sem, vmem_limit_bytes=VMEM_LIMIT)


def _dot(a, b):
    return jnp.dot(a.astype(BF16), b.astype(BF16), preferred_element_type=F32)


def _dot_nt(a, b):
    return lax.dot_general(a.astype(BF16), b.astype(BF16), (((1,), (1,)), ((), ())), preferred_element_type=F32)


def _dot_tn(a, b):
    return lax.dot_general(a.astype(BF16), b.astype(BF16), (((0,), (0,)), ((), ())), preferred_element_type=F32)


def _dot_hi(a, b):
    return jnp.dot(a, b, precision=HI, preferred_element_type=F32)


def _rms(x, n=None):
    n = x.shape[-1] if n is None else n
    return x * lax.rsqrt(jnp.sum(x * x, axis=-1, keepdims=True) * (1.0 / n) + EPS)


def _sigmoid(x):
    return 1.0 / (1.0 + jnp.exp(-x))


def _silu(x):
    return x * _sigmoid(x)


def _softplus(x):
    return jnp.maximum(x, 0.0) + jnp.log(1.0 + jnp.exp(-jnp.abs(x)))


def _log_sigmoid(x):
    return -_softplus(-x)


def _mods_body(c_ref, w_ref, b_ref, o_ref):
    o_ref[0] = _dot_hi(_silu(c_ref[...]), w_ref[0]) + b_ref[0]


def mods_call(cvec, mod_w, mod_b):
    depth, d, n = mod_w.shape
    tn = 1536
    return pl.pallas_call(
        _mods_body,
        grid=(depth, n // tn),
        in_specs=[pl.BlockSpec((8, d), lambda l, j: (0, 0)),
                  pl.BlockSpec((1, d, tn), lambda l, j: (l, 0, j)),
                  pl.BlockSpec((1, 1, tn), lambda l, j: (l, 0, j))],
        out_specs=pl.BlockSpec((1, 8, tn), lambda l, j: (l, 0, j)),
        out_shape=jax.ShapeDtypeStruct((depth, 8, n), F32),
        compiler_params=_cparams(("parallel", "parallel")),
        name="mods",
    )(cvec, mod_w, mod_b.reshape(depth, 1, n))


def _inproj_body(x_ref, sh_ref, sc_ref, w_ref, o_ref):
    h = _rms(x_ref[0]) * (1.0 + sc_ref[0]) + sh_ref[0]
    o_ref[0] = _dot(h, w_ref[...])


def inproj_call(x, shift, scale, w):
    bn, t, d = x.shape
    n = w.shape[1]
    tm = min(256, t)
    return pl.pallas_call(
        _inproj_body,
        grid=(bn, t // tm),
        in_specs=[pl.BlockSpec((1, tm, d), lambda b, i: (b, i, 0)),
                  pl.BlockSpec((1, 1, d), lambda b, i: (b, 0, 0)),
                  pl.BlockSpec((1, 1, d), lambda b, i: (b, 0, 0)),
                  pl.BlockSpec((d, n), lambda b, i: (0, 0))],
        out_specs=pl.BlockSpec((1, tm, n), lambda b, i: (b, i, 0)),
        out_shape=jax.ShapeDtypeStruct((bn, t, n), F32),
        compiler_params=_cparams(("parallel", "parallel")),
        name="inproj",
    )(x, shift, scale, w)


def _shift_rows(u, prev_row, next_row):
    tm = u.shape[0]
    rows = lax.broadcasted_iota(jnp.int32, (tm, 1), 0)
    um = jnp.where(rows == 0, prev_row, pltpu.roll(u, 1, 0))
    up = jnp.where(rows == tm - 1, next_row, pltpu.roll(u, tm - 1, 0))
    return um, up


def _dwconv_body(gdn, z_ref, zp_ref, zn_ref, w_ref, b_ref, o_ref):
    i, nt = pl.program_id(1), pl.num_programs(1)
    u = z_ref[0]
    prev_row = zp_ref[0][7:8, :] * (i > 0).astype(F32)
    next_row = zn_ref[0][0:1, :] * (i < nt - 1).astype(F32)
    um, up = _shift_rows(u, prev_row, next_row)
    y = um * w_ref[0:1, :] + u * w_ref[1:2, :] + up * w_ref[2:3, :] + b_ref[...]
    if not gdn:
        o_ref[0] = y
        return
    y = _silu(y)
    for h in range(2 * N_HEADS):
        s = y[:, h * DH:(h + 1) * DH]
        s = s * lax.rsqrt(jnp.sum(s * s, axis=-1, keepdims=True) + EPS)
        if h < N_HEADS:
            s = s * DH ** -0.5
        o_ref[0, :, h * DH:(h + 1) * DH] = s
    o_ref[0, :, 2 * MIX_W:] = y[:, 2 * MIX_W:]


def dwconv_call(z, col_block, w, b, gdn):
    bn, t, _ = z.shape
    c = w.shape[1]
    tm = min(512, t)
    r8 = tm // 8
    return pl.pallas_call(
        functools.partial(_dwconv_body, gdn),
        grid=(bn, t // tm),
        in_specs=[pl.BlockSpec((1, tm, c), lambda b_, i: (b_, i, col_block)),
                  pl.BlockSpec((1, 8, c), lambda b_, i: (b_, jnp.maximum(i * r8 - 1, 0), col_block)),
                  pl.BlockSpec((1, 8, c), lambda b_, i: (b_, jnp.minimum((i + 1) * r8, t // 8 - 1), col_block)),
                  pl.BlockSpec((3, c), lambda b_, i: (0, 0)),
                  pl.BlockSpec((1, c), lambda b_, i: (0, 0))],
        out_specs=pl.BlockSpec((1, tm, c), lambda b_, i: (b_, i, 0)),
        out_shape=jax.ShapeDtypeStruct((bn, t, c), F32),
        compiler_params=_cparams(("parallel", "parallel")),
        name="dwconv_gdn" if gdn else "dwconv",
    )(z, z, z, w, b.reshape(1, c))


def _block_masks(d, tb):
    sh = CHUNK.bit_length() - 1
    ii = lax.broadcasted_iota(jnp.int32, (tb, tb), 0)
    jj = lax.broadcasted_iota(jnp.int32, (tb, tb), 1)
    same = (ii >> sh) == (jj >> sh)
    vis = jnp.logical_and(same, (ii >= jj) if d == 0 else (ii <= jj))
    vis_t = jnp.logical_and(same, (jj >= ii) if d == 0 else (jj <= ii))
    return ii, jj, same, vis, vis_t


def _chunk_order(d, nc):
    return range(nc) if d == 0 else reversed(range(nc))


def _mlstm_block(d, q, k, v, li_c, lf_c, li_r, lf_r, c_mat, n_vec, m):
    tb = q.shape[0]
    nc = tb // CHUNK
    _, _, same, vis, vis_t = _block_masks(d, tb)
    b_c = jnp.sum(vis.astype(F32) * lf_r, axis=1, keepdims=True)
    b_r = jnp.sum(vis_t.astype(F32) * lf_c, axis=0, keepdims=True)
    bl_c = jnp.sum(same.astype(F32) * lf_r, axis=1, keepdims=True)
    a_c = bl_c - b_c + li_c
    dmat = jnp.where(vis, b_c - b_r + li_r, -jnp.inf)
    m_in = jnp.max(dmat, axis=1, keepdims=True)
    s = _dot_nt(q, k) * jnp.exp(dmat - m_in)
    num_in = _dot(s, v)
    den_in = jnp.sum(s, axis=1, keepdims=True)
    hs = [None] * nc
    for c in _chunk_order(d, nc):
        sl = slice(c * CHUNK, (c + 1) * CHUNK)
        b_last = bl_c[c * CHUNK:c * CHUNK + 1, :]
        m_loc = jnp.max(a_c[sl], axis=0, keepdims=True)
        m_inter = b_c[sl] + m
        m_t = jnp.maximum(m_inter, m_in[sl])
        s_inter, s_in = jnp.exp(m_inter - m_t), jnp.exp(m_in[sl] - m_t)
        num = s_inter * _dot_nt(q[sl], c_mat) + s_in * num_in[sl]
        den = s_inter * jnp.sum(q[sl] * n_vec, axis=1, keepdims=True) + s_in * den_in[sl]
        hs[c] = num / jnp.maximum(jnp.abs(den), jnp.exp(-m_t))
        m_new = jnp.maximum(b_last + m, m_loc)
        f_old, f_loc = jnp.exp(b_last + m - m_new), jnp.exp(m_loc - m_new)
        kw = k[sl] * jnp.exp(a_c[sl] - m_loc)
        c_mat = f_old * c_mat + f_loc * _dot_tn(v[sl], kw)
        n_vec = f_old * n_vec + f_loc * jnp.sum(kw, axis=0, keepdims=True)
        m = m_new
    return jnp.concatenate(hs, axis=0), c_mat, n_vec, m


def _mlstm_body(d, q_ref, k_ref, v_ref, gc_ref, gr_ref, bias_ref, c0_ref, n0_ref, m0_ref,
                h_ref, c_ref, n_ref, m_ref):
    @pl.when(pl.program_id(1) == 0)
    def _():
        c_ref[...] = c0_ref[...]
        n_ref[...] = n0_ref[...]
        m_ref[...] = m0_ref[...]

    gc = gc_ref[0, 0]
    gr = gr_ref[0, 0]
    for h in range(N_HEADS):
        hs = slice(h * DH, (h + 1) * DH)
        bi = bias_ref[0, :, h:h + 1]
        bf = bias_ref[0, :, N_HEADS + h:N_HEADS + h + 1]
        li_c = gc[:, h:h + 1] + bi
        lf_c = _log_sigmoid(gc[:, N_HEADS + h:N_HEADS + h + 1] + bf)
        li_r = gr[h:h + 1, :] + bi
        lf_r = _log_sigmoid(gr[N_HEADS + h:N_HEADS + h + 1, :] + bf)
        hv, c_new, n_new, m_new = _mlstm_block(d, q_ref[0, :, hs], k_ref[0, :, hs] * DH ** -0.5, v_ref[0, :, hs],
                                               li_c, lf_c, li_r, lf_r, c_ref[0, h], n_ref[0, h], m_ref[0, h][:, 0:1])
        h_ref[0, :, hs] = hv
        c_ref[0, h] = c_new
        n_ref[0, h] = n_new
        m_ref[0, h] = jnp.broadcast_to(m_new, (1, DH))


def _gdn_block(d, q, k, v, bt_c, la_c, la_r, s_mat):
    tb = q.shape[0]
    nc = tb // CHUNK
    sh = CHUNK.bit_length() - 1
    ii, jj, same, vis, vis_t = _block_masks(d, tb)
    g_c = jnp.sum(vis.astype(F32) * la_r, axis=1, keepdims=True)
    g_r = jnp.sum(vis_t.astype(F32) * la_c, axis=0, keepdims=True)
    gl_c = jnp.sum(same.astype(F32) * la_r, axis=1, keepdims=True)
    decay = jnp.exp(jnp.where(vis, g_c - g_r, -jnp.inf))
    kb = k * bt_c
    a_mat = jnp.where(ii == jj, 0.0, _dot_nt(kb, k) * decay)
    off = lambda lvl: jnp.logical_and((ii >> (lvl + 1)) == (jj >> (lvl + 1)), (ii >> lvl) != (jj >> lvl))
    t_mat = (ii == jj).astype(F32) - jnp.where(off(0), a_mat, 0.0)
    for lvl in range(1, sh):
        t_mat = t_mat - _dot(_dot(t_mat, jnp.where(off(lvl), a_mat, 0.0)), t_mat)
    eg = jnp.exp(g_c)
    u = _dot(t_mat, v * bt_c)
    w = _dot(t_mat, kb * eg)
    kd = k * jnp.exp(gl_c - g_c)
    qg = q * eg
    attn = _dot_nt(q, k) * decay
    v_new, o_state = [None] * nc, [None] * nc
    for c in _chunk_order(d, nc):
        sl = slice(c * CHUNK, (c + 1) * CHUNK)
        v_new[c] = u[sl] - _dot(w[sl], s_mat)
        o_state[c] = _dot(qg[sl], s_mat)
        s_mat = s_mat * jnp.exp(gl_c[c * CHUNK:c * CHUNK + 1, :]) + _dot_tn(kd[sl], v_new[c])
    o = jnp.concatenate(o_state, axis=0) + _dot(attn, jnp.concatenate(v_new, axis=0))
    return o, s_mat


def _gdn_body(d, q_ref, k_ref, v_ref, gc_ref, gr_ref, dtb_ref, alog_ref, s0_ref, o_ref, s_ref):
    @pl.when(pl.program_id(1) == 0)
    def _():
        s_ref[...] = s0_ref[...]

    gc = gc_ref[0, 0]
    gr = gr_ref[0, 0]
    for h in range(N_HEADS):
        hs = slice(h * DH, (h + 1) * DH)
        dtb = dtb_ref[0, :, h:h + 1]
        neg_a = -jnp.exp(alog_ref[0, :, h:h + 1])
        bt_c = _sigmoid(gc[:, h:h + 1])
        la_c = neg_a * _softplus(gc[:, N_HEADS + h:N_HEADS + h + 1] + dtb)
        la_r = neg_a * _softplus(gr[N_HEADS + h:N_HEADS + h + 1, :] + dtb)
        ov, s_new = _gdn_block(d, q_ref[0, :, hs], k_ref[0, :, hs], v_ref[0, :, hs], bt_c, la_c, la_r, s_ref[0, h])
        o_ref[0, :, hs] = ov
        s_ref[0, h] = s_new


def _gate_views(graw):
    bsz, t, _ = graw.shape
    gcol = graw.reshape(bsz, t, 2, 2, N_HEADS).transpose(0, 3, 1, 2, 4).reshape(bsz, 2, t, 8)
    return gcol, gcol.transpose(0, 1, 3, 2)


def _scan_specs(d, t, col0):
    tb = min(SCAN_TB, t)
    nb = t // tb
    blk = (lambda i: i) if d == 0 else (lambda i: nb - 1 - i)
    qkv = [pl.BlockSpec((1, tb, MIX_W), lambda b, i, j=j: (b, blk(i), col0 + j)) for j in range(3)]
    gates = [pl.BlockSpec((1, 1, tb, 8), lambda b, i: (b, d, blk(i), 0)),
             pl.BlockSpec((1, 1, 8, tb), lambda b, i: (b, d, 0, blk(i)))]
    par = pl.BlockSpec((1, 1, 8), lambda b, i: (d, 0, 0))
    out = pl.BlockSpec((1, tb, MIX_W), lambda b, i: (b, blk(i), 0))
    return nb, qkv, gates, par, out


def _state_spec(*tail):
    return pl.BlockSpec((1, N_HEADS) + tail, lambda b, i: (b, 0) + (0,) * len(tail))


def mlstm_call(d, z, bias, state):
    bsz, t, _ = z.shape
    nb, qkv, gates, par, out = _scan_specs(d, t, E_QKV // MIX_W)
    gcol, grow = _gate_views(z[..., E_GATE:E_GATE + 16])
    st_specs = [_state_spec(DH, DH), _state_spec(1, DH), _state_spec(1, DH)]
    st_shapes = [jax.ShapeDtypeStruct((bsz, N_HEADS, DH, DH), F32),
                 jax.ShapeDtypeStruct((bsz, N_HEADS, 1, DH), F32),
                 jax.ShapeDtypeStruct((bsz, N_HEADS, 1, DH), F32)]
    res = pl.pallas_call(
        functools.partial(_mlstm_body, d),
        grid=(bsz, nb),
        in_specs=qkv + gates + [par] + st_specs,
        out_specs=[out] + st_specs,
        out_shape=[jax.ShapeDtypeStruct((bsz, t, MIX_W), F32)] + st_shapes,
        compiler_params=_cparams(("parallel", "arbitrary")),
        name="mlstm_scan_fwd" if d == 0 else "mlstm_scan_bwd",
    )(z, z, z, gcol, grow, bias, *state)
    return res[0], tuple(res[1:])


def gdn_call(d, qkv_arr, graw, dtb, alog, state):
    bsz, t, _ = qkv_arr.shape
    nb, qkv, gates, par, out = _scan_specs(d, t, 0)
    gcol, grow = _gate_views(graw)
    st_spec = _state_spec(DH, DH)
    return pl.pallas_call(
        functools.partial(_gdn_body, d),
        grid=(bsz, nb),
        in_specs=qkv + gates + [par, par, st_spec],
        out_specs=[out, st_spec],
        out_shape=[jax.ShapeDtypeStruct((bsz, t, MIX_W), F32),
                   jax.ShapeDtypeStruct((bsz, N_HEADS, DH, DH), F32)],
        compiler_params=_cparams(("parallel", "arbitrary")),
        name="gdn_scan_fwd" if d == 0 else "gdn_scan_bwd",
    )(qkv_arr, qkv_arr, qkv_arr, gcol, grow, dtb, alog, state)


def _outproj_body(act, hf_ref, hb_ref, gate_ref, y_ref, g_ref, w_ref, x_ref, mg_ref, o_ref):
    hs = hf_ref[0] + hb_ref[0]
    hn = jnp.concatenate([_rms(hs[:, h * DH:(h + 1) * DH]) for h in range(N_HEADS)], axis=-1) * g_ref[...]
    gate = gate_ref[0]
    hn = hn * (_sigmoid(gate) if act == "sigmoid" else _silu(gate))
    out = _dot(hn, w_ref[0:MIX_W, :]) + _dot(y_ref[0], w_ref[MIX_W:, :])
    o_ref[0] = x_ref[0] + mg_ref[0] * out


def outproj_call(act, hf, hb, z, gate_block, y, g, w, x, mgate):
    bn, t, d = x.shape
    tm = min(512, t)
    return pl.pallas_call(
        functools.partial(_outproj_body, act),
        grid=(bn, t // tm),
        in_specs=[pl.BlockSpec((1, tm, MIX_W), lambda b, i: (b, i, 0)),
                  pl.BlockSpec((1, tm, MIX_W), lambda b, i: (b, i, 0)),
                  pl.BlockSpec((1, tm, MIX_W), lambda b, i: (b, i, gate_block)),
                  pl.BlockSpec((1, tm, MIX_W), lambda b, i: (b, i, 0)),
                  pl.BlockSpec((1, MIX_W), lambda b, i: (0, 0)),
                  pl.BlockSpec((2 * MIX_W, d), lambda b, i: (0, 0)),
                  pl.BlockSpec((1, tm, d), lambda b, i: (b, i, 0)),
                  pl.BlockSpec((1, 1, d), lambda b, i: (b, 0, 0))],
        out_specs=pl.BlockSpec((1, tm, d), lambda b, i: (b, i, 0)),
        out_shape=jax.ShapeDtypeStruct((bn, t, d), F32),
        compiler_params=_cparams(("parallel", "parallel")),
        name="outproj_" + act,
    )(hf, hb, z, y, g, w, x, mgate)


def _ffn_body(x_ref, xp_ref, xn_ref, sh_ref, sc_ref, mg_ref, wa_ref, wb_ref, cwa_ref, cwb_ref, wd_ref, o_ref):
    i, nt = pl.program_id(1), pl.num_programs(1)
    sc, sh = 1.0 + sc_ref[0], sh_ref[0]
    x = x_ref[0]
    h = (_rms(x) * sc + sh).astype(BF16)
    hp = (_rms(xp_ref[0]) * sc + sh).astype(BF16)
    hn = (_rms(xn_ref[0]) * sc + sh).astype(BF16)
    pv, nv = (i > 0).astype(F32), (i < nt - 1).astype(F32)

    def conv(w_ref, cw_ref, j):
        w = w_ref[j]
        u = jnp.dot(h, w, preferred_element_type=F32)
        prev_row = jnp.dot(hp, w, preferred_element_type=F32)[7:8, :] * pv
        next_row = jnp.dot(hn, w, preferred_element_type=F32)[0:1, :] * nv
        um, up = _shift_rows(u, prev_row, next_row)
        cw = cw_ref[j]
        return um * cw[0:1, :] + u * cw[1:2, :] + up * cw[2:3, :] + cw[3:4, :]

    def tile(j, acc):
        a, b = conv(wa_ref, cwa_ref, j), conv(wb_ref, cwb_ref, j)
        return acc + _dot(_silu(a) * b, wd_ref[j])

    acc = lax.fori_loop(0, N_FF_TILES, tile, jnp.zeros(x.shape, F32))
    o_ref[0] = x + mg_ref[0] * acc


def ffn_call(x, shift, scale, mgate, wa, wb, cwa, cwb, wd):
    bn, t, d = x.shape
    tm = min(512, t)
    r8 = tm // 8
    full = lambda a: pl.BlockSpec(a.shape, lambda b, i: (0,) * a.ndim)
    vec = pl.BlockSpec((1, 1, d), lambda b, i: (b, 0, 0))
    return pl.pallas_call(
        _ffn_body,
        grid=(bn, t // tm),
        in_specs=[pl.BlockSpec((1, tm, d), lambda b, i: (b, i, 0)),
                  pl.BlockSpec((1, 8, d), lambda b, i: (b, jnp.maximum(i * r8 - 1, 0), 0)),
                  pl.BlockSpec((1, 8, d), lambda b, i: (b, jnp.minimum((i + 1) * r8, t // 8 - 1), 0)),
                  vec, vec, vec, full(wa), full(wb), full(cwa), full(cwb), full(wd)],
        out_specs=pl.BlockSpec((1, tm, d), lambda b, i: (b, i, 0)),
        out_shape=jax.ShapeDtypeStruct((bn, t, d), F32),
        compiler_params=_cparams(("parallel", "parallel")),
        name="conv_ffn",
    )(x, x, x, shift, scale, mgate, wa, wb, cwa, cwb, wd)


def ffn_weights(w_up, conv_w, conv_b, w_down):
    d = w_up.shape[0]
    tiles = lambda a: a.reshape(a.shape[0], N_FF_TILES, FF_TILE).transpose(1, 0, 2)
    cw = jnp.concatenate([conv_w, conv_b[None]], axis=0)
    return (tiles(w_up[:, :D_FF]).astype(BF16), tiles(w_up[:, D_FF:]).astype(BF16),
            tiles(cw[:, :D_FF]), tiles(cw[:, D_FF:]),
            w_down.reshape(N_FF_TILES, FF_TILE, d).astype(BF16))


def _hy_filter_body(lf, w1t_ref, w1c_ref, w1s_ref, b1_ref, w2_ref, b2_ref, w3_ref, b3_ref, w4_ref, fr_ref,
                    band_ref, dl_ref, o_ref, l1_ref):
    ph, r = pl.program_id(0), pl.program_id(1)
    tr = o_ref.shape[0]
    n = (r * tr + lax.broadcasted_iota(jnp.int32, (tr, 1), 0))
    t = jnp.where(n < lf, n, 2 * lf - n).astype(F32)
    t_norm = t / (lf - 1)
    ang = (2.0 * math.pi * t / lf) * band_ref[...]
    fr = fr_ref[...]
    pre = t_norm * w1t_ref[...] + _dot_hi(jnp.cos(ang), w1c_ref[...]) - _dot_hi(jnp.sin(ang), w1s_ref[...])
    hdn = jnp.sin(fr * (pre + b1_ref[...]))
    hdn = jnp.sin(fr * (_dot_hi(hdn, w2_ref[...]) + b2_ref[...]))
    hdn = jnp.sin(fr * (_dot_hi(hdn, w3_ref[...]) + b3_ref[...]))
    hval = _dot_hi(hdn, w4_ref[0]) * jnp.exp(-t_norm * jnp.abs(dl_ref[0]))
    hval = jnp.where(n == lf, 0.0, hval)

    @pl.when(jnp.logical_and(ph == 0, r == 0))
    def _():
        l1_ref[...] = jnp.zeros_like(l1_ref)

    @pl.when(ph == 0)
    def _():
        l1_ref[...] += jnp.sum(jnp.abs(hval), axis=0, keepdims=True)

    @pl.when(ph == 1)
    def _():
        o_ref[...] = hval / l1_ref[...]


def hy_filter_call(lf, f_w1, f_b1, f_w2, f_b2, f_w3, f_b3, f_w4, f_freq):
    tr = min(512, lf)
    nt = 2 * lf // tr
    half = lf // tr
    ncol = 2 * HYENA_W
    pad = lambda a: jnp.concatenate([a, jnp.zeros((DH - a.shape[0], a.shape[1]), F32)], axis=0)
    w4 = f_w4.reshape(HYENA_FFN, 2, 2, HYENA_W).transpose(2, 0, 1, 3).reshape(2, HYENA_FFN, ncol)
    deltas = jnp.linspace(math.log(HYENA_TARGET) / HYENA_SLOW_DECAY, math.log(HYENA_TARGET) / HYENA_FAST_DECAY,
                          HYENA_NFILT, dtype=F32)
    deltas = deltas.reshape(2, 2, HYENA_W).transpose(1, 0, 2).reshape(2, 1, ncol)
    bands = jnp.linspace(1e-4, HYENA_BANDS - 1, HYENA_BANDS, dtype=F32)
    bands = jnp.concatenate([bands, jnp.zeros((DH - HYENA_BANDS,), F32)]).reshape(1, DH)
    row = lambda a: a.reshape(1, -1)
    args = (f_w1[0:1], pad(f_w1[1:1 + HYENA_BANDS]), pad(f_w1[1 + HYENA_BANDS:]), row(f_b1), f_w2, row(f_b2),
            f_w3, row(f_b3), w4, row(f_freq), bands, deltas)
    full = lambda a: pl.BlockSpec(a.shape, lambda p, r: (0,) * a.ndim)
    by_dir = lambda a: pl.BlockSpec((1,) + a.shape[1:], lambda p, r: (r // half, 0, 0))
    in_specs = [by_dir(a) if a.ndim == 3 else full(a) for a in args]
    return pl.pallas_call(
        functools.partial(_hy_filter_body, lf),
        grid=(2, nt),
        in_specs=in_specs,
        out_specs=pl.BlockSpec((tr, ncol), lambda p, r: (r * p, 0)),
        out_shape=jax.ShapeDtypeStruct((2 * lf, ncol), F32),
        scratch_shapes=[pltpu.VMEM((1, ncol), F32)],
        compiler_params=_cparams(("arbitrary", "arbitrary")),
        name="hyena_filter",
    )(*args)


def _dft_consts(lf):
    n1 = FFT_N1
    n = 2 * lf
    assert n == n1 * n1
    idx = np.arange(n1)
    f = np.exp(-2j * np.pi * np.outer(idx, idx) / n1)
    tw = np.exp(-2j * np.pi * np.outer(idx, idx) / n)
    fh = f[:, :n1 // 2]
    blk = lambda a: np.block([[a.real, -a.imag], [a.imag, a.real]])
    f_sig = blk(fh)
    f_flt = np.concatenate([f.real, f.imag], axis=0)
    c_inv = blk(np.conj(f)[:n1 // 2, :] / n)
    c = lambda a: jnp.asarray(a, dtype=F32)
    return dict(f_sig=c(f_sig), f_flt=c(f_flt), c_inv=c(c_inv), fr=c(f.real), fi=c(f.imag),
                twr=c(tw.real.reshape(n1, 1, n1)), twi=c(tw.imag.reshape(n1, 1, n1)))


def _lead_mm_body(m_ref, x_ref, o_ref):
    o_ref[...] = _dot_hi(m_ref[...], x_ref[...])


def lead_mm_call(mat, x2d, name):
    r, k = mat.shape
    nl = x2d.shape[1]
    tn = 2048
    return pl.pallas_call(
        _lead_mm_body,
        grid=(nl // tn,),
        in_specs=[pl.BlockSpec((r, k), lambda j: (0, 0)), pl.BlockSpec((k, tn), lambda j: (0, j))],
        out_specs=pl.BlockSpec((r, tn), lambda j: (0, j)),
        out_shape=jax.ShapeDtypeStruct((r, nl), F32),
        compiler_params=_cparams(("parallel",)),
        name=name,
    )(mat, x2d)


def _stage2_mat(fr, fi, twr, twi):
    hr = fr * twr - fi * twi
    hi = fr * twi + fi * twr
    return hr, hi


def _blk(ar, ai):
    return jnp.concatenate([jnp.concatenate([ar, -ai], axis=1), jnp.concatenate([ai, ar], axis=1)], axis=0)


def _spec_filter_body(a_ref, fr_ref, fi_ref, twr_ref, twi_ref, o_ref):
    hr, hi = _stage2_mat(fr_ref[...], fi_ref[...], twr_ref[0], twi_ref[0])
    n1 = hr.shape[0]
    o_ref[0] = _dot_hi(_blk(hr, hi), a_ref[...].reshape(2 * n1, a_ref.shape[-1]))


def _spec_conv_body(a_ref, k_ref, fr_ref, fi_ref, twr_ref, twi_ref, o_ref):
    hr, hi = _stage2_mat(fr_ref[...], fi_ref[...], twr_ref[0], twi_ref[0])
    n1 = hr.shape[0]
    x = _dot_hi(_blk(hr, hi), a_ref[...].reshape(2 * n1, a_ref.shape[-1]))
    xr, xi = x[:n1], x[n1:]
    kr, ki = k_ref[0, :n1], k_ref[0, n1:]
    y = jnp.concatenate([xr * kr - xi * ki, xr * ki + xi * kr], axis=0)
    o_ref[0] = _dot_hi(_blk(hr.T, -(hi.T)), y)


def spec_call(a2d, kspec, order, cst):
    n1 = FFT_N1
    c = a2d.shape[1] // n1
    ct = 512
    a4 = a2d.reshape(2, n1, n1, c)
    sq = pl.BlockSpec((n1, n1), lambda k, j: (0, 0))
    tw = pl.BlockSpec((1, 1, n1), lambda k, j: (k, 0, 0))
    a_spec = pl.BlockSpec((2, 1, n1, ct), lambda k, j: (0, k, 0, j))
    o_spec = pl.BlockSpec((1, 2 * n1, ct), lambda k, j: (k, 0, j))
    common = dict(grid=(n1, c // ct), out_specs=o_spec, out_shape=jax.ShapeDtypeStruct((n1, 2 * n1, c), F32),
                  compiler_params=_cparams(("parallel", "parallel")))
    if kspec is None:
        return pl.pallas_call(_spec_filter_body, in_specs=[a_spec, sq, sq, tw, tw], name="hyena_spec_filter",
                              **common)(a4, cst["fr"], cst["fi"], cst["twr"], cst["twi"])
    k_spec = pl.BlockSpec((1, 2 * n1, ct), lambda k, j: (k, 0, order * (c // ct) + j))
    return pl.pallas_call(_spec_conv_body, in_specs=[a_spec, k_spec, sq, sq, tw, tw], name="hyena_spec_conv",
                          **common)(a4, kspec, cst["fr"], cst["fi"], cst["twr"], cst["twi"])


def _inv_gate_body(m_ref, br_ref, bi_ref, v_ref, x_ref, d_ref, o_ref):
    y = _dot_hi(m_ref[...], jnp.concatenate([br_ref[...], bi_ref[...]], axis=0))
    o_ref[...] = x_ref[...] * (y + v_ref[...] * d_ref[...])


def inv_gate_call(b3, v2d, x2d, dvec, cst):
    n1 = FFT_N1
    nl = v2d.shape[1]
    tn = 2048
    nb = nl // tn
    b2d = b3.reshape(n1, 2 * nl)
    tile = pl.BlockSpec((n1, tn), lambda j: (0, j))
    return pl.pallas_call(
        _inv_gate_body,
        grid=(nb,),
        in_specs=[pl.BlockSpec((n1, 2 * n1), lambda j: (0, 0)), tile,
                  pl.BlockSpec((n1, tn), lambda j: (0, nb + j)), tile, tile,
                  pl.BlockSpec((1, tn), lambda j: (0, j))],
        out_specs=tile,
        out_shape=jax.ShapeDtypeStruct((n1, nl), F32),
        compiler_params=_cparams(("parallel",)),
        name="hyena_inv_gate",
    )(cst["c_inv"], b2d, b2d, v2d, x2d, dvec)


def hyena_long(u, kf, d_skip):
    bsz, lf, _ = u.shape
    assert bsz == 2
    n1 = FFT_N1
    cst = _dft_consts(lf)
    rb = lambda a: a.reshape(bsz * n1 // 2, n1 * HYENA_W)
    v, x1, x2 = (rb(u[..., j * HYENA_W:(j + 1) * HYENA_W]) for j in range(3))
    kspec = spec_call(lead_mm_call(cst["f_flt"], kf.reshape(n1, n1 * 2 * HYENA_W), "hyena_lead_filter"),
                      None, 0, cst)
    y = v
    for o, xg in enumerate((x1, x2)):
        a = lead_mm_call(cst["f_sig"], y, "hyena_lead_signal")
        b3 = spec_call(a, kspec, o, cst)
        y = inv_gate_call(b3, y, xg, jnp.tile(d_skip[o], n1).reshape(1, -1), cst)
    return y.reshape(bsz, lf, HYENA_W)


def _hyena_small_body(u_ref, kf_ref, fs_ref, ff_ref, ci_ref, d_ref, o_ref):
    w = HYENA_W
    nk = ff_ref.shape[0] // 2
    kspec = _dot_hi(ff_ref[...], kf_ref[...])
    ucat = jnp.concatenate([u_ref[0], u_ref[1]], axis=0)
    y = ucat[:, :w]
    for o in range(2):
        x = _dot_hi(fs_ref[...], y)
        xr, xi = x[:nk], x[nk:]
        kr, ki = kspec[:nk, o * w:(o + 1) * w], kspec[nk:, o * w:(o + 1) * w]
        prod = jnp.concatenate([xr * kr - xi * ki, xr * ki + xi * kr], axis=0)
        conv = _dot_hi(ci_ref[...], prod)
        y = ucat[:, (o + 1) * w:(o + 2) * w] * (conv + y * d_ref[o:o + 1, :])
    lf = u_ref.shape[1]
    o_ref[0] = y[:lf]
    o_ref[1] = y[lf:]


def hyena_small(u, kf, d_skip):
    bsz, lf, _ = u.shape
    assert bsz == 2
    n = 2 * lf
    idx = np.arange(n)
    f = np.exp(-2j * np.pi * np.outer(idx, idx) / n)
    blk = lambda a: np.block([[a.real, -a.imag], [a.imag, a.real]])
    fs = jnp.asarray(blk(f[:, :lf]), F32)
    ff = jnp.asarray(np.concatenate([f.real, f.imag], axis=0), F32)
    ci = jnp.asarray(blk(np.conj(f)[:lf, :] / n), F32)
    return pl.pallas_call(
        _hyena_small_body,
        out_shape=jax.ShapeDtypeStruct((bsz, lf, HYENA_W), F32),
        compiler_params=pltpu.CompilerParams(vmem_limit_bytes=VMEM_LIMIT),
        name="hyena_small",
    )(u, kf, fs, ff, ci, d_skip)


def _rope(x, cos_ref, sa_ref, sb_ref):
    w = x.shape[-1]
    half = MLA_ROPE // 2
    return x * cos_ref[...] + pltpu.roll(x, w - half, 1) * sa_ref[...] + pltpu.roll(x, half, 1) * sb_ref[...]


def _kv_body(rope, lkv_ref, lkr_ref, g_ref, w_ref, kg_ref, cos_ref, sa_ref, sb_ref, k_ref, v_ref):
    kv = _dot(_rms(lkv_ref[0]) * g_ref[...], w_ref[...])
    kr = lkr_ref[0]
    for h in range(N_HEADS):
        base = h * (MLA_NOPE + DH)
        k = jnp.concatenate([kv[:, base:base + MLA_NOPE], kr], axis=-1)
        k = _rms(k, MLA_QK) * kg_ref[...]
        if rope:
            k = _rope(k, cos_ref, sa_ref, sb_ref)
        k_ref[0, h] = k.astype(BF16)
        v_ref[0, h] = kv[:, base + MLA_NOPE:base + MLA_NOPE + DH].astype(BF16)


def kv_call(z, rope, g, w, kg, tabs):
    bn, t, _ = z.shape
    tm = min(512, t)
    tab = pl.BlockSpec((tm, MLA_QK_PAD), lambda b, i: (i, 0))
    return pl.pallas_call(
        functools.partial(_kv_body, rope),
        grid=(bn, t // tm),
        in_specs=[pl.BlockSpec((1, tm, MLA_KV_RANK), lambda b, i: (b, i, O_LKV // MLA_KV_RANK)),
                  pl.BlockSpec((1, tm, DH), lambda b, i: (b, i, O_LKR // DH)),
                  pl.BlockSpec((1, MLA_KV_RANK), lambda b, i: (0, 0)),
                  pl.BlockSpec(w.shape, lambda b, i: (0, 0)),
                  pl.BlockSpec((1, MLA_QK_PAD), lambda b, i: (0, 0)), tab, tab, tab],
        out_specs=[pl.BlockSpec((1, N_HEADS, tm, MLA_QK_PAD), lambda b, i: (b, 0, i, 0)),
                   pl.BlockSpec((1, N_HEADS, tm, DH), lambda b, i: (b, 0, i, 0))],
        out_shape=[jax.ShapeDtypeStruct((bn, N_HEADS, t, MLA_QK_PAD), BF16),
                   jax.ShapeDtypeStruct((bn, N_HEADS, t, DH), BF16)],
        compiler_params=_cparams(("parallel", "parallel")),
        name="mla_kv_rope" if rope else "mla_kv",
    )(z, z, g, w, kg, *tabs)


def _q_body(lq_ref, g_ref, w_ref, qg_ref, cos_ref, sa_ref, sb_ref, q_ref):
    q = _dot(_rms(lq_ref[0]) * g_ref[...], w_ref[...])
    for h in range(N_HEADS):
        qh = _rms(q[:, h * MLA_QK_PAD:(h + 1) * MLA_QK_PAD], MLA_QK) * qg_ref[...]
        qh = _rope(qh, cos_ref, sa_ref, sb_ref) * (MLA_QK ** -0.5 * LOG2E)
        q_ref[0, h] = qh.astype(BF16)


def q_call(z, g, w, qg, tabs):
    bn, t, _ = z.shape
    tm = min(512, t)
    tab = pl.BlockSpec((tm, MLA_QK_PAD), lambda b, i: (i, 0))
    return pl.pallas_call(
        _q_body,
        grid=(bn, t // tm),
        in_specs=[pl.BlockSpec((1, tm, MLA_Q_RANK), lambda b, i: (b, i, O_LQ // MLA_Q_RANK)),
                  pl.BlockSpec((1, MLA_Q_RANK), lambda b, i: (0, 0)),
                  pl.BlockSpec(w.shape, lambda b, i: (0, 0)),
                  pl.BlockSpec((1, MLA_QK_PAD), lambda b, i: (0, 0)), tab, tab, tab],
        out_specs=pl.BlockSpec((1, N_HEADS, tm, MLA_QK_PAD), lambda b, i: (b, 0, i, 0)),
        out_shape=jax.ShapeDtypeStruct((bn, N_HEADS, t, MLA_QK_PAD), BF16),
        compiler_params=_cparams(("parallel", "parallel")),
        name="mla_q",
    )(z, g, w, qg, *tabs)


def _attn_body(q_ref, kc_ref, vc_ref, kx_ref, vx_ref, o_ref, m_sc, l_sc, acc_sc):
    j, nj = pl.program_id(3), pl.num_programs(3)

    def update(k, v):
        s = lax.dot_general(q_ref[0, 0], k, (((1,), (1,)), ((), ())), preferred_element_type=F32)
        m_prev = m_sc[...]
        m_new = jnp.maximum(m_prev, jnp.max(s, axis=1, keepdims=True))
        alpha = jnp.exp2(m_prev - m_new)
        p = jnp.exp2(s - m_new)
        l_sc[...] = alpha * l_sc[...] + jnp.sum(p, axis=1, keepdims=True)
        acc_sc[...] = alpha * acc_sc[...] + jnp.dot(p.astype(BF16), v, preferred_element_type=F32)
        m_sc[...] = m_new

    @pl.when(j == 0)
    def _():
        m_sc[...] = jnp.full_like(m_sc, -jnp.inf)
        l_sc[...] = jnp.zeros_like(l_sc)
        acc_sc[...] = jnp.zeros_like(acc_sc)
        update(kc_ref[0, 0], vc_ref[0, 0])

    @pl.when(j > 0)
    def _():
        update(kx_ref[0, 0], vx_ref[0, 0])

    @pl.when(j == nj - 1)
    def _():
        o_ref[0] = acc_sc[...] / l_sc[...]


def attn_call(q, kc, vc, kx, vx):
    bn, nh, t, dq = q.shape
    tc = kc.shape[2]
    tq, tk = 1024, 1024
    nk = t // tk
    xblk = lambda j: jnp.maximum(j - 1, 0)
    return pl.pallas_call(
        _attn_body,
        grid=(bn, nh, t // tq, nk + 1),
        in_specs=[pl.BlockSpec((1, 1, tq, dq), lambda b, h, i, j: (b, h, i, 0)),
                  pl.BlockSpec((1, 1, tc, dq), lambda b, h, i, j: (b, h, 0, 0)),
                  pl.BlockSpec((1, 1, tc, DH), lambda b, h, i, j: (b, h, 0, 0)),
                  pl.BlockSpec((1, 1, tk, dq), lambda b, h, i, j: (b, h, xblk(j), 0)),
                  pl.BlockSpec((1, 1, tk, DH), lambda b, h, i, j: (b, h, xblk(j), 0))],
        out_specs=pl.BlockSpec((1, tq, DH), lambda b, h, i, j: (b, i, h)),
        out_shape=jax.ShapeDtypeStruct((bn, t, nh * DH), F32),
        scratch_shapes=[pltpu.VMEM((tq, 1), F32), pltpu.VMEM((tq, 1), F32), pltpu.VMEM((tq, DH), F32)],
        compiler_params=_cparams(("parallel", "parallel", "parallel", "arbitrary")),
        name="mla_attention",
    )(q, kc, vc, kx, vx)


def rope_tables(seq):
    rows = seq // GRID_W
    row = jnp.repeat(jnp.arange(rows), GRID_W).astype(F32)
    col = jnp.tile(jnp.arange(GRID_W), rows).astype(F32)
    n_freq = MLA_ROPE // 4
    inv = ROPE_THETA ** (-jnp.arange(n_freq, dtype=F32) / n_freq)
    ang = jnp.concatenate([row[:, None] * inv, col[:, None] * inv], axis=-1)
    cos, sin = jnp.cos(ang), jnp.sin(ang)
    half = MLA_ROPE // 2
    ones = jnp.ones((seq, MLA_NOPE), F32)
    zn = jnp.zeros((seq, MLA_NOPE), F32)
    zh = jnp.zeros((seq, half), F32)
    zp = jnp.zeros((seq, MLA_QK_PAD - MLA_QK), F32)
    cos_t = jnp.concatenate([ones, cos, cos, zp], axis=-1)
    sin_a = jnp.concatenate([zn, -sin, zh, zp], axis=-1)
    sin_b = jnp.concatenate([zn, zh, sin, zp], axis=-1)
    return cos_t, sin_a, sin_b


def _padcols(w, n):
    return jnp.concatenate([w, jnp.zeros((w.shape[0], n - w.shape[1]), w.dtype)], axis=1)


def pack_even(w):
    q, k, v, o = (w[:, j * MIX_W:(j + 1) * MIX_W] for j in range(4))
    gates = w[:, 4 * MIX_W:4 * MIX_W + 16]
    hy = w[:, 4 * MIX_W + 16:]
    return jnp.concatenate([q, k, v, hy, o, _padcols(gates, DH)], axis=1).astype(BF16)


def pack_odd(w):
    q, k, v, gg = (w[:, j * MIX_W:(j + 1) * MIX_W] for j in range(4))
    c0 = 4 * MIX_W
    gates = w[:, c0:c0 + 16]
    lq = w[:, c0 + 16:c0 + 16 + MLA_Q_RANK]
    lkv = w[:, c0 + 16 + MLA_Q_RANK:c0 + 16 + MLA_Q_RANK + MLA_KV_RANK]
    lkr = w[:, c0 + 16 + MLA_Q_RANK + MLA_KV_RANK:]
    return jnp.concatenate([q, k, v, gg, lkv, lq, _padcols(lkr, DH), _padcols(gates, DH)], axis=1).astype(BF16)


def kernel(x, c, ctx, c_ctx, mod_w, mod_b, a_w_in, a_i_bias, a_f_bias, a_norm_g, b_conv_w, b_conv_b, b_f_w1, b_f_b1, b_f_w2, b_f_b2, b_f_w3, b_f_b3, b_f_w4, b_f_freq, b_d, ab_w_out, cd_w_in, c_conv_w, c_conv_b, c_a_log, c_dt_bias, c_norm_g, d_q_norm_g, d_w_q_up, d_kv_norm_g, d_w_kv_up, d_qn_g, d_kn_g, cd_w_out, ffn_w_up, ffn_conv_w, ffn_conv_b, ffn_w_down):
    bsz, seq, d = x.shape
    ctx_len = ctx.shape[1]
    cvec = jnp.concatenate([c, c_ctx[None], jnp.zeros((8 - bsz - 1, d), F32)], axis=0)
    mods = mods_call(cvec, mod_w, mod_b)
    mx = lambda l, j: mods[l, :bsz, None, j * d:(j + 1) * d]
    mc = lambda l, j: jnp.broadcast_to(mods[l, bsz, j * d:(j + 1) * d], (bsz, 1, d))

    def ffn(l, h, m):
        return ffn_call(h, m(l, 3), m(l, 4), m(l, 5),
                        *ffn_weights(ffn_w_up[l], ffn_conv_w[l], ffn_conv_b[l], ffn_w_down[l]))

    w_in = pack_even(a_w_in[0])
    zx = inproj_call(x, mx(0, 0), mx(0, 1), w_in)
    zc = inproj_call(ctx, mc(0, 0), mc(0, 1), w_in)
    bias = jnp.concatenate([a_i_bias[0].reshape(2, N_HEADS), a_f_bias[0].reshape(2, N_HEADS)], axis=-1)[:, None, :]
    zero_state = (jnp.zeros((bsz, N_HEADS, DH, DH), F32), jnp.zeros((bsz, N_HEADS, 1, DH), F32),
                  jnp.zeros((bsz, N_HEADS, 1, DH), F32))
    hc, hx = [], []
    for dd in range(2):
        h_ctx, st = mlstm_call(dd, zc, bias, zero_state)
        hc.append(h_ctx)
        hx.append(mlstm_call(dd, zx, bias, st)[0])
    filt = (b_f_w1[0], b_f_b1[0], b_f_w2[0], b_f_b2[0], b_f_w3[0], b_f_b3[0], b_f_w4[0], b_f_freq[0])
    ux = dwconv_call(zx, E_HY // (3 * HYENA_W), b_conv_w[0], b_conv_b[0], False)
    uc = dwconv_call(zc, E_HY // (3 * HYENA_W), b_conv_w[0], b_conv_b[0], False)
    yx = hyena_long(ux, hy_filter_call(seq, *filt), b_d[0])
    yc = hyena_small(uc, hy_filter_call(ctx_len, *filt), b_d[0])
    w_out = ab_w_out[0].astype(BF16)
    g = a_norm_g[0].reshape(1, MIX_W)
    x = outproj_call("sigmoid", hx[0], hx[1], zx, E_MO // MIX_W, yx, g, w_out, x, mx(0, 2))
    ctx = outproj_call("sigmoid", hc[0], hc[1], zc, E_MO // MIX_W, yc, g, w_out, ctx, mc(0, 2))
    x = ffn(0, x, mx)
    ctx = ffn(0, ctx, mc)

    w_in = pack_odd(cd_w_in[0])
    zx = inproj_call(x, mx(1, 0), mx(1, 1), w_in)
    zc = inproj_call(ctx, mc(1, 0), mc(1, 1), w_in)
    qkv_x = dwconv_call(zx, 0, c_conv_w[0], c_conv_b[0], True)
    qkv_c = dwconv_call(zc, 0, c_conv_w[0], c_conv_b[0], True)
    zeros4 = jnp.zeros((2, 1, N_HEADS), F32)
    dtb = jnp.concatenate([c_dt_bias[0][:, None, :], zeros4], axis=-1)
    alog = jnp.concatenate([c_a_log[0][:, None, :], zeros4], axis=-1)
    ox = []
    for dd in range(2):
        _, s_ctx = gdn_call(dd, qkv_c, zc[..., O_GATE:O_GATE + 16], dtb, alog, zero_state[0])
        ox.append(gdn_call(dd, qkv_x, zx[..., O_GATE:O_GATE + 16], dtb, alog, s_ctx)[0])
    tabs = rope_tables(seq)
    kvg = d_kv_norm_g[0].reshape(1, -1)
    w_kv = d_w_kv_up[0].astype(BF16)
    kn_g = _padcols(d_kn_g[0].reshape(1, -1), MLA_QK_PAD)
    kc, vc = kv_call(zc, False, kvg, w_kv, kn_g, tuple(t[:ctx_len] for t in tabs))
    kx, vx = kv_call(zx, True, kvg, w_kv, kn_g, tabs)
    w_q = jnp.concatenate([_padcols(d_w_q_up[0][:, h * MLA_QK:(h + 1) * MLA_QK], MLA_QK_PAD)
                           for h in range(N_HEADS)], axis=1).astype(BF16)
    qx = q_call(zx, d_q_norm_g[0].reshape(1, -1), w_q, _padcols(d_qn_g[0].reshape(1, -1), MLA_QK_PAD), tabs)
    ax = attn_call(qx, kc, vc, kx, vx)
    g = jnp.tile(c_norm_g[0], N_HEADS).reshape(1, MIX_W)
    x = outproj_call("silu", ox[0], ox[1], zx, O_GG // MIX_W, ax, g, cd_w_out[0].astype(BF16), x, mx(1, 2))
    x = ffn(1, x, mx)
    return x
```

```python
import functools
import math

import numpy as np
import jax
import jax.numpy as jnp
from jax import lax
from jax.experimental import pallas as pl
from jax.experimental.pallas import tpu as pltpu

F32 = jnp.float32
BF16 = jnp.bfloat16
HI = lax.Precision.HIGHEST
LOG2E = math.log2(math.e)

D_MODEL = 1024
GRID_W = 64
EPS = 1e-6
CHUNK = 64
ROPE_THETA = 10000.0
N_HEADS = 4
DH = 128
MIX_W = N_HEADS * DH
HYENA_W = 512
HYENA_EMB = 33
HYENA_BANDS = 16
HYENA_FFN = 64
HYENA_NFILT = 4 * HYENA_W
HYENA_FAST_DECAY = 0.3
HYENA_SLOW_DECAY = 1.5
HYENA_TARGET = 1e-2
MLA_NOPE = 128
MLA_ROPE = 64
MLA_QK = 192
MLA_QK_PAD = 256
MLA_Q_RANK = 384
MLA_KV_RANK = 256
D_FF = 2816
FF_TILE = 256
N_FF_TILES = D_FF // FF_TILE
FFT_N1 = 128
SCAN_TB = 256
VMEM_LIMIT = 56 * 1024 * 1024

E_QKV, E_HY, E_MO, E_GATE, E_N = 0, 1536, 3072, 3584, 3712
O_QKV, O_GG, O_LKV, O_LQ, O_LKR, O_GATE, O_N = 0, 1536, 2048, 2304, 2688, 2816, 2944


def _cparams(sem):
    return pltpu.CompilerParams(dimension_semantics=sem, vmem_limit_bytes=VMEM_LIMIT)


def _dot(a, b):
    return jnp.dot(a.astype(BF16), b.astype(BF16), preferred_element_type=F32)


def _dot_nt(a, b):
    return lax.dot_general(a.astype(BF16), b.astype(BF16), (((1,), (1,)), ((), ())), preferred_element_type=F32)


def _dot_tn(a, b):
    return lax.dot_general(a.astype(BF16), b.astype(BF16), (((0,), (0,)), ((), ())), preferred_element_type=F32)


def _dot_hi(a, b):
    return jnp.dot(a, b, precision=HI, preferred_element_type=F32)


def _rms(x, n=None):
    n = x.shape[-1] if n is None else n
    return x * lax.rsqrt(jnp.sum(x * x, axis=-1, keepdims=True) * (1.0 / n) + EPS)


def _sigmoid(x):
    return 1.0 / (1.0 + jnp.exp(-x))


def _silu(x):
    return x * _sigmoid(x)


def _softplus(x):
    return jnp.maximum(x, 0.0) + jnp.log(1.0 + jnp.exp(-jnp.abs(x)))


def _log_sigmoid(x):
    return -_softplus(-x)


def _mods_body(c_ref, w_ref, b_ref, o_ref):
    o_ref[0] = _dot_hi(_silu(c_ref[...]), w_ref[0]) + b_ref[0]


def mods_call(cvec, mod_w, mod_b):
    depth, d, n = mod_w.shape
    tn = 1536
    return pl.pallas_call(
        _mods_body,
        grid=(depth, n // tn),
        in_specs=[pl.BlockSpec((8, d), lambda l, j: (0, 0)),
                  pl.BlockSpec((1, d, tn), lambda l, j: (l, 0, j)),
                  pl.BlockSpec((1, 1, tn), lambda l, j: (l, 0, j))],
        out_specs=pl.BlockSpec((1, 8, tn), lambda l, j: (l, 0, j)),
        out_shape=jax.ShapeDtypeStruct((depth, 8, n), F32),
        compiler_params=_cparams(("parallel", "parallel")),
        name="mods",
    )(cvec, mod_w, mod_b.reshape(depth, 1, n))


def _inproj_body(x_ref, sh_ref, sc_ref, w_ref, o_ref):
    h = _rms(x_ref[0]) * (1.0 + sc_ref[0]) + sh_ref[0]
    o_ref[0] = _dot(h, w_ref[...])


def inproj_call(x, shift, scale, w):
    bn, t, d = x.shape
    n = w.shape[1]
    tm = min(256, t)
    return pl.pallas_call(
        _inproj_body,
        grid=(bn, t // tm),
        in_specs=[pl.BlockSpec((1, tm, d), lambda b, i: (b, i, 0)),
                  pl.BlockSpec((1, 1, d), lambda b, i: (b, 0, 0)),
                  pl.BlockSpec((1, 1, d), lambda b, i: (b, 0, 0)),
                  pl.BlockSpec((d, n), lambda b, i: (0, 0))],
        out_specs=pl.BlockSpec((1, tm, n), lambda b, i: (b, i, 0)),
        out_shape=jax.ShapeDtypeStruct((bn, t, n), F32),
        compiler_params=_cparams(("parallel", "parallel")),
        name="inproj",
    )(x, shift, scale, w)


def _shift_rows(u, prev_row, next_row):
    tm = u.shape[0]
    rows = lax.broadcasted_iota(jnp.int32, (tm, 1), 0)
    um = jnp.where(rows == 0, prev_row, pltpu.roll(u, 1, 0))
    up = jnp.where(rows == tm - 1, next_row, pltpu.roll(u, tm - 1, 0))
    return um, up


def _dwconv_body(gdn, z_ref, zp_ref, zn_ref, w_ref, b_ref, o_ref):
    i, nt = pl.program_id(1), pl.num_programs(1)
    u = z_ref[0]
    prev_row = zp_ref[0][7:8, :] * (i > 0).astype(F32)
    next_row = zn_ref[0][0:1, :] * (i < nt - 1).astype(F32)
    um, up = _shift_rows(u, prev_row, next_row)
    y = um * w_ref[0:1, :] + u * w_ref[1:2, :] + up * w_ref[2:3, :] + b_ref[...]
    if not gdn:
        o_ref[0] = y
        return
    y = _silu(y)
    for h in range(2 * N_HEADS):
        s = y[:, h * DH:(h + 1) * DH]
        s = s * lax.rsqrt(jnp.sum(s * s, axis=-1, keepdims=True) + EPS)
        if h < N_HEADS:
            s = s * DH ** -0.5
        o_ref[0, :, h * DH:(h + 1) * DH] = s
    o_ref[0, :, 2 * MIX_W:] = y[:, 2 * MIX_W:]


def dwconv_call(z, col_block, w, b, gdn):
    bn, t, _ = z.shape
    c = w.shape[1]
    tm = min(512, t)
    r8 = tm // 8
    return pl.pallas_call(
        functools.partial(_dwconv_body, gdn),
        grid=(bn, t // tm),
        in_specs=[pl.BlockSpec((1, tm, c), lambda b_, i: (b_, i, col_block)),
                  pl.BlockSpec((1, 8, c), lambda b_, i: (b_, jnp.maximum(i * r8 - 1, 0), col_block)),
                  pl.BlockSpec((1, 8, c), lambda b_, i: (b_, jnp.minimum((i + 1) * r8, t // 8 - 1), col_block)),
                  pl.BlockSpec((3, c), lambda b_, i: (0, 0)),
                  pl.BlockSpec((1, c), lambda b_, i: (0, 0))],
        out_specs=pl.BlockSpec((1, tm, c), lambda b_, i: (b_, i, 0)),
        out_shape=jax.ShapeDtypeStruct((bn, t, c), F32),
        compiler_params=_cparams(("parallel", "parallel")),
        name="dwconv_gdn" if gdn else "dwconv",
    )(z, z, z, w, b.reshape(1, c))


def _block_masks(d, tb):
    sh = CHUNK.bit_length() - 1
    ii = lax.broadcasted_iota(jnp.int32, (tb, tb), 0)
    jj = lax.broadcasted_iota(jnp.int32, (tb, tb), 1)
    same = (ii >> sh) == (jj >> sh)
    vis = jnp.logical_and(same, (ii >= jj) if d == 0 else (ii <= jj))
    vis_t = jnp.logical_and(same, (jj >= ii) if d == 0 else (jj <= ii))
    return ii, jj, same, vis, vis_t


def _chunk_order(d, nc):
    return range(nc) if d == 0 else reversed(range(nc))


def _lockstep(gens):
    gens = list(gens)
    results = [None] * len(gens)
    live = list(range(len(gens)))
    while live:
        for i in list(live):
            try:
                next(gens[i])
            except StopIteration as stop:
                results[i] = stop.value
                live.remove(i)
    return results


def _mlstm_block(d, q, k, v, li_c, lf_c, li_r, lf_r, c_mat, n_vec, m):
    tb = q.shape[0]
    nc = tb // CHUNK
    _, _, same, vis, vis_t = _block_masks(d, tb)
    b_c = jnp.sum(vis.astype(F32) * lf_r, axis=1, keepdims=True)
    b_r = jnp.sum(vis_t.astype(F32) * lf_c, axis=0, keepdims=True)
    bl_c = jnp.sum(same.astype(F32) * lf_r, axis=1, keepdims=True)
    a_c = bl_c - b_c + li_c
    dmat = jnp.where(vis, b_c - b_r + li_r, -jnp.inf)
    m_in = jnp.max(dmat, axis=1, keepdims=True)
    yield
    s = _dot_nt(q, k) * jnp.exp(dmat - m_in)
    yield
    num_in = _dot(s, v)
    den_in = jnp.sum(s, axis=1, keepdims=True)
    yield
    hs = [None] * nc
    for c in _chunk_order(d, nc):
        sl = slice(c * CHUNK, (c + 1) * CHUNK)
        b_last = bl_c[c * CHUNK:c * CHUNK + 1, :]
        m_loc = jnp.max(a_c[sl], axis=0, keepdims=True)
        m_inter = b_c[sl] + m
        m_t = jnp.maximum(m_inter, m_in[sl])
        s_inter, s_in = jnp.exp(m_inter - m_t), jnp.exp(m_in[sl] - m_t)
        num = s_inter * _dot_nt(q[sl], c_mat) + s_in * num_in[sl]
        den = s_inter * jnp.sum(q[sl] * n_vec, axis=1, keepdims=True) + s_in * den_in[sl]
        hs[c] = num / jnp.maximum(jnp.abs(den), jnp.exp(-m_t))
        m_new = jnp.maximum(b_last + m, m_loc)
        f_old, f_loc = jnp.exp(b_last + m - m_new), jnp.exp(m_loc - m_new)
        kw = k[sl] * jnp.exp(a_c[sl] - m_loc)
        c_mat = f_old * c_mat + f_loc * _dot_tn(v[sl], kw)
        n_vec = f_old * n_vec + f_loc * jnp.sum(kw, axis=0, keepdims=True)
        m = m_new
        yield
    return jnp.concatenate(hs, axis=0), c_mat, n_vec, m


def _mlstm_body(d, q_ref, k_ref, v_ref, gc_ref, gr_ref, bias_ref, c0_ref, n0_ref, m0_ref,
                h_ref, c_ref, n_ref, m_ref):
    @pl.when(pl.program_id(1) == 0)
    def _():
        c_ref[...] = c0_ref[...]
        n_ref[...] = n0_ref[...]
        m_ref[...] = m0_ref[...]

    gc = gc_ref[0, 0]
    gr = gr_ref[0, 0]

    def head(h):
        hs = slice(h * DH, (h + 1) * DH)
        bi = bias_ref[0, :, h:h + 1]
        bf = bias_ref[0, :, N_HEADS + h:N_HEADS + h + 1]
        li_c = gc[:, h:h + 1] + bi
        lf_c = _log_sigmoid(gc[:, N_HEADS + h:N_HEADS + h + 1] + bf)
        li_r = gr[h:h + 1, :] + bi
        lf_r = _log_sigmoid(gr[N_HEADS + h:N_HEADS + h + 1, :] + bf)
        return _mlstm_block(d, q_ref[0, :, hs], k_ref[0, :, hs] * DH ** -0.5, v_ref[0, :, hs],
                            li_c, lf_c, li_r, lf_r, c_ref[0, h], n_ref[0, h], m_ref[0, h][:, 0:1])

    for h, (hv, c_new, n_new, m_new) in enumerate(_lockstep(head(h) for h in range(N_HEADS))):
        h_ref[0, :, h * DH:(h + 1) * DH] = hv
        c_ref[0, h] = c_new
        n_ref[0, h] = n_new
        m_ref[0, h] = jnp.broadcast_to(m_new, (1, DH))


def _gdn_block(d, q, k, v, bt_c, la_c, la_r, s_mat):
    tb = q.shape[0]
    nc = tb // CHUNK
    sh = CHUNK.bit_length() - 1
    ii, jj, same, vis, vis_t = _block_masks(d, tb)
    g_c = jnp.sum(vis.astype(F32) * la_r, axis=1, keepdims=True)
    g_r = jnp.sum(vis_t.astype(F32) * la_c, axis=0, keepdims=True)
    gl_c = jnp.sum(same.astype(F32) * la_r, axis=1, keepdims=True)
    decay = jnp.exp(jnp.where(vis, g_c - g_r, -jnp.inf))
    kb = k * bt_c
    yield
    a_mat = jnp.where(ii == jj, 0.0, _dot_nt(kb, k) * decay)
    attn = _dot_nt(q, k) * decay
    yield
    off = lambda lvl: jnp.logical_and((ii >> (lvl + 1)) == (jj >> (lvl + 1)), (ii >> lvl) != (jj >> lvl))
    t_mat = (ii == jj).astype(F32) - jnp.where(off(0), a_mat, 0.0)
    for lvl in range(1, sh):
        te = _dot(t_mat, jnp.where(off(lvl), a_mat, 0.0))
        yield
        t_mat = t_mat - _dot(te, t_mat)
        yield
    eg = jnp.exp(g_c)
    u = _dot(t_mat, v * bt_c)
    w = _dot(t_mat, kb * eg)
    kd = k * jnp.exp(gl_c - g_c)
    qg = q * eg
    yield
    v_new, o_state = [None] * nc, [None] * nc
    for c in _chunk_order(d, nc):
        sl = slice(c * CHUNK, (c + 1) * CHUNK)
        v_new[c] = u[sl] - _dot(w[sl], s_mat)
        o_state[c] = _dot(qg[sl], s_mat)
        yield
        s_mat = s_mat * jnp.exp(gl_c[c * CHUNK:c * CHUNK + 1, :]) + _dot_tn(kd[sl], v_new[c])
        yield
    o = jnp.concatenate(o_state, axis=0) + _dot(attn, jnp.concatenate(v_new, axis=0))
    return o, s_mat


def _gdn_body(d, q_ref, k_ref, v_ref, gc_ref, gr_ref, dtb_ref, alog_ref, s0_ref, o_ref, s_ref):
    @pl.when(pl.program_id(1) == 0)
    def _():
        s_ref[...] = s0_ref[...]

    gc = gc_ref[0, 0]
    gr = gr_ref[0, 0]

    def head(h):
        hs = slice(h * DH, (h + 1) * DH)
        dtb = dtb_ref[0, :, h:h + 1]
        neg_a = -jnp.exp(alog_ref[0, :, h:h + 1])
        bt_c = _sigmoid(gc[:, h:h + 1])
        la_c = neg_a * _softplus(gc[:, N_HEADS + h:N_HEADS + h + 1] + dtb)
        la_r = neg_a * _softplus(gr[N_HEADS + h:N_HEADS + h + 1, :] + dtb)
        return _gdn_block(d, q_ref[0, :, hs], k_ref[0, :, hs], v_ref[0, :, hs], bt_c, la_c, la_r, s_ref[0, h])

    for h, (ov, s_new) in enumerate(_lockstep(head(h) for h in range(N_HEADS))):
        o_ref[0, :, h * DH:(h + 1) * DH] = ov
        s_ref[0, h] = s_new


def _gate_views(graw):
    bsz, t, _ = graw.shape
    gcol = graw.reshape(bsz, t, 2, 2, N_HEADS).transpose(0, 3, 1, 2, 4).reshape(bsz, 2, t, 8)
    return gcol, gcol.transpose(0, 1, 3, 2)


def _scan_specs(d, t, col0):
    tb = min(SCAN_TB, t)
    nb = t // tb
    blk = (lambda i: i) if d == 0 else (lambda i: nb - 1 - i)
    qkv = [pl.BlockSpec((1, tb, MIX_W), lambda b, i, j=j: (b, blk(i), col0 + j)) for j in range(3)]
    gates = [pl.BlockSpec((1, 1, tb, 8), lambda b, i: (b, d, blk(i), 0)),
             pl.BlockSpec((1, 1, 8, tb), lambda b, i: (b, d, 0, blk(i)))]
    par = pl.BlockSpec((1, 1, 8), lambda b, i: (d, 0, 0))
    out = pl.BlockSpec((1, tb, MIX_W), lambda b, i: (b, blk(i), 0))
    return nb, qkv, gates, par, out


def _state_spec(*tail):
    return pl.BlockSpec((1, N_HEADS) + tail, lambda b, i: (b, 0) + (0,) * len(tail))


def mlstm_call(d, z, bias, state):
    bsz, t, _ = z.shape
    nb, qkv, gates, par, out = _scan_specs(d, t, E_QKV // MIX_W)
    gcol, grow = _gate_views(z[..., E_GATE:E_GATE + 16])
    st_specs = [_state_spec(DH, DH), _state_spec(1, DH), _state_spec(1, DH)]
    st_shapes = [jax.ShapeDtypeStruct((bsz, N_HEADS, DH, DH), F32),
                 jax.ShapeDtypeStruct((bsz, N_HEADS, 1, DH), F32),
                 jax.ShapeDtypeStruct((bsz, N_HEADS, 1, DH), F32)]
    res = pl.pallas_call(
        functools.partial(_mlstm_body, d),
        grid=(bsz, nb),
        in_specs=qkv + gates + [par] + st_specs,
        out_specs=[out] + st_specs,
        out_shape=[jax.ShapeDtypeStruct((bsz, t, MIX_W), F32)] + st_shapes,
        compiler_params=_cparams(("parallel", "arbitrary")),
        name="mlstm_scan_fwd" if d == 0 else "mlstm_scan_bwd",
    )(z, z, z, gcol, grow, bias, *state)
    return res[0], tuple(res[1:])


def gdn_call(d, qkv_arr, graw, dtb, alog, state):
    bsz, t, _ = qkv_arr.shape
    nb, qkv, gates, par, out = _scan_specs(d, t, 0)
    gcol, grow = _gate_views(graw)
    st_spec = _state_spec(DH, DH)
    return pl.pallas_call(
        functools.partial(_gdn_body, d),
        grid=(bsz, nb),
        in_specs=qkv + gates + [par, par, st_spec],
        out_specs=[out, st_spec],
        out_shape=[jax.ShapeDtypeStruct((bsz, t, MIX_W), F32),
                   jax.ShapeDtypeStruct((bsz, N_HEADS, DH, DH), F32)],
        compiler_params=_cparams(("parallel", "arbitrary")),
        name="gdn_scan_fwd" if d == 0 else "gdn_scan_bwd",
    )(qkv_arr, qkv_arr, qkv_arr, gcol, grow, dtb, alog, state)


def _outproj_body(act, hf_ref, hb_ref, gate_ref, y_ref, g_ref, w_ref, x_ref, mg_ref, o_ref):
    hs = hf_ref[0] + hb_ref[0]
    hn = jnp.concatenate([_rms(hs[:, h * DH:(h + 1) * DH]) for h in range(N_HEADS)], axis=-1) * g_ref[...]
    gate = gate_ref[0]
    hn = hn * (_sigmoid(gate) if act == "sigmoid" else _silu(gate))
    out = _dot(hn, w_ref[0:MIX_W, :]) + _dot(y_ref[0], w_ref[MIX_W:, :])
    o_ref[0] = x_ref[0] + mg_ref[0] * out


def outproj_call(act, hf, hb, z, gate_block, y, g, w, x, mgate):
    bn, t, d = x.shape
    tm = min(512, t)
    return pl.pallas_call(
        functools.partial(_outproj_body, act),
        grid=(bn, t // tm),
        in_specs=[pl.BlockSpec((1, tm, MIX_W), lambda b, i: (b, i, 0)),
                  pl.BlockSpec((1, tm, MIX_W), lambda b, i: (b, i, 0)),
                  pl.BlockSpec((1, tm, MIX_W), lambda b, i: (b, i, gate_block)),
                  pl.BlockSpec((1, tm, MIX_W), lambda b, i: (b, i, 0)),
                  pl.BlockSpec((1, MIX_W), lambda b, i: (0, 0)),
                  pl.BlockSpec((2 * MIX_W, d), lambda b, i: (0, 0)),
                  pl.BlockSpec((1, tm, d), lambda b, i: (b, i, 0)),
                  pl.BlockSpec((1, 1, d), lambda b, i: (b, 0, 0))],
        out_specs=pl.BlockSpec((1, tm, d), lambda b, i: (b, i, 0)),
        out_shape=jax.ShapeDtypeStruct((bn, t, d), F32),
        compiler_params=_cparams(("parallel", "parallel")),
        name="outproj_" + act,
    )(hf, hb, z, y, g, w, x, mgate)


def _ffn_body(x_ref, xp_ref, xn_ref, sh_ref, sc_ref, mg_ref, wa_ref, wb_ref, cwa_ref, cwb_ref, wd_ref, o_ref):
    i, nt = pl.program_id(1), pl.num_programs(1)
    sc, sh = 1.0 + sc_ref[0], sh_ref[0]
    x = x_ref[0]
    tm = x.shape[0]
    mod = lambda v: _rms(v) * sc + sh
    hp = mod(xp_ref[0]) * (i > 0).astype(F32)
    hn = mod(xn_ref[0]) * (i < nt - 1).astype(F32)
    h = jnp.concatenate([hp, mod(x), hn], axis=0).astype(BF16)
    rows = tm + 16

    def conv(w_ref, cw_ref, j):
        u = jnp.dot(h, w_ref[j], preferred_element_type=F32)
        cw = cw_ref[j]
        y = pltpu.roll(u, 1, 0) * cw[0:1, :] + u * cw[1:2, :] + pltpu.roll(u, rows - 1, 0) * cw[2:3, :] + cw[3:4, :]
        return y[8:8 + tm]

    def tile(j, acc):
        a, b = conv(wa_ref, cwa_ref, j), conv(wb_ref, cwb_ref, j)
        gate = a * (0.5 + 0.5 * jnp.tanh(0.5 * a))
        return acc + _dot(gate * b, wd_ref[j])

    acc = lax.fori_loop(0, N_FF_TILES, tile, jnp.zeros(x.shape, F32), unroll=True)
    o_ref[0] = x + mg_ref[0] * acc


def ffn_call(x, shift, scale, mgate, wa, wb, cwa, cwb, wd):
    bn, t, d = x.shape
    tm = min(512, t)
    r8 = tm // 8
    full = lambda a: pl.BlockSpec(a.shape, lambda b, i: (0,) * a.ndim)
    vec = pl.BlockSpec((1, 1, d), lambda b, i: (b, 0, 0))
    return pl.pallas_call(
        _ffn_body,
        grid=(bn, t // tm),
        in_specs=[pl.BlockSpec((1, tm, d), lambda b, i: (b, i, 0)),
                  pl.BlockSpec((1, 8, d), lambda b, i: (b, jnp.maximum(i * r8 - 1, 0), 0)),
                  pl.BlockSpec((1, 8, d), lambda b, i: (b, jnp.minimum((i + 1) * r8, t // 8 - 1), 0)),
                  vec, vec, vec, full(wa), full(wb), full(cwa), full(cwb), full(wd)],
        out_specs=pl.BlockSpec((1, tm, d), lambda b, i: (b, i, 0)),
        out_shape=jax.ShapeDtypeStruct((bn, t, d), F32),
        compiler_params=_cparams(("parallel", "parallel")),
        name="conv_ffn",
    )(x, x, x, shift, scale, mgate, wa, wb, cwa, cwb, wd)


def ffn_weights(w_up, conv_w, conv_b, w_down):
    d = w_up.shape[0]
    tiles = lambda a: a.reshape(a.shape[0], N_FF_TILES, FF_TILE).transpose(1, 0, 2)
    cw = jnp.concatenate([conv_w, conv_b[None]], axis=0)
    return (tiles(w_up[:, :D_FF]).astype(BF16), tiles(w_up[:, D_FF:]).astype(BF16),
            tiles(cw[:, :D_FF]), tiles(cw[:, D_FF:]),
            w_down.reshape(N_FF_TILES, FF_TILE, d).astype(BF16))


def _hy_filter_body(lf, w1t_ref, w1c_ref, w1s_ref, b1_ref, w2_ref, b2_ref, w3_ref, b3_ref, w4_ref, fr_ref,
                    band_ref, dl_ref, o_ref, l1_ref):
    r = pl.program_id(0)
    tr = o_ref.shape[0]
    n = (r * tr + lax.broadcasted_iota(jnp.int32, (tr, 1), 0))
    t = jnp.where(n < lf, n, 2 * lf - n).astype(F32)
    t_norm = t / (lf - 1)
    ang = (2.0 * math.pi * t / lf) * band_ref[...]
    fr = fr_ref[...]
    pre = t_norm * w1t_ref[...] + _dot_hi(jnp.cos(ang), w1c_ref[...]) - _dot_hi(jnp.sin(ang), w1s_ref[...])
    hdn = jnp.sin(fr * (pre + b1_ref[...]))
    hdn = jnp.sin(fr * (_dot_hi(hdn, w2_ref[...]) + b2_ref[...]))
    hdn = jnp.sin(fr * (_dot_hi(hdn, w3_ref[...]) + b3_ref[...]))
    hval = _dot(hdn, w4_ref[0]) * jnp.exp(-t_norm * jnp.abs(dl_ref[0]))
    hval = jnp.where(n == lf, 0.0, hval)
    o_ref[...] = hval

    @pl.when(r == 0)
    def _():
        l1_ref[...] = jnp.zeros_like(l1_ref)

    l1_ref[...] += jnp.sum(jnp.abs(hval), axis=0, keepdims=True)


def hy_filter_call(lf, f_w1, f_b1, f_w2, f_b2, f_w3, f_b3, f_w4, f_freq):
    tr = min(512, lf)
    nt = 2 * lf // tr
    half = lf // tr
    ncol = 2 * HYENA_W
    pad = lambda a: jnp.concatenate([a, jnp.zeros((DH - a.shape[0], a.shape[1]), F32)], axis=0)
    w4 = f_w4.reshape(HYENA_FFN, 2, 2, HYENA_W).transpose(2, 0, 1, 3).reshape(2, HYENA_FFN, ncol)
    deltas = jnp.linspace(math.log(HYENA_TARGET) / HYENA_SLOW_DECAY, math.log(HYENA_TARGET) / HYENA_FAST_DECAY,
                          HYENA_NFILT, dtype=F32)
    deltas = deltas.reshape(2, 2, HYENA_W).transpose(1, 0, 2).reshape(2, 1, ncol)
    bands = jnp.linspace(1e-4, HYENA_BANDS - 1, HYENA_BANDS, dtype=F32)
    bands = jnp.concatenate([bands, jnp.zeros((DH - HYENA_BANDS,), F32)]).reshape(1, DH)
    row = lambda a: a.reshape(1, -1)
    args = (f_w1[0:1], pad(f_w1[1:1 + HYENA_BANDS]), pad(f_w1[1 + HYENA_BANDS:]), row(f_b1), f_w2, row(f_b2),
            f_w3, row(f_b3), w4, row(f_freq), bands, deltas)
    full = lambda a: pl.BlockSpec(a.shape, lambda r: (0,) * a.ndim)
    by_dir = lambda a: pl.BlockSpec((1,) + a.shape[1:], lambda r: (r // half, 0, 0))
    in_specs = [by_dir(a) if a.ndim == 3 else full(a) for a in args]
    return pl.pallas_call(
        functools.partial(_hy_filter_body, lf),
        grid=(nt,),
        in_specs=in_specs,
        out_specs=[pl.BlockSpec((tr, ncol), lambda r: (r, 0)), pl.BlockSpec((1, ncol), lambda r: (0, 0))],
        out_shape=[jax.ShapeDtypeStruct((2 * lf, ncol), F32), jax.ShapeDtypeStruct((1, ncol), F32)],
        compiler_params=_cparams(("arbitrary",)),
        name="hyena_filter",
    )(*args)


def _dft_consts(lf):
    n1 = FFT_N1
    n = 2 * lf
    assert n == n1 * n1
    idx = np.arange(n1)
    f = np.exp(-2j * np.pi * np.outer(idx, idx) / n1)
    tw = np.exp(-2j * np.pi * np.outer(idx, idx) / n)
    fh = f[:, :n1 // 2]
    blk = lambda a: np.block([[a.real, -a.imag], [a.imag, a.real]])
    f_sig = blk(fh)
    f_flt = np.concatenate([f.real, f.imag], axis=0)
    c_inv = blk(np.conj(f)[:n1 // 2, :] / n)
    c = lambda a: jnp.asarray(a, dtype=F32)
    cb = lambda a: jnp.asarray(a, dtype=F32).astype(BF16)
    return dict(f_sig=cb(f_sig), f_flt=cb(f_flt), c_inv=cb(c_inv), fr=c(f.real), fi=c(f.imag),
                twr=c(tw.real.reshape(n1, 1, n1)), twi=c(tw.imag.reshape(n1, 1, n1)))


def _lead_mm_body(m_ref, x_ref, o_ref):
    o_ref[...] = _dot(m_ref[...], x_ref[...]).astype(o_ref.dtype)


def lead_mm_call(mat, x2d, name):
    r, k = mat.shape
    nl = x2d.shape[1]
    tn = 4096
    return pl.pallas_call(
        _lead_mm_body,
        grid=(nl // tn,),
        in_specs=[pl.BlockSpec((r, k), lambda j: (0, 0)), pl.BlockSpec((k, tn), lambda j: (0, j))],
        out_specs=pl.BlockSpec((r, tn), lambda j: (0, j)),
        out_shape=jax.ShapeDtypeStruct((r, nl), BF16),
        compiler_params=_cparams(("parallel",)),
        name=name,
    )(mat, x2d)


def _stage2_mat(fr, fi, twr, twi):
    hr = fr * twr - fi * twi
    hi = fr * twi + fi * twr
    return hr, hi


def _blk(ar, ai):
    return jnp.concatenate([jnp.concatenate([ar, -ai], axis=1), jnp.concatenate([ai, ar], axis=1)], axis=0)


def _spec_filter_body(a_ref, l1_ref, fr_ref, fi_ref, twr_ref, twi_ref, o_ref):
    hr, hi = _stage2_mat(fr_ref[...], fi_ref[...], twr_ref[0], twi_ref[0])
    n1 = hr.shape[0]
    o_ref[0] = _dot(_blk(hr, hi), a_ref[...].reshape(2 * n1, a_ref.shape[-1])) * (1.0 / l1_ref[...])


def _spec_conv_body(a_ref, k_ref, fr_ref, fi_ref, twr_ref, twi_ref, o_ref):
    hr, hi = _stage2_mat(fr_ref[...], fi_ref[...], twr_ref[0], twi_ref[0])
    n1 = hr.shape[0]
    x = _dot(_blk(hr, hi), a_ref[...].reshape(2 * n1, a_ref.shape[-1]))
    xr, xi = x[:n1], x[n1:]
    kr, ki = k_ref[0, :n1], k_ref[0, n1:]
    y = jnp.concatenate([xr * kr - xi * ki, xr * ki + xi * kr], axis=0)
    o_ref[0] = _dot(_blk(hr.T, -(hi.T)), y).astype(o_ref.dtype)


def spec_call(a2d, kspec, order, cst, l1=None):
    n1 = FFT_N1
    c = a2d.shape[1] // n1
    ct = 512
    a4 = a2d.reshape(2, n1, n1, c)
    sq = pl.BlockSpec((n1, n1), lambda k, j: (0, 0))
    tw = pl.BlockSpec((1, 1, n1), lambda k, j: (k, 0, 0))
    a_spec = pl.BlockSpec((2, 1, n1, ct), lambda k, j: (0, k, 0, j))
    o_spec = pl.BlockSpec((1, 2 * n1, ct), lambda k, j: (k, 0, j))
    common = dict(grid=(n1, c // ct), out_specs=o_spec, compiler_params=_cparams(("parallel", "parallel")))
    if kspec is None:
        l1_spec = pl.BlockSpec((1, ct), lambda k, j: (0, j))
        return pl.pallas_call(_spec_filter_body, in_specs=[a_spec, l1_spec, sq, sq, tw, tw], name="hyena_spec_filter",
                              out_shape=jax.ShapeDtypeStruct((n1, 2 * n1, c), F32),
                              **common)(a4, l1, cst["fr"], cst["fi"], cst["twr"], cst["twi"])
    k_spec = pl.BlockSpec((1, 2 * n1, ct), lambda k, j: (k, 0, order * (c // ct) + j))
    return pl.pallas_call(_spec_conv_body, in_specs=[a_spec, k_spec, sq, sq, tw, tw], name="hyena_spec_conv",
                          out_shape=jax.ShapeDtypeStruct((n1, 2 * n1, c), BF16),
                          **common)(a4, kspec, cst["fr"], cst["fi"], cst["twr"], cst["twi"])


def _inv_gate_body(m_ref, br_ref, bi_ref, v_ref, x_ref, d_ref, o_ref):
    y = _dot(m_ref[...], jnp.concatenate([br_ref[...], bi_ref[...]], axis=0))
    o_ref[...] = x_ref[...] * (y + v_ref[...] * d_ref[...])


def inv_gate_call(b3, v2d, x2d, dvec, cst):
    n1 = FFT_N1
    nl = v2d.shape[1]
    tn = 2048
    nb = nl // tn
    b2d = b3.reshape(n1, 2 * nl)
    tile = pl.BlockSpec((n1, tn), lambda j: (0, j))
    return pl.pallas_call(
        _inv_gate_body,
        grid=(nb,),
        in_specs=[pl.BlockSpec((n1, 2 * n1), lambda j: (0, 0)), tile,
                  pl.BlockSpec((n1, tn), lambda j: (0, nb + j)), tile, tile,
                  pl.BlockSpec((1, tn), lambda j: (0, j))],
        out_specs=tile,
        out_shape=jax.ShapeDtypeStruct((n1, nl), F32),
        compiler_params=_cparams(("parallel",)),
        name="hyena_inv_gate",
    )(cst["c_inv"], b2d, b2d, v2d, x2d, dvec)


def hyena_long(u, kf, l1, d_skip):
    bsz, lf, _ = u.shape
    assert bsz == 2
    n1 = FFT_N1
    cst = _dft_consts(lf)
    rb = lambda a: a.reshape(bsz * n1 // 2, n1 * HYENA_W)
    v, x1, x2 = (rb(u[..., j * HYENA_W:(j + 1) * HYENA_W]) for j in range(3))
    kspec = spec_call(lead_mm_call(cst["f_flt"], kf.reshape(n1, n1 * 2 * HYENA_W), "hyena_lead_filter"),
                      None, 0, cst, l1)
    y = v
    for o, xg in enumerate((x1, x2)):
        a = lead_mm_call(cst["f_sig"], y, "hyena_lead_signal")
        b3 = spec_call(a, kspec, o, cst)
        y = inv_gate_call(b3, y, xg, jnp.tile(d_skip[o], n1).reshape(1, -1), cst)
    return y.reshape(bsz, lf, HYENA_W)


def _hyena_small_body(u_ref, kf_ref, l1_ref, fs_ref, ff_ref, ci_ref, d_ref, o_ref):
    w = HYENA_W
    nk = ff_ref.shape[0] // 2
    kspec = _dot(ff_ref[...], kf_ref[...]) * (1.0 / l1_ref[...])
    ucat = jnp.concatenate([u_ref[0], u_ref[1]], axis=0)
    y = ucat[:, :w]
    for o in range(2):
        x = _dot(fs_ref[...], y)
        xr, xi = x[:nk], x[nk:]
        kr, ki = kspec[:nk, o * w:(o + 1) * w], kspec[nk:, o * w:(o + 1) * w]
        prod = jnp.concatenate([xr * kr - xi * ki, xr * ki + xi * kr], axis=0)
        conv = _dot(ci_ref[...], prod)
        y = ucat[:, (o + 1) * w:(o + 2) * w] * (conv + y * d_ref[o:o + 1, :])
    lf = u_ref.shape[1]
    o_ref[0] = y[:lf]
    o_ref[1] = y[lf:]


def hyena_small(u, kf, l1, d_skip):
    bsz, lf, _ = u.shape
    assert bsz == 2
    n = 2 * lf
    idx = np.arange(n)
    f = np.exp(-2j * np.pi * np.outer(idx, idx) / n)
    blk = lambda a: np.block([[a.real, -a.imag], [a.imag, a.real]])
    cb = lambda a: jnp.asarray(a, F32).astype(BF16)
    fs = cb(blk(f[:, :lf]))
    ff = cb(np.concatenate([f.real, f.imag], axis=0))
    ci = cb(blk(np.conj(f)[:lf, :] / n))
    return pl.pallas_call(
        _hyena_small_body,
        out_shape=jax.ShapeDtypeStruct((bsz, lf, HYENA_W), F32),
        compiler_params=pltpu.CompilerParams(vmem_limit_bytes=VMEM_LIMIT),
        name="hyena_small",
    )(u, kf, l1, fs, ff, ci, d_skip)


def _rope(x, cos_ref, sa_ref, sb_ref):
    w = x.shape[-1]
    half = MLA_ROPE // 2
    return x * cos_ref[...] + pltpu.roll(x, w - half, 1) * sa_ref[...] + pltpu.roll(x, half, 1) * sb_ref[...]


def _kv_body(rope, lkv_ref, lkr_ref, g_ref, w_ref, kg_ref, cos_ref, sa_ref, sb_ref, k_ref, v_ref):
    kv = _dot(_rms(lkv_ref[0]) * g_ref[...], w_ref[...])
    kr = lkr_ref[0]
    for h in range(N_HEADS):
        base = h * (MLA_NOPE + DH)
        k = jnp.concatenate([kv[:, base:base + MLA_NOPE], kr], axis=-1)
        k = _rms(k, MLA_QK) * kg_ref[...]
        if rope:
            k = _rope(k, cos_ref, sa_ref, sb_ref)
        k_ref[0, h] = k.astype(BF16)
        v_ref[0, h] = kv[:, base + MLA_NOPE:base + MLA_NOPE + DH].astype(BF16)


def kv_call(z, rope, g, w, kg, tabs):
    bn, t, _ = z.shape
    tm = min(512, t)
    tab = pl.BlockSpec((tm, MLA_QK_PAD), lambda b, i: (i, 0))
    return pl.pallas_call(
        functools.partial(_kv_body, rope),
        grid=(bn, t // tm),
        in_specs=[pl.BlockSpec((1, tm, MLA_KV_RANK), lambda b, i: (b, i, O_LKV // MLA_KV_RANK)),
                  pl.BlockSpec((1, tm, DH), lambda b, i: (b, i, O_LKR // DH)),
                  pl.BlockSpec((1, MLA_KV_RANK), lambda b, i: (0, 0)),
                  pl.BlockSpec(w.shape, lambda b, i: (0, 0)),
                  pl.BlockSpec((1, MLA_QK_PAD), lambda b, i: (0, 0)), tab, tab, tab],
        out_specs=[pl.BlockSpec((1, N_HEADS, tm, MLA_QK_PAD), lambda b, i: (b, 0, i, 0)),
                   pl.BlockSpec((1, N_HEADS, tm, DH), lambda b, i: (b, 0, i, 0))],
        out_shape=[jax.ShapeDtypeStruct((bn, N_HEADS, t, MLA_QK_PAD), BF16),
                   jax.ShapeDtypeStruct((bn, N_HEADS, t, DH), BF16)],
        compiler_params=_cparams(("parallel", "parallel")),
        name="mla_kv_rope" if rope else "mla_kv",
    )(z, z, g, w, kg, *tabs)


def _q_body(lq_ref, g_ref, w_ref, qg_ref, cos_ref, sa_ref, sb_ref, q_ref):
    q = _dot(_rms(lq_ref[0]) * g_ref[...], w_ref[...])
    for h in range(N_HEADS):
        qh = _rms(q[:, h * MLA_QK_PAD:(h + 1) * MLA_QK_PAD], MLA_QK) * qg_ref[...]
        qh = _rope(qh, cos_ref, sa_ref, sb_ref) * (MLA_QK ** -0.5 * LOG2E)
        q_ref[0, h] = qh.astype(BF16)


def q_call(z, g, w, qg, tabs):
    bn, t, _ = z.shape
    tm = min(512, t)
    tab = pl.BlockSpec((tm, MLA_QK_PAD), lambda b, i: (i, 0))
    return pl.pallas_call(
        _q_body,
        grid=(bn, t // tm),
        in_specs=[pl.BlockSpec((1, tm, MLA_Q_RANK), lambda b, i: (b, i, O_LQ // MLA_Q_RANK)),
                  pl.BlockSpec((1, MLA_Q_RANK), lambda b, i: (0, 0)),
                  pl.BlockSpec(w.shape, lambda b, i: (0, 0)),
                  pl.BlockSpec((1, MLA_QK_PAD), lambda b, i: (0, 0)), tab, tab, tab],
        out_specs=pl.BlockSpec((1, N_HEADS, tm, MLA_QK_PAD), lambda b, i: (b, 0, i, 0)),
        out_shape=jax.ShapeDtypeStruct((bn, N_HEADS, t, MLA_QK_PAD), BF16),
        compiler_params=_cparams(("parallel", "parallel")),
        name="mla_q",
    )(z, g, w, qg, *tabs)


def _attn_body(q_ref, kc_ref, vc_ref, kx_ref, vx_ref, o_ref, m_sc, l_sc, acc_sc):
    j, nj = pl.program_id(3), pl.num_programs(3)

    def update(k, v):
        s = lax.dot_general(q_ref[0, 0], k, (((1,), (1,)), ((), ())), preferred_element_type=F32)
        m_prev = m_sc[...]
        m_new = jnp.maximum(m_prev, jnp.max(s, axis=1, keepdims=True))
        alpha = jnp.exp2(m_prev - m_new)
        p = jnp.exp2(s - m_new)
        l_sc[...] = alpha * l_sc[...] + jnp.sum(p, axis=1, keepdims=True)
        acc_sc[...] = alpha * acc_sc[...] + jnp.dot(p.astype(BF16), v, preferred_element_type=F32)
        m_sc[...] = m_new

    @pl.when(j == 0)
    def _():
        m_sc[...] = jnp.full_like(m_sc, -jnp.inf)
        l_sc[...] = jnp.zeros_like(l_sc)
        acc_sc[...] = jnp.zeros_like(acc_sc)
        update(kc_ref[0, 0], vc_ref[0, 0])

    @pl.when(j > 0)
    def _():
        update(kx_ref[0, 0], vx_ref[0, 0])

    @pl.when(j == nj - 1)
    def _():
        o_ref[0] = acc_sc[...] / l_sc[...]


def attn_call(q, kc, vc, kx, vx):
    bn, nh, t, dq = q.shape
    tc = kc.shape[2]
    tq, tk = 1024, 1024
    nk = t // tk
    xblk = lambda j: jnp.maximum(j - 1, 0)
    return pl.pallas_call(
        _attn_body,
        grid=(bn, nh, t // tq, nk + 1),
        in_specs=[pl.BlockSpec((1, 1, tq, dq), lambda b, h, i, j: (b, h, i, 0)),
                  pl.BlockSpec((1, 1, tc, dq), lambda b, h, i, j: (b, h, 0, 0)),
                  pl.BlockSpec((1, 1, tc, DH), lambda b, h, i, j: (b, h, 0, 0)),
                  pl.BlockSpec((1, 1, tk, dq), lambda b, h, i, j: (b, h, xblk(j), 0)),
                  pl.BlockSpec((1, 1, tk, DH), lambda b, h, i, j: (b, h, xblk(j), 0))],
        out_specs=pl.BlockSpec((1, tq, DH), lambda b, h, i, j: (b, i, h)),
        out_shape=jax.ShapeDtypeStruct((bn, t, nh * DH), F32),
        scratch_shapes=[pltpu.VMEM((tq, 1), F32), pltpu.VMEM((tq, 1), F32), pltpu.VMEM((tq, DH), F32)],
        compiler_params=_cparams(("parallel", "parallel", "parallel", "arbitrary")),
        name="mla_attention",
    )(q, kc, vc, kx, vx)


def rope_tables(seq):
    rows = seq // GRID_W
    row = jnp.repeat(jnp.arange(rows), GRID_W).astype(F32)
    col = jnp.tile(jnp.arange(GRID_W), rows).astype(F32)
    n_freq = MLA_ROPE // 4
    inv = ROPE_THETA ** (-jnp.arange(n_freq, dtype=F32) / n_freq)
    ang = jnp.concatenate([row[:, None] * inv, col[:, None] * inv], axis=-1)
    cos, sin = jnp.cos(ang), jnp.sin(ang)
    half = MLA_ROPE // 2
    ones = jnp.ones((seq, MLA_NOPE), F32)
    zn = jnp.zeros((seq, MLA_NOPE), F32)
    zh = jnp.zeros((seq, half), F32)
    zp = jnp.zeros((seq, MLA_QK_PAD - MLA_QK), F32)
    cos_t = jnp.concatenate([ones, cos, cos, zp], axis=-1)
    sin_a = jnp.concatenate([zn, -sin, zh, zp], axis=-1)
    sin_b = jnp.concatenate([zn, zh, sin, zp], axis=-1)
    return cos_t, sin_a, sin_b


def _padcols(w, n):
    return jnp.concatenate([w, jnp.zeros((w.shape[0], n - w.shape[1]), w.dtype)], axis=1)


def pack_even(w):
    q, k, v, o = (w[:, j * MIX_W:(j + 1) * MIX_W] for j in range(4))
    gates = w[:, 4 * MIX_W:4 * MIX_W + 16]
    hy = w[:, 4 * MIX_W + 16:]
    return jnp.concatenate([q, k, v, hy, o, _padcols(gates, DH)], axis=1).astype(BF16)


def pack_odd(w):
    q, k, v, gg = (w[:, j * MIX_W:(j + 1) * MIX_W] for j in range(4))
    c0 = 4 * MIX_W
    gates = w[:, c0:c0 + 16]
    lq = w[:, c0 + 16:c0 + 16 + MLA_Q_RANK]
    lkv = w[:, c0 + 16 + MLA_Q_RANK:c0 + 16 + MLA_Q_RANK + MLA_KV_RANK]
    lkr = w[:, c0 + 16 + MLA_Q_RANK + MLA_KV_RANK:]
    return jnp.concatenate([q, k, v, gg, lkv, lq, _padcols(lkr, DH), _padcols(gates, DH)], axis=1).astype(BF16)


def kernel(x, c, ctx, c_ctx, mod_w, mod_b, a_w_in, a_i_bias, a_f_bias, a_norm_g, b_conv_w, b_conv_b, b_f_w1, b_f_b1, b_f_w2, b_f_b2, b_f_w3, b_f_b3, b_f_w4, b_f_freq, b_d, ab_w_out, cd_w_in, c_conv_w, c_conv_b, c_a_log, c_dt_bias, c_norm_g, d_q_norm_g, d_w_q_up, d_kv_norm_g, d_w_kv_up, d_qn_g, d_kn_g, cd_w_out, ffn_w_up, ffn_conv_w, ffn_conv_b, ffn_w_down):
    bsz, seq, d = x.shape
    ctx_len = ctx.shape[1]
    cvec = jnp.concatenate([c, c_ctx[None], jnp.zeros((8 - bsz - 1, d), F32)], axis=0)
    mods = mods_call(cvec, mod_w, mod_b)
    mx = lambda l, j: mods[l, :bsz, None, j * d:(j + 1) * d]
    mc = lambda l, j: jnp.broadcast_to(mods[l, bsz, j * d:(j + 1) * d], (bsz, 1, d))

    def ffn(l, h, m):
        return ffn_call(h, m(l, 3), m(l, 4), m(l, 5),
                        *ffn_weights(ffn_w_up[l], ffn_conv_w[l], ffn_conv_b[l], ffn_w_down[l]))

    w_in = pack_even(a_w_in[0])
    zx = inproj_call(x, mx(0, 0), mx(0, 1), w_in)
    zc = inproj_call(ctx, mc(0, 0), mc(0, 1), w_in)
    bias = jnp.concatenate([a_i_bias[0].reshape(2, N_HEADS), a_f_bias[0].reshape(2, N_HEADS)], axis=-1)[:, None, :]
    zero_state = (jnp.zeros((bsz, N_HEADS, DH, DH), F32), jnp.zeros((bsz, N_HEADS, 1, DH), F32),
                  jnp.zeros((bsz, N_HEADS, 1, DH), F32))
    hc, hx = [], []
    for dd in range(2):
        h_ctx, st = mlstm_call(dd, zc, bias, zero_state)
        hc.append(h_ctx)
        hx.append(mlstm_call(dd, zx, bias, st)[0])
    filt = (b_f_w1[0], b_f_b1[0], b_f_w2[0], b_f_b2[0], b_f_w3[0], b_f_b3[0], b_f_w4[0], b_f_freq[0])
    ux = dwconv_call(zx, E_HY // (3 * HYENA_W), b_conv_w[0], b_conv_b[0], False)
    uc = dwconv_call(zc, E_HY // (3 * HYENA_W), b_conv_w[0], b_conv_b[0], False)
    yx = hyena_long(ux, *hy_filter_call(seq, *filt), b_d[0])
    yc = hyena_small(uc, *hy_filter_call(ctx_len, *filt), b_d[0])
    w_out = ab_w_out[0].astype(BF16)
    g = a_norm_g[0].reshape(1, MIX_W)
    x = outproj_call("sigmoid", hx[0], hx[1], zx, E_MO // MIX_W, yx, g, w_out, x, mx(0, 2))
    ctx = outproj_call("sigmoid", hc[0], hc[1], zc, E_MO // MIX_W, yc, g, w_out, ctx, mc(0, 2))
    x = ffn(0, x, mx)
    ctx = ffn(0, ctx, mc)

    w_in = pack_odd(cd_w_in[0])
    zx = inproj_call(x, mx(1, 0), mx(1, 1), w_in)
    zc = inproj_call(ctx, mc(1, 0), mc(1, 1), w_in)
    qkv_x = dwconv_call(zx, 0, c_conv_w[0], c_conv_b[0], True)
    qkv_c = dwconv_call(zc, 0, c_conv_w[0], c_conv_b[0], True)
    zeros4 = jnp.zeros((2, 1, N_HEADS), F32)
    dtb = jnp.concatenate([c_dt_bias[0][:, None, :], zeros4], axis=-1)
    alog = jnp.concatenate([c_a_log[0][:, None, :], zeros4], axis=-1)
    ox = []
    for dd in range(2):
        _, s_ctx = gdn_call(dd, qkv_c, zc[..., O_GATE:O_GATE + 16], dtb, alog, zero_state[0])
        ox.append(gdn_call(dd, qkv_x, zx[..., O_GATE:O_GATE + 16], dtb, alog, s_ctx)[0])
    tabs = rope_tables(seq)
    kvg = d_kv_norm_g[0].reshape(1, -1)
    w_kv = d_w_kv_up[0].astype(BF16)
    kn_g = _padcols(d_kn_g[0].reshape(1, -1), MLA_QK_PAD)
    kc, vc = kv_call(zc, False, kvg, w_kv, kn_g, tuple(t[:ctx_len] for t in tabs))
    kx, vx = kv_call(zx, True, kvg, w_kv, kn_g, tabs)
    w_q = jnp.concatenate([_padcols(d_w_q_up[0][:, h * MLA_QK:(h + 1) * MLA_QK], MLA_QK_PAD)
                           for h in range(N_HEADS)], axis=1).astype(BF16)
    qx = q_call(zx, d_q_norm_g[0].reshape(1, -1), w_q, _padcols(d_qn_g[0].reshape(1, -1), MLA_QK_PAD), tabs)
    ax = attn_call(qx, kc, vc, kx, vx)
    g = jnp.tile(c_norm_g[0], N_HEADS).reshape(1, MIX_W)
    x = outproj_call("silu", ox[0], ox[1], zx, O_GG // MIX_W, ax, g, cd_w_out[0].astype(BF16), x, mx(1, 2))
    x = ffn(1, x, mx)
    return x
```

```python
import functools
import math

import numpy as np
import jax
import jax.numpy as jnp
from jax import lax
from jax.experimental import pallas as pl
from jax.experimental.pallas import tpu as pltpu

F32 = jnp.float32
BF16 = jnp.bfloat16
HI = lax.Precision.HIGHEST
LOG2E = math.log2(math.e)

D_MODEL = 1024
GRID_W = 64
EPS = 1e-6
CHUNK = 64
ROPE_THETA = 10000.0
N_HEADS = 4
DH = 128
MIX_W = N_HEADS * DH
HYENA_W = 512
HYENA_EMB = 33
HYENA_BANDS = 16
HYENA_FFN = 64
HYENA_NFILT = 4 * HYENA_W
HYENA_FAST_DECAY = 0.3
HYENA_SLOW_DECAY = 1.5
HYENA_TARGET = 1e-2
MLA_NOPE = 128
MLA_ROPE = 64
MLA_QK = 192
MLA_QK_PAD = 256
MLA_Q_RANK = 384
MLA_KV_RANK = 256
D_FF = 2816
FF_TILE = 256
N_FF_TILES = D_FF // FF_TILE
FFT_N1 = 128
SCAN_TB = 256
ATTN_TQ = 512
ATTN_TK = 2048
VMEM_LIMIT = 56 * 1024 * 1024

E_QKV, E_HY, E_MO, E_GATE, E_N = 0, 1536, 3072, 3584, 3712
O_QKV, O_GG, O_LKV, O_LQ, O_LKR, O_GATE, O_N = 0, 1536, 2048, 2304, 2688, 2816, 2944


def _cparams(sem):
    return pltpu.CompilerParams(dimension_semantics=sem, vmem_limit_bytes=VMEM_LIMIT)


def _dot(a, b):
    return jnp.dot(a.astype(BF16), b.astype(BF16), preferred_element_type=F32)


def _dot_nt(a, b):
    return lax.dot_general(a.astype(BF16), b.astype(BF16), (((1,), (1,)), ((), ())), preferred_element_type=F32)


def _dot_tn(a, b):
    return lax.dot_general(a.astype(BF16), b.astype(BF16), (((0,), (0,)), ((), ())), preferred_element_type=F32)


def _dot_hi(a, b):
    return jnp.dot(a, b, precision=HI, preferred_element_type=F32)


def _rms(x, n=None):
    n = x.shape[-1] if n is None else n
    return x * lax.rsqrt(jnp.sum(x * x, axis=-1, keepdims=True) * (1.0 / n) + EPS)


def _sigmoid(x):
    return 1.0 / (1.0 + jnp.exp(-x))


def _silu(x):
    return x * _sigmoid(x)


def _softplus(x):
    return jnp.maximum(x, 0.0) + jnp.log(1.0 + jnp.exp(-jnp.abs(x)))


def _log_sigmoid(x):
    return -_softplus(-x)


def _mods_body(c_ref, w_ref, b_ref, o_ref):
    o_ref[0] = _dot_hi(_silu(c_ref[...]), w_ref[0]) + b_ref[0]


def mods_call(cvec, mod_w, mod_b):
    depth, d, n = mod_w.shape
    tn = 1536
    return pl.pallas_call(
        _mods_body,
        grid=(depth, n // tn),
        in_specs=[pl.BlockSpec((8, d), lambda l, j: (0, 0)),
                  pl.BlockSpec((1, d, tn), lambda l, j: (l, 0, j)),
                  pl.BlockSpec((1, 1, tn), lambda l, j: (l, 0, j))],
        out_specs=pl.BlockSpec((1, 8, tn), lambda l, j: (l, 0, j)),
        out_shape=jax.ShapeDtypeStruct((depth, 8, n), F32),
        compiler_params=_cparams(("parallel", "parallel")),
        name="mods",
    )(cvec, mod_w, mod_b.reshape(depth, 1, n))


def _inproj_body(x_ref, sh_ref, sc_ref, w_ref, o_ref):
    h = _rms(x_ref[0]) * (1.0 + sc_ref[0]) + sh_ref[0]
    o_ref[0] = _dot(h, w_ref[...])


def inproj_call(x, shift, scale, w):
    bn, t, d = x.shape
    n = w.shape[1]
    tm = min(512, t)
    return pl.pallas_call(
        _inproj_body,
        grid=(bn, t // tm),
        in_specs=[pl.BlockSpec((1, tm, d), lambda b, i: (b, i, 0)),
                  pl.BlockSpec((1, 1, d), lambda b, i: (b, 0, 0)),
                  pl.BlockSpec((1, 1, d), lambda b, i: (b, 0, 0)),
                  pl.BlockSpec((d, n), lambda b, i: (0, 0))],
        out_specs=pl.BlockSpec((1, tm, n), lambda b, i: (b, i, 0)),
        out_shape=jax.ShapeDtypeStruct((bn, t, n), F32),
        compiler_params=_cparams(("parallel", "parallel")),
        name="inproj",
    )(x, shift, scale, w)


def _shift_rows(u, prev_row, next_row):
    tm = u.shape[0]
    rows = lax.broadcasted_iota(jnp.int32, (tm, 1), 0)
    um = jnp.where(rows == 0, prev_row, pltpu.roll(u, 1, 0))
    up = jnp.where(rows == tm - 1, next_row, pltpu.roll(u, tm - 1, 0))
    return um, up


def _dwconv_body(gdn, z_ref, zp_ref, zn_ref, w_ref, b_ref, o_ref):
    i, nt = pl.program_id(1), pl.num_programs(1)
    u = z_ref[0]
    prev_row = zp_ref[0][7:8, :] * (i > 0).astype(F32)
    next_row = zn_ref[0][0:1, :] * (i < nt - 1).astype(F32)
    um, up = _shift_rows(u, prev_row, next_row)
    y = um * w_ref[0:1, :] + u * w_ref[1:2, :] + up * w_ref[2:3, :] + b_ref[...]
    if not gdn:
        o_ref[0] = y
        return
    y = _silu(y)
    for h in range(2 * N_HEADS):
        s = y[:, h * DH:(h + 1) * DH]
        s = s * lax.rsqrt(jnp.sum(s * s, axis=-1, keepdims=True) + EPS)
        if h < N_HEADS:
            s = s * DH ** -0.5
        o_ref[0, :, h * DH:(h + 1) * DH] = s
    o_ref[0, :, 2 * MIX_W:] = y[:, 2 * MIX_W:]


def dwconv_call(z, col_block, w, b, gdn):
    bn, t, _ = z.shape
    c = w.shape[1]
    tm = min(512, t)
    r8 = tm // 8
    return pl.pallas_call(
        functools.partial(_dwconv_body, gdn),
        grid=(bn, t // tm),
        in_specs=[pl.BlockSpec((1, tm, c), lambda b_, i: (b_, i, col_block)),
                  pl.BlockSpec((1, 8, c), lambda b_, i: (b_, jnp.maximum(i * r8 - 1, 0), col_block)),
                  pl.BlockSpec((1, 8, c), lambda b_, i: (b_, jnp.minimum((i + 1) * r8, t // 8 - 1), col_block)),
                  pl.BlockSpec((3, c), lambda b_, i: (0, 0)),
                  pl.BlockSpec((1, c), lambda b_, i: (0, 0))],
        out_specs=pl.BlockSpec((1, tm, c), lambda b_, i: (b_, i, 0)),
        out_shape=jax.ShapeDtypeStruct((bn, t, c), F32),
        compiler_params=_cparams(("parallel", "parallel")),
        name="dwconv_gdn" if gdn else "dwconv",
    )(z, z, z, w, b.reshape(1, c))


def _block_masks(d, tb):
    sh = CHUNK.bit_length() - 1
    ii = lax.broadcasted_iota(jnp.int32, (tb, tb), 0)
    jj = lax.broadcasted_iota(jnp.int32, (tb, tb), 1)
    same = (ii >> sh) == (jj >> sh)
    vis = jnp.logical_and(same, (ii >= jj) if d == 0 else (ii <= jj))
    vis_t = jnp.logical_and(same, (jj >= ii) if d == 0 else (jj <= ii))
    return ii, jj, same, vis, vis_t


def _chunk_order(d, nc):
    return range(nc) if d == 0 else reversed(range(nc))


def _lockstep(gens):
    gens = list(gens)
    results = [None] * len(gens)
    live = list(range(len(gens)))
    while live:
        for i in list(live):
            try:
                next(gens[i])
            except StopIteration as stop:
                results[i] = stop.value
                live.remove(i)
    return results


def _mlstm_block(d, q, k, v, li_c, lf_c, li_r, lf_r, c_mat, n_vec, m):
    tb = q.shape[0]
    ii = lax.broadcasted_iota(jnp.int32, (tb, tb), 0)
    jj = lax.broadcasted_iota(jnp.int32, (tb, tb), 1)
    vis = (ii >= jj) if d == 0 else (ii <= jj)
    vis_t = (jj >= ii) if d == 0 else (jj <= ii)
    b_c = jnp.sum(vis.astype(F32) * lf_r, axis=1, keepdims=True)
    b_r = jnp.sum(vis_t.astype(F32) * lf_c, axis=0, keepdims=True)
    b_last = jnp.sum(lf_r, axis=1, keepdims=True)
    a_c = b_last - b_c + li_c
    m_loc = jnp.max(a_c, axis=0, keepdims=True)
    dmat = jnp.where(vis, b_c - b_r + li_r, -jnp.inf)
    m_in = jnp.max(dmat, axis=1, keepdims=True)
    yield
    s = _dot_nt(q, k) * jnp.exp(dmat - m_in)
    q_c = _dot_nt(q, c_mat)
    yield
    num_in = _dot(s, v)
    den_in = jnp.sum(s, axis=1, keepdims=True)
    m_inter = b_c + m
    m_t = jnp.maximum(m_inter, m_in)
    s_inter, s_in = jnp.exp(m_inter - m_t), jnp.exp(m_in - m_t)
    yield
    num = s_inter * q_c + s_in * num_in
    den = s_inter * jnp.sum(q * n_vec, axis=1, keepdims=True) + s_in * den_in
    h = num / jnp.maximum(jnp.abs(den), jnp.exp(-m_t))
    m_new = jnp.maximum(b_last + m, m_loc)
    f_old, f_loc = jnp.exp(b_last + m - m_new), jnp.exp(m_loc - m_new)
    kw = k * jnp.exp(a_c - m_loc)
    yield
    c_mat = f_old * c_mat + f_loc * _dot_tn(v, kw)
    n_vec = f_old * n_vec + f_loc * jnp.sum(kw, axis=0, keepdims=True)
    return h, c_mat, n_vec, m_new


def _mlstm_body(d, q_ref, k_ref, v_ref, gc_ref, gr_ref, bias_ref, c0_ref, n0_ref, m0_ref,
                h_ref, c_ref, n_ref, m_ref):
    @pl.when(pl.program_id(1) == 0)
    def _():
        c_ref[...] = c0_ref[...]
        n_ref[...] = n0_ref[...]
        m_ref[...] = m0_ref[...]

    gc = gc_ref[0, 0]
    gr = gr_ref[0, 0]

    def head(h):
        hs = slice(h * DH, (h + 1) * DH)
        bi = bias_ref[0, :, h:h + 1]
        bf = bias_ref[0, :, N_HEADS + h:N_HEADS + h + 1]
        li_c = gc[:, h:h + 1] + bi
        lf_c = _log_sigmoid(gc[:, N_HEADS + h:N_HEADS + h + 1] + bf)
        li_r = gr[h:h + 1, :] + bi
        lf_r = _log_sigmoid(gr[N_HEADS + h:N_HEADS + h + 1, :] + bf)
        return _mlstm_block(d, q_ref[0, :, hs], k_ref[0, :, hs] * DH ** -0.5, v_ref[0, :, hs],
                            li_c, lf_c, li_r, lf_r, c_ref[0, h], n_ref[0, h], m_ref[0, h][:, 0:1])

    for h, (hv, c_new, n_new, m_new) in enumerate(_lockstep(head(h) for h in range(N_HEADS))):
        h_ref[0, :, h * DH:(h + 1) * DH] = hv
        c_ref[0, h] = c_new
        n_ref[0, h] = n_new
        m_ref[0, h] = jnp.broadcast_to(m_new, (1, DH))


def _gdn_block(d, q, k, v, bt_c, la_c, la_r, s_mat):
    tb = q.shape[0]
    nc = tb // CHUNK
    sh = CHUNK.bit_length() - 1
    ii, jj, same, vis, vis_t = _block_masks(d, tb)
    g_c = jnp.sum(vis.astype(F32) * la_r, axis=1, keepdims=True)
    g_r = jnp.sum(vis_t.astype(F32) * la_c, axis=0, keepdims=True)
    gl_c = jnp.sum(same.astype(F32) * la_r, axis=1, keepdims=True)
    decay = jnp.exp(jnp.where(vis, g_c - g_r, -jnp.inf))
    kb = k * bt_c
    yield
    a_mat = jnp.where(ii == jj, 0.0, _dot_nt(kb, k) * decay)
    attn = _dot_nt(q, k) * decay
    yield
    off = lambda lvl: jnp.logical_and((ii >> (lvl + 1)) == (jj >> (lvl + 1)), (ii >> lvl) != (jj >> lvl))
    t_mat = (ii == jj).astype(F32) - jnp.where(off(0), a_mat, 0.0)
    for lvl in range(1, sh):
        te = _dot(t_mat, jnp.where(off(lvl), a_mat, 0.0))
        yield
        t_mat = t_mat - _dot(te, t_mat)
        yield
    eg = jnp.exp(g_c)
    u = _dot(t_mat, v * bt_c)
    w = _dot(t_mat, kb * eg)
    kd = k * jnp.exp(gl_c - g_c)
    qg = q * eg
    yield
    v_new, o_state = [None] * nc, [None] * nc
    for c in _chunk_order(d, nc):
        sl = slice(c * CHUNK, (c + 1) * CHUNK)
        v_new[c] = u[sl] - _dot(w[sl], s_mat)
        o_state[c] = _dot(qg[sl], s_mat)
        yield
        s_mat = s_mat * jnp.exp(gl_c[c * CHUNK:c * CHUNK + 1, :]) + _dot_tn(kd[sl], v_new[c])
        yield
    o = jnp.concatenate(o_state, axis=0) + _dot(attn, jnp.concatenate(v_new, axis=0))
    return o, s_mat


def _gdn_body(d, q_ref, k_ref, v_ref, gc_ref, gr_ref, dtb_ref, alog_ref, s0_ref, o_ref, s_ref):
    @pl.when(pl.program_id(1) == 0)
    def _():
        s_ref[...] = s0_ref[...]

    gc = gc_ref[0, 0]
    gr = gr_ref[0, 0]

    def head(h):
        hs = slice(h * DH, (h + 1) * DH)
        dtb = dtb_ref[0, :, h:h + 1]
        neg_a = -jnp.exp(alog_ref[0, :, h:h + 1])
        bt_c = _sigmoid(gc[:, h:h + 1])
        la_c = neg_a * _softplus(gc[:, N_HEADS + h:N_HEADS + h + 1] + dtb)
        la_r = neg_a * _softplus(gr[N_HEADS + h:N_HEADS + h + 1, :] + dtb)
        return _gdn_block(d, q_ref[0, :, hs], k_ref[0, :, hs], v_ref[0, :, hs], bt_c, la_c, la_r, s_ref[0, h])

    for h, (ov, s_new) in enumerate(_lockstep(head(h) for h in range(N_HEADS))):
        o_ref[0, :, h * DH:(h + 1) * DH] = ov
        s_ref[0, h] = s_new


def _gate_views(graw):
    bsz, t, _ = graw.shape
    gcol = graw.reshape(bsz, t, 2, 2, N_HEADS).transpose(0, 3, 1, 2, 4).reshape(bsz, 2, t, 8)
    return gcol, gcol.transpose(0, 1, 3, 2)


def _scan_specs(d, t, col0):
    tb = min(SCAN_TB, t)
    nb = t // tb
    blk = (lambda i: i) if d == 0 else (lambda i: nb - 1 - i)
    qkv = [pl.BlockSpec((1, tb, MIX_W), lambda b, i, j=j: (b, blk(i), col0 + j)) for j in range(3)]
    gates = [pl.BlockSpec((1, 1, tb, 8), lambda b, i: (b, d, blk(i), 0)),
             pl.BlockSpec((1, 1, 8, tb), lambda b, i: (b, d, 0, blk(i)))]
    par = pl.BlockSpec((1, 1, 8), lambda b, i: (d, 0, 0))
    out = pl.BlockSpec((1, tb, MIX_W), lambda b, i: (b, blk(i), 0))
    return nb, qkv, gates, par, out


def _state_spec(*tail):
    return pl.BlockSpec((1, N_HEADS) + tail, lambda b, i: (b, 0) + (0,) * len(tail))


def mlstm_call(d, z, bias, state):
    bsz, t, _ = z.shape
    nb, qkv, gates, par, out = _scan_specs(d, t, E_QKV // MIX_W)
    gcol, grow = _gate_views(z[..., E_GATE:E_GATE + 16])
    st_specs = [_state_spec(DH, DH), _state_spec(1, DH), _state_spec(1, DH)]
    st_shapes = [jax.ShapeDtypeStruct((bsz, N_HEADS, DH, DH), F32),
                 jax.ShapeDtypeStruct((bsz, N_HEADS, 1, DH), F32),
                 jax.ShapeDtypeStruct((bsz, N_HEADS, 1, DH), F32)]
    res = pl.pallas_call(
        functools.partial(_mlstm_body, d),
        grid=(bsz, nb),
        in_specs=qkv + gates + [par] + st_specs,
        out_specs=[out] + st_specs,
        out_shape=[jax.ShapeDtypeStruct((bsz, t, MIX_W), F32)] + st_shapes,
        compiler_params=_cparams(("parallel", "arbitrary")),
        name="mlstm_scan_fwd" if d == 0 else "mlstm_scan_bwd",
    )(z, z, z, gcol, grow, bias, *state)
    return res[0], tuple(res[1:])


def gdn_call(d, qkv_arr, graw, dtb, alog, state):
    bsz, t, _ = qkv_arr.shape
    nb, qkv, gates, par, out = _scan_specs(d, t, 0)
    gcol, grow = _gate_views(graw)
    st_spec = _state_spec(DH, DH)
    return pl.pallas_call(
        functools.partial(_gdn_body, d),
        grid=(bsz, nb),
        in_specs=qkv + gates + [par, par, st_spec],
        out_specs=[out, st_spec],
        out_shape=[jax.ShapeDtypeStruct((bsz, t, MIX_W), F32),
                   jax.ShapeDtypeStruct((bsz, N_HEADS, DH, DH), F32)],
        compiler_params=_cparams(("parallel", "arbitrary")),
        name="gdn_scan_fwd" if d == 0 else "gdn_scan_bwd",
    )(qkv_arr, qkv_arr, qkv_arr, gcol, grow, dtb, alog, state)


def _outproj_body(act, hf_ref, hb_ref, gate_ref, y_ref, g_ref, w_ref, x_ref, mg_ref, o_ref):
    hs = hf_ref[0] + hb_ref[0]
    hn = jnp.concatenate([_rms(hs[:, h * DH:(h + 1) * DH]) for h in range(N_HEADS)], axis=-1) * g_ref[...]
    gate = gate_ref[0]
    hn = hn * (_sigmoid(gate) if act == "sigmoid" else _silu(gate))
    out = _dot(hn, w_ref[0:MIX_W, :]) + _dot(y_ref[0], w_ref[MIX_W:, :])
    o_ref[0] = x_ref[0] + mg_ref[0] * out


def outproj_call(act, hf, hb, z, gate_block, y, g, w, x, mgate):
    bn, t, d = x.shape
    tm = min(512, t)
    return pl.pallas_call(
        functools.partial(_outproj_body, act),
        grid=(bn, t // tm),
        in_specs=[pl.BlockSpec((1, tm, MIX_W), lambda b, i: (b, i, 0)),
                  pl.BlockSpec((1, tm, MIX_W), lambda b, i: (b, i, 0)),
                  pl.BlockSpec((1, tm, MIX_W), lambda b, i: (b, i, gate_block)),
                  pl.BlockSpec((1, tm, MIX_W), lambda b, i: (b, i, 0)),
                  pl.BlockSpec((1, MIX_W), lambda b, i: (0, 0)),
                  pl.BlockSpec((2 * MIX_W, d), lambda b, i: (0, 0)),
                  pl.BlockSpec((1, tm, d), lambda b, i: (b, i, 0)),
                  pl.BlockSpec((1, 1, d), lambda b, i: (b, 0, 0))],
        out_specs=pl.BlockSpec((1, tm, d), lambda b, i: (b, i, 0)),
        out_shape=jax.ShapeDtypeStruct((bn, t, d), F32),
        compiler_params=_cparams(("parallel", "parallel")),
        name="outproj_" + act,
    )(hf, hb, z, y, g, w, x, mgate)


def _ffn_body(x_ref, xp_ref, xn_ref, sh_ref, sc_ref, mg_ref, wa_ref, wb_ref, cwa_ref, cwb_ref, wd_ref, o_ref):
    i, nt = pl.program_id(1), pl.num_programs(1)
    sc, sh = 1.0 + sc_ref[0], sh_ref[0]
    x = x_ref[0]
    tm = x.shape[0]
    mod = lambda v: _rms(v) * sc + sh
    hp = mod(xp_ref[0]) * (i > 0).astype(F32)
    hn = mod(xn_ref[0]) * (i < nt - 1).astype(F32)
    h = jnp.concatenate([hp, mod(x), hn], axis=0).astype(BF16)
    rows = tm + 16

    up = lambda j: (jnp.dot(h, wa_ref[j], preferred_element_type=F32), jnp.dot(h, wb_ref[j], preferred_element_type=F32))

    def conv(u, cw):
        y = pltpu.roll(u, 1, 0) * cw[0:1, :] + u * cw[1:2, :] + pltpu.roll(u, rows - 1, 0) * cw[2:3, :] + cw[3:4, :]
        return y[8:8 + tm]

    acc = jnp.zeros(x.shape, F32)
    u_next = up(0)
    for j in range(N_FF_TILES):
        ua, ub = u_next
        if j + 1 < N_FF_TILES:
            u_next = up(j + 1)
        a, b = conv(ua, cwa_ref[j]), conv(ub, cwb_ref[j])
        gate = a * (0.5 + 0.5 * jnp.tanh(0.5 * a))
        acc = acc + _dot(gate * b, wd_ref[j])
    o_ref[0] = x + mg_ref[0] * acc


def ffn_call(x, shift, scale, mgate, wa, wb, cwa, cwb, wd):
    bn, t, d = x.shape
    tm = min(512, t)
    r8 = tm // 8
    full = lambda a: pl.BlockSpec(a.shape, lambda b, i: (0,) * a.ndim)
    vec = pl.BlockSpec((1, 1, d), lambda b, i: (b, 0, 0))
    return pl.pallas_call(
        _ffn_body,
        grid=(bn, t // tm),
        in_specs=[pl.BlockSpec((1, tm, d), lambda b, i: (b, i, 0)),
                  pl.BlockSpec((1, 8, d), lambda b, i: (b, jnp.maximum(i * r8 - 1, 0), 0)),
                  pl.BlockSpec((1, 8, d), lambda b, i: (b, jnp.minimum((i + 1) * r8, t // 8 - 1), 0)),
                  vec, vec, vec, full(wa), full(wb), full(cwa), full(cwb), full(wd)],
        out_specs=pl.BlockSpec((1, tm, d), lambda b, i: (b, i, 0)),
        out_shape=jax.ShapeDtypeStruct((bn, t, d), F32),
        compiler_params=_cparams(("parallel", "parallel")),
        name="conv_ffn",
    )(x, x, x, shift, scale, mgate, wa, wb, cwa, cwb, wd)


def ffn_weights(w_up, conv_w, conv_b, w_down):
    d = w_up.shape[0]
    tiles = lambda a: a.reshape(a.shape[0], N_FF_TILES, FF_TILE).transpose(1, 0, 2)
    cw = jnp.concatenate([conv_w, conv_b[None]], axis=0)
    return (tiles(w_up[:, :D_FF]).astype(BF16), tiles(w_up[:, D_FF:]).astype(BF16),
            tiles(cw[:, :D_FF]), tiles(cw[:, D_FF:]),
            w_down.reshape(N_FF_TILES, FF_TILE, d).astype(BF16))


def _hy_filter_body(lf, w1t_ref, w1c_ref, w1s_ref, b1_ref, w2_ref, b2_ref, w3_ref, b3_ref, w4_ref, fr_ref,
                    band_ref, dl_ref, o_ref, l1_ref):
    r = pl.program_id(0)
    tr = o_ref.shape[0]
    n = (r * tr + lax.broadcasted_iota(jnp.int32, (tr, 1), 0))
    t = jnp.where(n < lf, n, 2 * lf - n).astype(F32)
    t_norm = t / (lf - 1)
    ang = (2.0 * math.pi * t / lf) * band_ref[...]
    fr = fr_ref[...]
    pre = t_norm * w1t_ref[...] + _dot_hi(jnp.cos(ang), w1c_ref[...]) - _dot_hi(jnp.sin(ang), w1s_ref[...])
    hdn = jnp.sin(fr * (pre + b1_ref[...]))
    hdn = jnp.sin(fr * (_dot_hi(hdn, w2_ref[...]) + b2_ref[...]))
    hdn = jnp.sin(fr * (_dot_hi(hdn, w3_ref[...]) + b3_ref[...]))
    hval = _dot(hdn, w4_ref[0]) * jnp.exp(-t_norm * jnp.abs(dl_ref[0]))
    hval = jnp.where(n == lf, 0.0, hval)
    o_ref[...] = hval

    @pl.when(r == 0)
    def _():
        l1_ref[...] = jnp.zeros_like(l1_ref)

    l1_ref[...] += jnp.sum(jnp.abs(hval), axis=0, keepdims=True)


def hy_filter_call(lf, f_w1, f_b1, f_w2, f_b2, f_w3, f_b3, f_w4, f_freq):
    tr = min(512, lf)
    nt = 2 * lf // tr
    half = lf // tr
    ncol = 2 * HYENA_W
    pad = lambda a: jnp.concatenate([a, jnp.zeros((DH - a.shape[0], a.shape[1]), F32)], axis=0)
    w4 = f_w4.reshape(HYENA_FFN, 2, 2, HYENA_W).transpose(2, 0, 1, 3).reshape(2, HYENA_FFN, ncol)
    deltas = jnp.linspace(math.log(HYENA_TARGET) / HYENA_SLOW_DECAY, math.log(HYENA_TARGET) / HYENA_FAST_DECAY,
                          HYENA_NFILT, dtype=F32)
    deltas = deltas.reshape(2, 2, HYENA_W).transpose(1, 0, 2).reshape(2, 1, ncol)
    bands = jnp.linspace(1e-4, HYENA_BANDS - 1, HYENA_BANDS, dtype=F32)
    bands = jnp.concatenate([bands, jnp.zeros((DH - HYENA_BANDS,), F32)]).reshape(1, DH)
    row = lambda a: a.reshape(1, -1)
    args = (f_w1[0:1], pad(f_w1[1:1 + HYENA_BANDS]), pad(f_w1[1 + HYENA_BANDS:]), row(f_b1), f_w2, row(f_b2),
            f_w3, row(f_b3), w4, row(f_freq), bands, deltas)
    full = lambda a: pl.BlockSpec(a.shape, lambda r: (0,) * a.ndim)
    by_dir = lambda a: pl.BlockSpec((1,) + a.shape[1:], lambda r: (r // half, 0, 0))
    in_specs = [by_dir(a) if a.ndim == 3 else full(a) for a in args]
    return pl.pallas_call(
        functools.partial(_hy_filter_body, lf),
        grid=(nt,),
        in_specs=in_specs,
        out_specs=[pl.BlockSpec((tr, ncol), lambda r: (r, 0)), pl.BlockSpec((1, ncol), lambda r: (0, 0))],
        out_shape=[jax.ShapeDtypeStruct((2 * lf, ncol), F32), jax.ShapeDtypeStruct((1, ncol), F32)],
        compiler_params=_cparams(("arbitrary",)),
        name="hyena_filter",
    )(*args)


def _dft_consts(lf):
    n1 = FFT_N1
    n = 2 * lf
    assert n == n1 * n1
    idx = np.arange(n1)
    f = np.exp(-2j * np.pi * np.outer(idx, idx) / n1)
    tw = np.exp(-2j * np.pi * np.outer(idx, idx) / n)
    fh = f[:, :n1 // 2]
    blk = lambda a: np.block([[a.real, -a.imag], [a.imag, a.real]])
    f_sig = blk(fh)
    f_flt = np.concatenate([f.real, f.imag], axis=0)
    c_inv = blk(np.conj(f)[:n1 // 2, :] / n)
    c = lambda a: jnp.asarray(a, dtype=F32)
    cb = lambda a: jnp.asarray(a, dtype=F32).astype(BF16)
    return dict(f_sig=cb(f_sig), f_flt=cb(f_flt), c_inv=cb(c_inv), fr=c(f.real), fi=c(f.imag),
                twr=c(tw.real.reshape(n1, 1, n1)), twi=c(tw.imag.reshape(n1, 1, n1)))


def _lead_mm_body(m_ref, x_ref, o_ref):
    o_ref[...] = _dot(m_ref[...], x_ref[...]).astype(o_ref.dtype)


def lead_mm_call(mat, x2d, name):
    r, k = mat.shape
    nl = x2d.shape[1]
    tn = 4096
    return pl.pallas_call(
        _lead_mm_body,
        grid=(nl // tn,),
        in_specs=[pl.BlockSpec((r, k), lambda j: (0, 0)), pl.BlockSpec((k, tn), lambda j: (0, j))],
        out_specs=pl.BlockSpec((r, tn), lambda j: (0, j)),
        out_shape=jax.ShapeDtypeStruct((r, nl), BF16),
        compiler_params=_cparams(("parallel",)),
        name=name,
    )(mat, x2d)


def _stage2_mat(fr, fi, twr, twi):
    hr = fr * twr - fi * twi
    hi = fr * twi + fi * twr
    return hr, hi


def _blk(ar, ai):
    return jnp.concatenate([jnp.concatenate([ar, -ai], axis=1), jnp.concatenate([ai, ar], axis=1)], axis=0)


def _spec_filter_body(a_ref, l1_ref, fr_ref, fi_ref, twr_ref, twi_ref, o_ref):
    hr, hi = _stage2_mat(fr_ref[...], fi_ref[...], twr_ref[0], twi_ref[0])
    n1 = hr.shape[0]
    o_ref[0] = _dot(_blk(hr, hi), a_ref[...].reshape(2 * n1, a_ref.shape[-1])) * (1.0 / l1_ref[...])


def _spec_conv_body(a_ref, k_ref, fr_ref, fi_ref, twr_ref, twi_ref, o_ref):
    hr, hi = _stage2_mat(fr_ref[...], fi_ref[...], twr_ref[0], twi_ref[0])
    n1 = hr.shape[0]
    x = _dot(_blk(hr, hi), a_ref[...].reshape(2 * n1, a_ref.shape[-1]))
    xr, xi = x[:n1], x[n1:]
    kr, ki = k_ref[0, :n1], k_ref[0, n1:]
    y = jnp.concatenate([xr * kr - xi * ki, xr * ki + xi * kr], axis=0)
    o_ref[0] = _dot(_blk(hr.T, -(hi.T)), y).astype(o_ref.dtype)


def spec_call(a2d, kspec, order, cst, l1=None):
    n1 = FFT_N1
    c = a2d.shape[1] // n1
    ct = 512 if kspec is not None else min(1024, c)
    a4 = a2d.reshape(2, n1, n1, c)
    sq = pl.BlockSpec((n1, n1), lambda k, j: (0, 0))
    tw = pl.BlockSpec((1, 1, n1), lambda k, j: (k, 0, 0))
    a_spec = pl.BlockSpec((2, 1, n1, ct), lambda k, j: (0, k, 0, j))
    o_spec = pl.BlockSpec((1, 2 * n1, ct), lambda k, j: (k, 0, j))
    common = dict(grid=(n1, c // ct), out_specs=o_spec, compiler_params=_cparams(("parallel", "parallel")))
    if kspec is None:
        l1_spec = pl.BlockSpec((1, ct), lambda k, j: (0, j))
        return pl.pallas_call(_spec_filter_body, in_specs=[a_spec, l1_spec, sq, sq, tw, tw], name="hyena_spec_filter",
                              out_shape=jax.ShapeDtypeStruct((n1, 2 * n1, c), F32),
                              **common)(a4, l1, cst["fr"], cst["fi"], cst["twr"], cst["twi"])
    k_spec = pl.BlockSpec((1, 2 * n1, ct), lambda k, j: (k, 0, order * (c // ct) + j))
    return pl.pallas_call(_spec_conv_body, in_specs=[a_spec, k_spec, sq, sq, tw, tw], name="hyena_spec_conv",
                          out_shape=jax.ShapeDtypeStruct((n1, 2 * n1, c), BF16),
                          **common)(a4, kspec, cst["fr"], cst["fi"], cst["twr"], cst["twi"])


def _inv_gate_body(m_ref, br_ref, bi_ref, v_ref, x_ref, d_ref, o_ref):
    y = _dot(m_ref[...], jnp.concatenate([br_ref[...], bi_ref[...]], axis=0))
    o_ref[...] = x_ref[...] * (y + v_ref[...] * d_ref[...])


def inv_gate_call(b3, v2d, x2d, dvec, cst):
    n1 = FFT_N1
    nl = v2d.shape[1]
    tn = 2048
    nb = nl // tn
    b2d = b3.reshape(n1, 2 * nl)
    tile = pl.BlockSpec((n1, tn), lambda j: (0, j))
    return pl.pallas_call(
        _inv_gate_body,
        grid=(nb,),
        in_specs=[pl.BlockSpec((n1, 2 * n1), lambda j: (0, 0)), tile,
                  pl.BlockSpec((n1, tn), lambda j: (0, nb + j)), tile, tile,
                  pl.BlockSpec((1, tn), lambda j: (0, j))],
        out_specs=tile,
        out_shape=jax.ShapeDtypeStruct((n1, nl), F32),
        compiler_params=_cparams(("parallel",)),
        name="hyena_inv_gate",
    )(cst["c_inv"], b2d, b2d, v2d, x2d, dvec)


def hyena_long(u, kf, l1, d_skip):
    bsz, lf, _ = u.shape
    assert bsz == 2
    n1 = FFT_N1
    cst = _dft_consts(lf)
    rb = lambda a: a.reshape(bsz * n1 // 2, n1 * HYENA_W)
    v, x1, x2 = (rb(u[..., j * HYENA_W:(j + 1) * HYENA_W]) for j in range(3))
    kspec = spec_call(lead_mm_call(cst["f_flt"], kf.reshape(n1, n1 * 2 * HYENA_W), "hyena_lead_filter"),
                      None, 0, cst, l1)
    y = v
    for o, xg in enumerate((x1, x2)):
        a = lead_mm_call(cst["f_sig"], y, "hyena_lead_signal")
        b3 = spec_call(a, kspec, o, cst)
        y = inv_gate_call(b3, y, xg, jnp.tile(d_skip[o], n1).reshape(1, -1), cst)
    return y.reshape(bsz, lf, HYENA_W)


def _hyena_small_body(u_ref, kf_ref, l1_ref, fs_ref, ff_ref, ci_ref, d_ref, o_ref):
    w = HYENA_W
    nk = ff_ref.shape[0] // 2
    kspec = _dot(ff_ref[...], kf_ref[...]) * (1.0 / l1_ref[...])
    ucat = jnp.concatenate([u_ref[0], u_ref[1]], axis=0)
    y = ucat[:, :w]
    for o in range(2):
        x = _dot(fs_ref[...], y)
        xr, xi = x[:nk], x[nk:]
        kr, ki = kspec[:nk, o * w:(o + 1) * w], kspec[nk:, o * w:(o + 1) * w]
        prod = jnp.concatenate([xr * kr - xi * ki, xr * ki + xi * kr], axis=0)
        conv = _dot(ci_ref[...], prod)
        y = ucat[:, (o + 1) * w:(o + 2) * w] * (conv + y * d_ref[o:o + 1, :])
    lf = u_ref.shape[1]
    o_ref[0] = y[:lf]
    o_ref[1] = y[lf:]


def hyena_small(u, kf, l1, d_skip):
    bsz, lf, _ = u.shape
    assert bsz == 2
    n = 2 * lf
    idx = np.arange(n)
    f = np.exp(-2j * np.pi * np.outer(idx, idx) / n)
    blk = lambda a: np.block([[a.real, -a.imag], [a.imag, a.real]])
    cb = lambda a: jnp.asarray(a, F32).astype(BF16)
    fs = cb(blk(f[:, :lf]))
    ff = cb(np.concatenate([f.real, f.imag], axis=0))
    ci = cb(blk(np.conj(f)[:lf, :] / n))
    return pl.pallas_call(
        _hyena_small_body,
        out_shape=jax.ShapeDtypeStruct((bsz, lf, HYENA_W), F32),
        compiler_params=pltpu.CompilerParams(vmem_limit_bytes=VMEM_LIMIT),
        name="hyena_small",
    )(u, kf, l1, fs, ff, ci, d_skip)


def _rope(x, cos_ref, sa_ref, sb_ref):
    w = x.shape[-1]
    half = MLA_ROPE // 2
    return x * cos_ref[...] + pltpu.roll(x, w - half, 1) * sa_ref[...] + pltpu.roll(x, half, 1) * sb_ref[...]


def _kv_body(rope, lkv_ref, lkr_ref, g_ref, w_ref, kg_ref, cos_ref, sa_ref, sb_ref, k_ref, v_ref):
    kv = _dot(_rms(lkv_ref[0]) * g_ref[...], w_ref[...])
    kr = lkr_ref[0]
    for h in range(N_HEADS):
        base = h * (MLA_NOPE + DH)
        k = jnp.concatenate([kv[:, base:base + MLA_NOPE], kr], axis=-1)
        k = _rms(k, MLA_QK) * kg_ref[...]
        if rope:
            k = _rope(k, cos_ref, sa_ref, sb_ref)
        k_ref[0, h] = k.astype(BF16)
        v = kv[:, base + MLA_NOPE:base + MLA_NOPE + DH]
        v_ref[0, h] = jnp.concatenate([v, jnp.ones_like(v)], axis=-1).astype(BF16)


def kv_call(z, rope, g, w, kg, tabs):
    bn, t, _ = z.shape
    tm = min(512, t)
    tab = pl.BlockSpec((tm, MLA_QK_PAD), lambda b, i: (i, 0))
    return pl.pallas_call(
        functools.partial(_kv_body, rope),
        grid=(bn, t // tm),
        in_specs=[pl.BlockSpec((1, tm, MLA_KV_RANK), lambda b, i: (b, i, O_LKV // MLA_KV_RANK)),
                  pl.BlockSpec((1, tm, DH), lambda b, i: (b, i, O_LKR // DH)),
                  pl.BlockSpec((1, MLA_KV_RANK), lambda b, i: (0, 0)),
                  pl.BlockSpec(w.shape, lambda b, i: (0, 0)),
                  pl.BlockSpec((1, MLA_QK_PAD), lambda b, i: (0, 0)), tab, tab, tab],
        out_specs=[pl.BlockSpec((1, N_HEADS, tm, MLA_QK_PAD), lambda b, i: (b, 0, i, 0)),
                   pl.BlockSpec((1, N_HEADS, tm, 2 * DH), lambda b, i: (b, 0, i, 0))],
        out_shape=[jax.ShapeDtypeStruct((bn, N_HEADS, t, MLA_QK_PAD), BF16),
                   jax.ShapeDtypeStruct((bn, N_HEADS, t, 2 * DH), BF16)],
        compiler_params=_cparams(("parallel", "parallel")),
        name="mla_kv_rope" if rope else "mla_kv",
    )(z, z, g, w, kg, *tabs)


def _q_body(lq_ref, g_ref, w_ref, qg_ref, cos_ref, sa_ref, sb_ref, q_ref):
    q = _dot(_rms(lq_ref[0]) * g_ref[...], w_ref[...])
    for h in range(N_HEADS):
        qh = _rms(q[:, h * MLA_QK_PAD:(h + 1) * MLA_QK_PAD], MLA_QK) * qg_ref[...]
        qh = _rope(qh, cos_ref, sa_ref, sb_ref) * (MLA_QK ** -0.5 * LOG2E)
        q_ref[0, h] = qh.astype(BF16)


def q_call(z, g, w, qg, tabs):
    bn, t, _ = z.shape
    tm = min(512, t)
    tab = pl.BlockSpec((tm, MLA_QK_PAD), lambda b, i: (i, 0))
    return pl.pallas_call(
        _q_body,
        grid=(bn, t // tm),
        in_specs=[pl.BlockSpec((1, tm, MLA_Q_RANK), lambda b, i: (b, i, O_LQ // MLA_Q_RANK)),
                  pl.BlockSpec((1, MLA_Q_RANK), lambda b, i: (0, 0)),
                  pl.BlockSpec(w.shape, lambda b, i: (0, 0)),
                  pl.BlockSpec((1, MLA_QK_PAD), lambda b, i: (0, 0)), tab, tab, tab],
        out_specs=pl.BlockSpec((1, N_HEADS, tm, MLA_QK_PAD), lambda b, i: (b, 0, i, 0)),
        out_shape=jax.ShapeDtypeStruct((bn, N_HEADS, t, MLA_QK_PAD), BF16),
        compiler_params=_cparams(("parallel", "parallel")),
        name="mla_q",
    )(z, g, w, qg, *tabs)


def _attn_body(q_ref, kc_ref, vc_ref, kx_ref, vx_ref, o_ref):
    q = q_ref[0, 0]
    tiles = [(kc_ref, vc_ref, 0, kc_ref.shape[2])]
    tk = min(ATTN_TK, kx_ref.shape[2])
    tiles += [(kx_ref, vx_ref, s0, tk) for s0 in range(0, kx_ref.shape[2], tk)]
    scores = lambda t: lax.dot_general(q, t[0][0, 0, t[2]:t[2] + t[3], :], (((1,), (1,)), ((), ())),
                                       preferred_element_type=F32)
    m = jnp.full((q.shape[0], 1), -jnp.inf, F32)
    acc = jnp.zeros((q.shape[0], 2 * DH), F32)
    s_next = scores(tiles[0])
    for idx, t in enumerate(tiles):
        s = s_next
        if idx + 1 < len(tiles):
            s_next = scores(tiles[idx + 1])
        m_new = jnp.maximum(m, jnp.max(s, axis=1, keepdims=True))
        p = jnp.exp2(s - m_new).astype(BF16)
        acc = jnp.exp2(m - m_new) * acc + jnp.dot(p, t[1][0, 0, t[2]:t[2] + t[3], :], preferred_element_type=F32)
        m = m_new
    o_ref[0] = acc[:, :DH] / acc[:, DH:]


def attn_call(q, kc, vc, kx, vx):
    bn, nh, t, dq = q.shape
    tc = kc.shape[2]
    tq = ATTN_TQ
    whole = lambda n, w: pl.BlockSpec((1, 1, n, w), lambda b, h, i: (b, h, 0, 0))
    return pl.pallas_call(
        _attn_body,
        grid=(bn, nh, t // tq),
        in_specs=[pl.BlockSpec((1, 1, tq, dq), lambda b, h, i: (b, h, i, 0)),
                  whole(tc, dq), whole(tc, 2 * DH), whole(t, dq), whole(t, 2 * DH)],
        out_specs=pl.BlockSpec((1, tq, DH), lambda b, h, i: (b, i, h)),
        out_shape=jax.ShapeDtypeStruct((bn, t, nh * DH), F32),
        compiler_params=_cparams(("parallel", "parallel", "parallel")),
        name="mla_attention",
    )(q, kc, vc, kx, vx)


def rope_tables(seq):
    rows = seq // GRID_W
    row = jnp.repeat(jnp.arange(rows), GRID_W).astype(F32)
    col = jnp.tile(jnp.arange(GRID_W), rows).astype(F32)
    n_freq = MLA_ROPE // 4
    inv = ROPE_THETA ** (-jnp.arange(n_freq, dtype=F32) / n_freq)
    ang = jnp.concatenate([row[:, None] * inv, col[:, None] * inv], axis=-1)
    cos, sin = jnp.cos(ang), jnp.sin(ang)
    half = MLA_ROPE // 2
    ones = jnp.ones((seq, MLA_NOPE), F32)
    zn = jnp.zeros((seq, MLA_NOPE), F32)
    zh = jnp.zeros((seq, half), F32)
    zp = jnp.zeros((seq, MLA_QK_PAD - MLA_QK), F32)
    cos_t = jnp.concatenate([ones, cos, cos, zp], axis=-1)
    sin_a = jnp.concatenate([zn, -sin, zh, zp], axis=-1)
    sin_b = jnp.concatenate([zn, zh, sin, zp], axis=-1)
    return cos_t, sin_a, sin_b


def _padcols(w, n):
    return jnp.concatenate([w, jnp.zeros((w.shape[0], n - w.shape[1]), w.dtype)], axis=1)


def pack_even(w):
    q, k, v, o = (w[:, j * MIX_W:(j + 1) * MIX_W] for j in range(4))
    gates = w[:, 4 * MIX_W:4 * MIX_W + 16]
    hy = w[:, 4 * MIX_W + 16:]
    return jnp.concatenate([q, k, v, hy, o, _padcols(gates, DH)], axis=1).astype(BF16)


def pack_odd(w):
    q, k, v, gg = (w[:, j * MIX_W:(j + 1) * MIX_W] for j in range(4))
    c0 = 4 * MIX_W
    gates = w[:, c0:c0 + 16]
    lq = w[:, c0 + 16:c0 + 16 + MLA_Q_RANK]
    lkv = w[:, c0 + 16 + MLA_Q_RANK:c0 + 16 + MLA_Q_RANK + MLA_KV_RANK]
    lkr = w[:, c0 + 16 + MLA_Q_RANK + MLA_KV_RANK:]
    return jnp.concatenate([q, k, v, gg, lkv, lq, _padcols(lkr, DH), _padcols(gates, DH)], axis=1).astype(BF16)


def kernel(x, c, ctx, c_ctx, mod_w, mod_b, a_w_in, a_i_bias, a_f_bias, a_norm_g, b_conv_w, b_conv_b, b_f_w1, b_f_b1, b_f_w2, b_f_b2, b_f_w3, b_f_b3, b_f_w4, b_f_freq, b_d, ab_w_out, cd_w_in, c_conv_w, c_conv_b, c_a_log, c_dt_bias, c_norm_g, d_q_norm_g, d_w_q_up, d_kv_norm_g, d_w_kv_up, d_qn_g, d_kn_g, cd_w_out, ffn_w_up, ffn_conv_w, ffn_conv_b, ffn_w_down):
    bsz, seq, d = x.shape
    ctx_len = ctx.shape[1]
    cvec = jnp.concatenate([c, c_ctx[None], jnp.zeros((8 - bsz - 1, d), F32)], axis=0)
    mods = mods_call(cvec, mod_w, mod_b)
    mx = lambda l, j: mods[l, :bsz, None, j * d:(j + 1) * d]
    mc = lambda l, j: jnp.broadcast_to(mods[l, bsz, j * d:(j + 1) * d], (bsz, 1, d))

    def ffn(l, h, m):
        return ffn_call(h, m(l, 3), m(l, 4), m(l, 5),
                        *ffn_weights(ffn_w_up[l], ffn_conv_w[l], ffn_conv_b[l], ffn_w_down[l]))

    w_in = pack_even(a_w_in[0])
    zx = inproj_call(x, mx(0, 0), mx(0, 1), w_in)
    zc = inproj_call(ctx, mc(0, 0), mc(0, 1), w_in)
    bias = jnp.concatenate([a_i_bias[0].reshape(2, N_HEADS), a_f_bias[0].reshape(2, N_HEADS)], axis=-1)[:, None, :]
    zero_state = (jnp.zeros((bsz, N_HEADS, DH, DH), F32), jnp.zeros((bsz, N_HEADS, 1, DH), F32),
                  jnp.zeros((bsz, N_HEADS, 1, DH), F32))
    hc, hx = [], []
    for dd in range(2):
        h_ctx, st = mlstm_call(dd, zc, bias, zero_state)
        hc.append(h_ctx)
        hx.append(mlstm_call(dd, zx, bias, st)[0])
    filt = (b_f_w1[0], b_f_b1[0], b_f_w2[0], b_f_b2[0], b_f_w3[0], b_f_b3[0], b_f_w4[0], b_f_freq[0])
    ux = dwconv_call(zx, E_HY // (3 * HYENA_W), b_conv_w[0], b_conv_b[0], False)
    uc = dwconv_call(zc, E_HY // (3 * HYENA_W), b_conv_w[0], b_conv_b[0], False)
    yx = hyena_long(ux, *hy_filter_call(seq, *filt), b_d[0])
    yc = hyena_small(uc, *hy_filter_call(ctx_len, *filt), b_d[0])
    w_out = ab_w_out[0].astype(BF16)
    g = a_norm_g[0].reshape(1, MIX_W)
    x = outproj_call("sigmoid", hx[0], hx[1], zx, E_MO // MIX_W, yx, g, w_out, x, mx(0, 2))
    ctx = outproj_call("sigmoid", hc[0], hc[1], zc, E_MO // MIX_W, yc, g, w_out, ctx, mc(0, 2))
    x = ffn(0, x, mx)
    ctx = ffn(0, ctx, mc)

    w_in = pack_odd(cd_w_in[0])
    zx = inproj_call(x, mx(1, 0), mx(1, 1), w_in)
    zc = inproj_call(ctx, mc(1, 0), mc(1, 1), w_in)
    qkv_x = dwconv_call(zx, 0, c_conv_w[0], c_conv_b[0], True)
    qkv_c = dwconv_call(zc, 0, c_conv_w[0], c_conv_b[0], True)
    zeros4 = jnp.zeros((2, 1, N_HEADS), F32)
    dtb = jnp.concatenate([c_dt_bias[0][:, None, :], zeros4], axis=-1)
    alog = jnp.concatenate([c_a_log[0][:, None, :], zeros4], axis=-1)
    ox = []
    for dd in range(2):
        _, s_ctx = gdn_call(dd, qkv_c, zc[..., O_GATE:O_GATE + 16], dtb, alog, zero_state[0])
        ox.append(gdn_call(dd, qkv_x, zx[..., O_GATE:O_GATE + 16], dtb, alog, s_ctx)[0])
    tabs = rope_tables(seq)
    kvg = d_kv_norm_g[0].reshape(1, -1)
    w_kv = d_w_kv_up[0].astype(BF16)
    kn_g = _padcols(d_kn_g[0].reshape(1, -1), MLA_QK_PAD)
    kc, vc = kv_call(zc, False, kvg, w_kv, kn_g, tuple(t[:ctx_len] for t in tabs))
    kx, vx = kv_call(zx, True, kvg, w_kv, kn_g, tabs)
    w_q = jnp.concatenate([_padcols(d_w_q_up[0][:, h * MLA_QK:(h + 1) * MLA_QK], MLA_QK_PAD)
                           for h in range(N_HEADS)], axis=1).astype(BF16)
    qx = q_call(zx, d_q_norm_g[0].reshape(1, -1), w_q, _padcols(d_qn_g[0].reshape(1, -1), MLA_QK_PAD), tabs)
    ax = attn_call(qx, kc, vc, kx, vx)
    g = jnp.tile(c_norm_g[0], N_HEADS).reshape(1, MIX_W)
    x = outproj_call("silu", ox[0], ox[1], zx, O_GG // MIX_W, ax, g, cd_w_out[0].astype(BF16), x, mx(1, 2))
    x = ffn(1, x, mx)
    return x
```

```python
import functools
import math

import numpy as np
import jax
import jax.numpy as jnp
from jax import lax
from jax.experimental import pallas as pl
from jax.experimental.pallas import tpu as pltpu

F32 = jnp.float32
BF16 = jnp.bfloat16
HI = lax.Precision.HIGHEST
LOG2E = math.log2(math.e)

D_MODEL = 1024
GRID_W = 64
EPS = 1e-6
CHUNK = 64
ROPE_THETA = 10000.0
N_HEADS = 4
DH = 128
MIX_W = N_HEADS * DH
HYENA_W = 512
HYENA_EMB = 33
HYENA_BANDS = 16
HYENA_FFN = 64
HYENA_NFILT = 4 * HYENA_W
HYENA_FAST_DECAY = 0.3
HYENA_SLOW_DECAY = 1.5
HYENA_TARGET = 1e-2
MLA_NOPE = 128
MLA_ROPE = 64
MLA_QK = 192
MLA_QK_PAD = 256
MLA_Q_RANK = 384
MLA_KV_RANK = 256
D_FF = 2816
FF_TILE = 256
N_FF_TILES = D_FF // FF_TILE
FFT_N1 = 128
SCAN_TB = 256
ATTN_TQ = 512
ATTN_TK = 2048
VMEM_LIMIT = 56 * 1024 * 1024

E_QKV, E_HY, E_MO, E_GATE, E_N = 0, 1536, 3072, 3584, 3712
O_QKV, O_GG, O_LKV, O_LQ, O_LKR, O_GATE, O_N = 0, 1536, 2048, 2304, 2688, 2816, 2944


def _cparams(sem):
    return pltpu.CompilerParams(dimension_semantics=sem, vmem_limit_bytes=VMEM_LIMIT)


def _dot(a, b):
    return jnp.dot(a.astype(BF16), b.astype(BF16), preferred_element_type=F32)


def _dot_nt(a, b):
    return lax.dot_general(a.astype(BF16), b.astype(BF16), (((1,), (1,)), ((), ())), preferred_element_type=F32)


def _dot_tn(a, b):
    return lax.dot_general(a.astype(BF16), b.astype(BF16), (((0,), (0,)), ((), ())), preferred_element_type=F32)


def _dot_hi(a, b):
    return jnp.dot(a, b, precision=HI, preferred_element_type=F32)


def _rms(x, n=None):
    n = x.shape[-1] if n is None else n
    return x * lax.rsqrt(jnp.sum(x * x, axis=-1, keepdims=True) * (1.0 / n) + EPS)


def _sigmoid(x):
    return 1.0 / (1.0 + jnp.exp(-x))


def _silu(x):
    return x * _sigmoid(x)


def _softplus(x):
    return jnp.maximum(x, 0.0) + jnp.log(1.0 + jnp.exp(-jnp.abs(x)))


def _log_sigmoid(x):
    return -_softplus(-x)


def _mods_body(c_ref, w_ref, b_ref, o_ref):
    o_ref[0] = _dot_hi(_silu(c_ref[...]), w_ref[0]) + b_ref[0]


def mods_call(cvec, mod_w, mod_b):
    depth, d, n = mod_w.shape
    tn = 1536
    return pl.pallas_call(
        _mods_body,
        grid=(depth, n // tn),
        in_specs=[pl.BlockSpec((8, d), lambda l, j: (0, 0)),
                  pl.BlockSpec((1, d, tn), lambda l, j: (l, 0, j)),
                  pl.BlockSpec((1, 1, tn), lambda l, j: (l, 0, j))],
        out_specs=pl.BlockSpec((1, 8, tn), lambda l, j: (l, 0, j)),
        out_shape=jax.ShapeDtypeStruct((depth, 8, n), F32),
        compiler_params=_cparams(("parallel", "parallel")),
        name="mods",
    )(cvec, mod_w, mod_b.reshape(depth, 1, n))


def _inproj_body(x_ref, sh_ref, sc_ref, w_ref, o_ref):
    h = _rms(x_ref[0]) * (1.0 + sc_ref[0]) + sh_ref[0]
    o_ref[0] = _dot(h, w_ref[...])


def inproj_call(x, shift, scale, w):
    bn, t, d = x.shape
    n = w.shape[1]
    tm = min(512, t)
    return pl.pallas_call(
        _inproj_body,
        grid=(bn, t // tm),
        in_specs=[pl.BlockSpec((1, tm, d), lambda b, i: (b, i, 0)),
                  pl.BlockSpec((1, 1, d), lambda b, i: (b, 0, 0)),
                  pl.BlockSpec((1, 1, d), lambda b, i: (b, 0, 0)),
                  pl.BlockSpec((d, n), lambda b, i: (0, 0))],
        out_specs=pl.BlockSpec((1, tm, n), lambda b, i: (b, i, 0)),
        out_shape=jax.ShapeDtypeStruct((bn, t, n), F32),
        compiler_params=_cparams(("parallel", "parallel")),
        name="inproj",
    )(x, shift, scale, w)


def _shift_rows(u, prev_row, next_row):
    tm = u.shape[0]
    rows = lax.broadcasted_iota(jnp.int32, (tm, 1), 0)
    um = jnp.where(rows == 0, prev_row, pltpu.roll(u, 1, 0))
    up = jnp.where(rows == tm - 1, next_row, pltpu.roll(u, tm - 1, 0))
    return um, up


def _dwconv_body(gdn, z_ref, zp_ref, zn_ref, w_ref, b_ref, o_ref):
    i, nt = pl.program_id(1), pl.num_programs(1)
    u = z_ref[0]
    prev_row = zp_ref[0][7:8, :] * (i > 0).astype(F32)
    next_row = zn_ref[0][0:1, :] * (i < nt - 1).astype(F32)
    um, up = _shift_rows(u, prev_row, next_row)
    y = um * w_ref[0:1, :] + u * w_ref[1:2, :] + up * w_ref[2:3, :] + b_ref[...]
    if not gdn:
        o_ref[0] = y
        return
    y = _silu(y)
    for h in range(2 * N_HEADS):
        s = y[:, h * DH:(h + 1) * DH]
        s = s * lax.rsqrt(jnp.sum(s * s, axis=-1, keepdims=True) + EPS)
        if h < N_HEADS:
            s = s * DH ** -0.5
        o_ref[0, :, h * DH:(h + 1) * DH] = s
    o_ref[0, :, 2 * MIX_W:] = y[:, 2 * MIX_W:]


def dwconv_call(z, col_block, w, b, gdn):
    bn, t, _ = z.shape
    c = w.shape[1]
    tm = min(512, t)
    r8 = tm // 8
    return pl.pallas_call(
        functools.partial(_dwconv_body, gdn),
        grid=(bn, t // tm),
        in_specs=[pl.BlockSpec((1, tm, c), lambda b_, i: (b_, i, col_block)),
                  pl.BlockSpec((1, 8, c), lambda b_, i: (b_, jnp.maximum(i * r8 - 1, 0), col_block)),
                  pl.BlockSpec((1, 8, c), lambda b_, i: (b_, jnp.minimum((i + 1) * r8, t // 8 - 1), col_block)),
                  pl.BlockSpec((3, c), lambda b_, i: (0, 0)),
                  pl.BlockSpec((1, c), lambda b_, i: (0, 0))],
        out_specs=pl.BlockSpec((1, tm, c), lambda b_, i: (b_, i, 0)),
        out_shape=jax.ShapeDtypeStruct((bn, t, c), F32),
        compiler_params=_cparams(("parallel", "parallel")),
        name="dwconv_gdn" if gdn else "dwconv",
    )(z, z, z, w, b.reshape(1, c))


def _block_masks(d, tb):
    sh = CHUNK.bit_length() - 1
    ii = lax.broadcasted_iota(jnp.int32, (tb, tb), 0)
    jj = lax.broadcasted_iota(jnp.int32, (tb, tb), 1)
    same = (ii >> sh) == (jj >> sh)
    vis = jnp.logical_and(same, (ii >= jj) if d == 0 else (ii <= jj))
    vis_t = jnp.logical_and(same, (jj >= ii) if d == 0 else (jj <= ii))
    return ii, jj, same, vis, vis_t


def _chunk_order(d, nc):
    return range(nc) if d == 0 else reversed(range(nc))


def _lockstep(gens):
    gens = list(gens)
    results = [None] * len(gens)
    live = list(range(len(gens)))
    while live:
        for i in list(live):
            try:
                next(gens[i])
            except StopIteration as stop:
                results[i] = stop.value
                live.remove(i)
    return results


def _mlstm_block(d, q, k, v, li_c, lf_c, li_r, lf_r, c_mat, n_vec, m):
    tb = q.shape[0]
    ii = lax.broadcasted_iota(jnp.int32, (tb, tb), 0)
    jj = lax.broadcasted_iota(jnp.int32, (tb, tb), 1)
    vis = (ii >= jj) if d == 0 else (ii <= jj)
    vis_t = (jj >= ii) if d == 0 else (jj <= ii)
    b_c = jnp.sum(vis.astype(F32) * lf_r, axis=1, keepdims=True)
    b_r = jnp.sum(vis_t.astype(F32) * lf_c, axis=0, keepdims=True)
    b_last = jnp.sum(lf_r, axis=1, keepdims=True)
    a_c = b_last - b_c + li_c
    m_loc = jnp.max(a_c, axis=0, keepdims=True)
    dmat = jnp.where(vis, b_c - b_r + li_r, -jnp.inf)
    m_in = jnp.max(dmat, axis=1, keepdims=True)
    yield
    s = _dot_nt(q, k) * jnp.exp(dmat - m_in)
    q_c = _dot_nt(q, c_mat)
    yield
    num_in = _dot(s, v)
    den_in = jnp.sum(s, axis=1, keepdims=True)
    m_inter = b_c + m
    m_t = jnp.maximum(m_inter, m_in)
    s_inter, s_in = jnp.exp(m_inter - m_t), jnp.exp(m_in - m_t)
    yield
    num = s_inter * q_c + s_in * num_in
    den = s_inter * jnp.sum(q * n_vec, axis=1, keepdims=True) + s_in * den_in
    h = num / jnp.maximum(jnp.abs(den), jnp.exp(-m_t))
    m_new = jnp.maximum(b_last + m, m_loc)
    f_old, f_loc = jnp.exp(b_last + m - m_new), jnp.exp(m_loc - m_new)
    kw = k * jnp.exp(a_c - m_loc)
    yield
    c_mat = f_old * c_mat + f_loc * _dot_tn(v, kw)
    n_vec = f_old * n_vec + f_loc * jnp.sum(kw, axis=0, keepdims=True)
    return h, c_mat, n_vec, m_new


def _mlstm_body(d, q_ref, k_ref, v_ref, gc_ref, gr_ref, bias_ref, c0_ref, n0_ref, m0_ref,
                h_ref, c_ref, n_ref, m_ref):
    @pl.when(pl.program_id(1) == 0)
    def _():
        c_ref[...] = c0_ref[...]
        n_ref[...] = n0_ref[...]
        m_ref[...] = m0_ref[...]

    gc = gc_ref[0, 0]
    gr = gr_ref[0, 0]

    def head(h):
        hs = slice(h * DH, (h + 1) * DH)
        bi = bias_ref[0, :, h:h + 1]
        bf = bias_ref[0, :, N_HEADS + h:N_HEADS + h + 1]
        li_c = gc[:, h:h + 1] + bi
        lf_c = _log_sigmoid(gc[:, N_HEADS + h:N_HEADS + h + 1] + bf)
        li_r = gr[h:h + 1, :] + bi
        lf_r = _log_sigmoid(gr[N_HEADS + h:N_HEADS + h + 1, :] + bf)
        return _mlstm_block(d, q_ref[0, :, hs], k_ref[0, :, hs] * DH ** -0.5, v_ref[0, :, hs],
                            li_c, lf_c, li_r, lf_r, c_ref[0, h], n_ref[0, h], m_ref[0, h][:, 0:1])

    for h, (hv, c_new, n_new, m_new) in enumerate(_lockstep(head(h) for h in range(N_HEADS))):
        h_ref[0, :, h * DH:(h + 1) * DH] = hv
        c_ref[0, h] = c_new
        n_ref[0, h] = n_new
        m_ref[0, h] = jnp.broadcast_to(m_new, (1, DH))


def _gdn_block(d, q, k, v, bt_c, la_c, la_r, s_mat):
    tb = q.shape[0]
    nc = tb // CHUNK
    sh = CHUNK.bit_length() - 1
    ii, jj, same, vis, vis_t = _block_masks(d, tb)
    g_c = jnp.sum(vis.astype(F32) * la_r, axis=1, keepdims=True)
    g_r = jnp.sum(vis_t.astype(F32) * la_c, axis=0, keepdims=True)
    gl_c = jnp.sum(same.astype(F32) * la_r, axis=1, keepdims=True)
    decay = jnp.exp(jnp.where(vis, g_c - g_r, -jnp.inf))
    kb = k * bt_c
    yield
    both = _dot_nt(jnp.concatenate([kb, q], axis=0), k)
    a_mat = jnp.where(ii == jj, 0.0, both[:tb] * decay)
    attn = both[tb:] * decay
    yield
    off = lambda lvl: jnp.logical_and((ii >> (lvl + 1)) == (jj >> (lvl + 1)), (ii >> lvl) != (jj >> lvl))
    t_mat = (ii == jj).astype(F32) - jnp.where(off(0), a_mat, 0.0)
    for lvl in range(1, sh):
        te = _dot(t_mat, jnp.where(off(lvl), a_mat, 0.0))
        yield
        t_mat = t_mat - _dot(te, t_mat)
        yield
    eg = jnp.exp(g_c)
    uw = _dot(t_mat, jnp.concatenate([v * bt_c, kb * eg], axis=1))
    u, w = uw[:, :DH], uw[:, DH:]
    kd = k * jnp.exp(gl_c - g_c)
    qg = q * eg
    yield
    v_new, o_state = [None] * nc, [None] * nc
    for c in _chunk_order(d, nc):
        sl = slice(c * CHUNK, (c + 1) * CHUNK)
        ws = _dot(jnp.concatenate([w[sl], qg[sl]], axis=0), s_mat)
        v_new[c] = u[sl] - ws[:CHUNK]
        o_state[c] = ws[CHUNK:]
        yield
        s_mat = s_mat * jnp.exp(gl_c[c * CHUNK:c * CHUNK + 1, :]) + _dot_tn(kd[sl], v_new[c])
        yield
    o = jnp.concatenate(o_state, axis=0) + _dot(attn, jnp.concatenate(v_new, axis=0))
    return o, s_mat


def _gdn_body(d, q_ref, k_ref, v_ref, gc_ref, gr_ref, dtb_ref, alog_ref, s0_ref, o_ref, s_ref):
    @pl.when(pl.program_id(1) == 0)
    def _():
        s_ref[...] = s0_ref[...]

    gc = gc_ref[0, 0]
    gr = gr_ref[0, 0]

    def head(h):
        hs = slice(h * DH, (h + 1) * DH)
        dtb = dtb_ref[0, :, h:h + 1]
        neg_a = -jnp.exp(alog_ref[0, :, h:h + 1])
        bt_c = _sigmoid(gc[:, h:h + 1])
        la_c = neg_a * _softplus(gc[:, N_HEADS + h:N_HEADS + h + 1] + dtb)
        la_r = neg_a * _softplus(gr[N_HEADS + h:N_HEADS + h + 1, :] + dtb)
        return _gdn_block(d, q_ref[0, :, hs], k_ref[0, :, hs], v_ref[0, :, hs], bt_c, la_c, la_r, s_ref[0, h])

    for h, (ov, s_new) in enumerate(_lockstep(head(h) for h in range(N_HEADS))):
        o_ref[0, :, h * DH:(h + 1) * DH] = ov
        s_ref[0, h] = s_new


def _gate_views(graw):
    bsz, t, _ = graw.shape
    gcol = graw.reshape(bsz, t, 2, 2, N_HEADS).transpose(0, 3, 1, 2, 4).reshape(bsz, 2, t, 8)
    return gcol, gcol.transpose(0, 1, 3, 2)


def _scan_specs(d, t, col0):
    tb = min(SCAN_TB, t)
    nb = t // tb
    blk = (lambda i: i) if d == 0 else (lambda i: nb - 1 - i)
    qkv = [pl.BlockSpec((1, tb, MIX_W), lambda b, i, j=j: (b, blk(i), col0 + j)) for j in range(3)]
    gates = [pl.BlockSpec((1, 1, tb, 8), lambda b, i: (b, d, blk(i), 0)),
             pl.BlockSpec((1, 1, 8, tb), lambda b, i: (b, d, 0, blk(i)))]
    par = pl.BlockSpec((1, 1, 8), lambda b, i: (d, 0, 0))
    out = pl.BlockSpec((1, tb, MIX_W), lambda b, i: (b, blk(i), 0))
    return nb, qkv, gates, par, out


def _state_spec(*tail):
    return pl.BlockSpec((1, N_HEADS) + tail, lambda b, i: (b, 0) + (0,) * len(tail))


def mlstm_call(d, z, bias, state):
    bsz, t, _ = z.shape
    nb, qkv, gates, par, out = _scan_specs(d, t, E_QKV // MIX_W)
    gcol, grow = _gate_views(z[..., E_GATE:E_GATE + 16])
    st_specs = [_state_spec(DH, DH), _state_spec(1, DH), _state_spec(1, DH)]
    st_shapes = [jax.ShapeDtypeStruct((bsz, N_HEADS, DH, DH), F32),
                 jax.ShapeDtypeStruct((bsz, N_HEADS, 1, DH), F32),
                 jax.ShapeDtypeStruct((bsz, N_HEADS, 1, DH), F32)]
    res = pl.pallas_call(
        functools.partial(_mlstm_body, d),
        grid=(bsz, nb),
        in_specs=qkv + gates + [par] + st_specs,
        out_specs=[out] + st_specs,
        out_shape=[jax.ShapeDtypeStruct((bsz, t, MIX_W), F32)] + st_shapes,
        compiler_params=_cparams(("parallel", "arbitrary")),
        name="mlstm_scan_fwd" if d == 0 else "mlstm_scan_bwd",
    )(z, z, z, gcol, grow, bias, *state)
    return res[0], tuple(res[1:])


def gdn_call(d, qkv_arr, graw, dtb, alog, state):
    bsz, t, _ = qkv_arr.shape
    nb, qkv, gates, par, out = _scan_specs(d, t, 0)
    gcol, grow = _gate_views(graw)
    st_spec = _state_spec(DH, DH)
    return pl.pallas_call(
        functools.partial(_gdn_body, d),
        grid=(bsz, nb),
        in_specs=qkv + gates + [par, par, st_spec],
        out_specs=[out, st_spec],
        out_shape=[jax.ShapeDtypeStruct((bsz, t, MIX_W), F32),
                   jax.ShapeDtypeStruct((bsz, N_HEADS, DH, DH), F32)],
        compiler_params=_cparams(("parallel", "arbitrary")),
        name="gdn_scan_fwd" if d == 0 else "gdn_scan_bwd",
    )(qkv_arr, qkv_arr, qkv_arr, gcol, grow, dtb, alog, state)


def _outproj_body(act, hf_ref, hb_ref, gate_ref, y_ref, g_ref, w_ref, x_ref, mg_ref, o_ref):
    hs = hf_ref[0] + hb_ref[0]
    hn = jnp.concatenate([_rms(hs[:, h * DH:(h + 1) * DH]) for h in range(N_HEADS)], axis=-1) * g_ref[...]
    gate = gate_ref[0]
    hn = hn * (_sigmoid(gate) if act == "sigmoid" else _silu(gate))
    out = _dot(hn, w_ref[0:MIX_W, :]) + _dot(y_ref[0], w_ref[MIX_W:, :])
    o_ref[0] = x_ref[0] + mg_ref[0] * out


def outproj_call(act, hf, hb, z, gate_block, y, g, w, x, mgate):
    bn, t, d = x.shape
    tm = min(512, t)
    return pl.pallas_call(
        functools.partial(_outproj_body, act),
        grid=(bn, t // tm),
        in_specs=[pl.BlockSpec((1, tm, MIX_W), lambda b, i: (b, i, 0)),
                  pl.BlockSpec((1, tm, MIX_W), lambda b, i: (b, i, 0)),
                  pl.BlockSpec((1, tm, MIX_W), lambda b, i: (b, i, gate_block)),
                  pl.BlockSpec((1, tm, MIX_W), lambda b, i: (b, i, 0)),
                  pl.BlockSpec((1, MIX_W), lambda b, i: (0, 0)),
                  pl.BlockSpec((2 * MIX_W, d), lambda b, i: (0, 0)),
                  pl.BlockSpec((1, tm, d), lambda b, i: (b, i, 0)),
                  pl.BlockSpec((1, 1, d), lambda b, i: (b, 0, 0))],
        out_specs=pl.BlockSpec((1, tm, d), lambda b, i: (b, i, 0)),
        out_shape=jax.ShapeDtypeStruct((bn, t, d), F32),
        compiler_params=_cparams(("parallel", "parallel")),
        name="outproj_" + act,
    )(hf, hb, z, y, g, w, x, mgate)


def _ffn_body(x_ref, xp_ref, xn_ref, sh_ref, sc_ref, mg_ref, wu_ref, cw_ref, wd_ref, o_ref):
    i, nt = pl.program_id(1), pl.num_programs(1)
    sc, sh = 1.0 + sc_ref[0], sh_ref[0]
    x = x_ref[0]
    tm = x.shape[0]
    mod = lambda v: _rms(v) * sc + sh
    hp = mod(xp_ref[0]) * (i > 0).astype(F32)
    hn = mod(xn_ref[0]) * (i < nt - 1).astype(F32)
    h = jnp.concatenate([hp, mod(x), hn], axis=0).astype(BF16)
    rows = tm + 16

    cols = lambda j, half: slice(half * D_FF + j * FF_TILE, half * D_FF + (j + 1) * FF_TILE)
    up = lambda j: tuple(jnp.dot(h, wu_ref[:, cols(j, half)], preferred_element_type=F32) for half in range(2))

    def conv(u, cw):
        y = pltpu.roll(u, 1, 0) * cw[0:1, :] + u * cw[1:2, :] + pltpu.roll(u, rows - 1, 0) * cw[2:3, :] + cw[3:4, :]
        return y[8:8 + tm]

    acc = jnp.zeros(x.shape, F32)
    u_next = up(0)
    for j in range(N_FF_TILES):
        ua, ub = u_next
        if j + 1 < N_FF_TILES:
            u_next = up(j + 1)
        a, b = conv(ua, cw_ref[:, cols(j, 0)]), conv(ub, cw_ref[:, cols(j, 1)])
        gate = a * (0.5 + 0.5 * jnp.tanh(0.5 * a))
        acc = acc + _dot(gate * b, wd_ref[j * FF_TILE:(j + 1) * FF_TILE, :])
    o_ref[0] = x + mg_ref[0] * acc


def ffn_call(x, shift, scale, mgate, w_up, conv_w, conv_b, w_down):
    bn, t, d = x.shape
    tm = min(512, t)
    r8 = tm // 8
    wu, wd = w_up.astype(BF16), w_down.astype(BF16)
    cw = jnp.concatenate([conv_w, conv_b[None]], axis=0)
    full = lambda a: pl.BlockSpec(a.shape, lambda b, i: (0,) * a.ndim)
    vec = pl.BlockSpec((1, 1, d), lambda b, i: (b, 0, 0))
    return pl.pallas_call(
        _ffn_body,
        grid=(bn, t // tm),
        in_specs=[pl.BlockSpec((1, tm, d), lambda b, i: (b, i, 0)),
                  pl.BlockSpec((1, 8, d), lambda b, i: (b, jnp.maximum(i * r8 - 1, 0), 0)),
                  pl.BlockSpec((1, 8, d), lambda b, i: (b, jnp.minimum((i + 1) * r8, t // 8 - 1), 0)),
                  vec, vec, vec, full(wu), full(cw), full(wd)],
        out_specs=pl.BlockSpec((1, tm, d), lambda b, i: (b, i, 0)),
        out_shape=jax.ShapeDtypeStruct((bn, t, d), F32),
        compiler_params=_cparams(("parallel", "parallel")),
        name="conv_ffn",
    )(x, x, x, shift, scale, mgate, wu, cw, wd)


def _hy_filter_body(lf, w1t_ref, w1c_ref, w1s_ref, b1_ref, w2_ref, b2_ref, w3_ref, b3_ref, w4_ref, fr_ref,
                    band_ref, dl_ref, o_ref, l1_ref):
    r = pl.program_id(0)
    tr = o_ref.shape[0]
    n = (r * tr + lax.broadcasted_iota(jnp.int32, (tr, 1), 0))
    t = jnp.where(n < lf, n, 2 * lf - n).astype(F32)
    t_norm = t / (lf - 1)
    ang = (2.0 * math.pi * t / lf) * band_ref[...]
    fr = fr_ref[...]
    pre = t_norm * w1t_ref[...] + _dot_hi(jnp.cos(ang), w1c_ref[...]) - _dot_hi(jnp.sin(ang), w1s_ref[...])
    hdn = jnp.sin(fr * (pre + b1_ref[...]))
    hdn = jnp.sin(fr * (_dot_hi(hdn, w2_ref[...]) + b2_ref[...]))
    hdn = jnp.sin(fr * (_dot_hi(hdn, w3_ref[...]) + b3_ref[...]))
    hval = _dot(hdn, w4_ref[0]) * jnp.exp(-t_norm * jnp.abs(dl_ref[0]))
    hval = jnp.where(n == lf, 0.0, hval)
    o_ref[...] = hval

    @pl.when(r == 0)
    def _():
        l1_ref[...] = jnp.zeros_like(l1_ref)

    l1_ref[...] += jnp.sum(jnp.abs(hval), axis=0, keepdims=True)


def hy_filter_call(lf, f_w1, f_b1, f_w2, f_b2, f_w3, f_b3, f_w4, f_freq):
    tr = min(512, lf)
    nt = 2 * lf // tr
    half = lf // tr
    ncol = 2 * HYENA_W
    pad = lambda a: jnp.concatenate([a, jnp.zeros((DH - a.shape[0], a.shape[1]), F32)], axis=0)
    w4 = f_w4.reshape(HYENA_FFN, 2, 2, HYENA_W).transpose(2, 0, 1, 3).reshape(2, HYENA_FFN, ncol)
    deltas = jnp.linspace(math.log(HYENA_TARGET) / HYENA_SLOW_DECAY, math.log(HYENA_TARGET) / HYENA_FAST_DECAY,
                          HYENA_NFILT, dtype=F32)
    deltas = deltas.reshape(2, 2, HYENA_W).transpose(1, 0, 2).reshape(2, 1, ncol)
    bands = jnp.linspace(1e-4, HYENA_BANDS - 1, HYENA_BANDS, dtype=F32)
    bands = jnp.concatenate([bands, jnp.zeros((DH - HYENA_BANDS,), F32)]).reshape(1, DH)
    row = lambda a: a.reshape(1, -1)
    args = (f_w1[0:1], pad(f_w1[1:1 + HYENA_BANDS]), pad(f_w1[1 + HYENA_BANDS:]), row(f_b1), f_w2, row(f_b2),
            f_w3, row(f_b3), w4, row(f_freq), bands, deltas)
    full = lambda a: pl.BlockSpec(a.shape, lambda r: (0,) * a.ndim)
    by_dir = lambda a: pl.BlockSpec((1,) + a.shape[1:], lambda r: (r // half, 0, 0))
    in_specs = [by_dir(a) if a.ndim == 3 else full(a) for a in args]
    return pl.pallas_call(
        functools.partial(_hy_filter_body, lf),
        grid=(nt,),
        in_specs=in_specs,
        out_specs=[pl.BlockSpec((tr, ncol), lambda r: (r, 0)), pl.BlockSpec((1, ncol), lambda r: (0, 0))],
        out_shape=[jax.ShapeDtypeStruct((2 * lf, ncol), F32), jax.ShapeDtypeStruct((1, ncol), F32)],
        compiler_params=_cparams(("arbitrary",)),
        name="hyena_filter",
    )(*args)


def _dft_consts(lf):
    n1 = FFT_N1
    n = 2 * lf
    assert n == n1 * n1
    idx = np.arange(n1)
    f = np.exp(-2j * np.pi * np.outer(idx, idx) / n1)
    tw = np.exp(-2j * np.pi * np.outer(idx, idx) / n)
    fh = f[:, :n1 // 2]
    blk = lambda a: np.block([[a.real, -a.imag], [a.imag, a.real]])
    f_sig = blk(fh)
    f_flt = np.concatenate([f.real, f.imag], axis=0)
    c_inv = blk(np.conj(f)[:n1 // 2, :] / n)
    c = lambda a: jnp.asarray(a, dtype=F32)
    cb = lambda a: jnp.asarray(a, dtype=F32).astype(BF16)
    return dict(f_sig=cb(f_sig), f_flt=cb(f_flt), c_inv=cb(c_inv), fr=c(f.real), fi=c(f.imag),
                twr=c(tw.real.reshape(n1, 1, n1)), twi=c(tw.imag.reshape(n1, 1, n1)))


LEAD_ROWS = 8


def _lead_mm_body(m_ref, x_ref, o_ref):
    w = x_ref.shape[-1]
    for j in range(LEAD_ROWS):
        o_ref[:, j * w:(j + 1) * w] = _dot(m_ref[...], x_ref[:, j, :]).astype(o_ref.dtype)


def lead_mm_call(mat, x3, group, w, name):
    r, k = mat.shape
    n1 = x3.shape[1]
    return pl.pallas_call(
        _lead_mm_body,
        grid=(n1 // LEAD_ROWS,),
        in_specs=[pl.BlockSpec((r, k), lambda j: (0, 0)), pl.BlockSpec((k, LEAD_ROWS, w), lambda j: (0, j, group))],
        out_specs=pl.BlockSpec((r, LEAD_ROWS * w), lambda j: (0, j)),
        out_shape=jax.ShapeDtypeStruct((r, n1 * w), BF16),
        compiler_params=_cparams(("parallel",)),
        name=name,
    )(mat, x3)


def _stage2_mat(fr, fi, twr, twi):
    hr = fr * twr - fi * twi
    hi = fr * twi + fi * twr
    return hr, hi


def _blk(ar, ai):
    return jnp.concatenate([jnp.concatenate([ar, -ai], axis=1), jnp.concatenate([ai, ar], axis=1)], axis=0)


def _spec_filter_body(a_ref, l1_ref, fr_ref, fi_ref, twr_ref, twi_ref, o_ref):
    hr, hi = _stage2_mat(fr_ref[...], fi_ref[...], twr_ref[0], twi_ref[0])
    n1 = hr.shape[0]
    o_ref[0] = _dot(_blk(hr, hi), a_ref[...].reshape(2 * n1, a_ref.shape[-1])) * (1.0 / l1_ref[...])


def _spec_conv_body(a_ref, k_ref, fr_ref, fi_ref, twr_ref, twi_ref, o_ref):
    hr, hi = _stage2_mat(fr_ref[...], fi_ref[...], twr_ref[0], twi_ref[0])
    n1 = hr.shape[0]
    x = _dot(_blk(hr, hi), a_ref[...].reshape(2 * n1, a_ref.shape[-1]))
    xr, xi = x[:n1], x[n1:]
    kr, ki = k_ref[0, :n1], k_ref[0, n1:]
    y = jnp.concatenate([xr * kr - xi * ki, xr * ki + xi * kr], axis=0)
    o_ref[0] = _dot(_blk(hr.T, -(hi.T)), y).astype(o_ref.dtype)


def spec_call(a2d, kspec, order, cst, l1=None):
    n1 = FFT_N1
    c = a2d.shape[1] // n1
    ct = 512 if kspec is not None else min(1024, c)
    a4 = a2d.reshape(2, n1, n1, c)
    sq = pl.BlockSpec((n1, n1), lambda k, j: (0, 0))
    tw = pl.BlockSpec((1, 1, n1), lambda k, j: (k, 0, 0))
    a_spec = pl.BlockSpec((2, 1, n1, ct), lambda k, j: (0, k, 0, j))
    o_spec = pl.BlockSpec((1, 2 * n1, ct), lambda k, j: (k, 0, j))
    common = dict(grid=(n1, c // ct), out_specs=o_spec, compiler_params=_cparams(("parallel", "parallel")))
    if kspec is None:
        l1_spec = pl.BlockSpec((1, ct), lambda k, j: (0, j))
        return pl.pallas_call(_spec_filter_body, in_specs=[a_spec, l1_spec, sq, sq, tw, tw], name="hyena_spec_filter",
                              out_shape=jax.ShapeDtypeStruct((n1, 2 * n1, c), F32),
                              **common)(a4, l1, cst["fr"], cst["fi"], cst["twr"], cst["twi"])
    k_spec = pl.BlockSpec((1, 2 * n1, ct), lambda k, j: (k, 0, order * (c // ct) + j))
    return pl.pallas_call(_spec_conv_body, in_specs=[a_spec, k_spec, sq, sq, tw, tw], name="hyena_spec_conv",
                          out_shape=jax.ShapeDtypeStruct((n1, 2 * n1, c), BF16),
                          **common)(a4, kspec, cst["fr"], cst["fi"], cst["twr"], cst["twi"])


def _inv_gate_body(m_ref, br_ref, bi_ref, v_ref, x_ref, d_ref, o_ref):
    w = o_ref.shape[-1]
    y = _dot(m_ref[...], jnp.concatenate([br_ref[...], bi_ref[...]], axis=0))
    for j in range(LEAD_ROWS):
        o_ref[:, j, :] = x_ref[:, j, :] * (y[:, j * w:(j + 1) * w] + v_ref[:, j, :] * d_ref[...])


def inv_gate_call(b3, v3, gv, x3, gx, d_row, cst):
    n1 = FFT_N1
    w = HYENA_W
    nb = n1 // LEAD_ROWS
    b2d = b3.reshape(n1, 2 * n1 * w)
    tn = LEAD_ROWS * w
    rows3 = lambda g: pl.BlockSpec((n1, LEAD_ROWS, w), lambda j: (0, j, g))
    return pl.pallas_call(
        _inv_gate_body,
        grid=(nb,),
        in_specs=[pl.BlockSpec((n1, 2 * n1), lambda j: (0, 0)), pl.BlockSpec((n1, tn), lambda j: (0, j)),
                  pl.BlockSpec((n1, tn), lambda j: (0, nb + j)), rows3(gv), rows3(gx),
                  pl.BlockSpec((1, w), lambda j: (0, 0))],
        out_specs=rows3(0),
        out_shape=jax.ShapeDtypeStruct((n1, n1, w), F32),
        compiler_params=_cparams(("parallel",)),
        name="hyena_inv_gate",
    )(cst["c_inv"], b2d, b2d, v3, x3, d_row)


def hyena_long(u, kf, l1, d_skip):
    bsz, lf, _ = u.shape
    assert bsz == 2
    n1 = FFT_N1
    w = HYENA_W
    cst = _dft_consts(lf)
    u3 = u.reshape(bsz * n1 // 2, n1, 3 * w)
    a_flt = lead_mm_call(cst["f_flt"], kf.reshape(n1, n1, 2 * w), 0, 2 * w, "hyena_lead_filter")
    kspec = spec_call(a_flt, None, 0, cst, l1)
    y3, gy = u3, 0
    for o in range(2):
        a = lead_mm_call(cst["f_sig"], y3, gy, w, "hyena_lead_signal")
        b3 = spec_call(a, kspec, o, cst)
        y3, gy = inv_gate_call(b3, y3, gy, u3, o + 1, d_skip[o:o + 1], cst), 0
    return y3.reshape(bsz, lf, w)


def _hyena_small_body(u_ref, kf_ref, l1_ref, fs_ref, ff_ref, ci_ref, d_ref, o_ref):
    w = HYENA_W
    nk = ff_ref.shape[0] // 2
    kspec = _dot(ff_ref[...], kf_ref[...]) * (1.0 / l1_ref[...])
    ucat = jnp.concatenate([u_ref[0], u_ref[1]], axis=0)
    y = ucat[:, :w]
    for o in range(2):
        x = _dot(fs_ref[...], y)
        xr, xi = x[:nk], x[nk:]
        kr, ki = kspec[:nk, o * w:(o + 1) * w], kspec[nk:, o * w:(o + 1) * w]
        prod = jnp.concatenate([xr * kr - xi * ki, xr * ki + xi * kr], axis=0)
        conv = _dot(ci_ref[...], prod)
        y = ucat[:, (o + 1) * w:(o + 2) * w] * (conv + y * d_ref[o:o + 1, :])
    lf = u_ref.shape[1]
    o_ref[0] = y[:lf]
    o_ref[1] = y[lf:]


def hyena_small(u, kf, l1, d_skip):
    bsz, lf, _ = u.shape
    assert bsz == 2
    n = 2 * lf
    idx = np.arange(n)
    f = np.exp(-2j * np.pi * np.outer(idx, idx) / n)
    blk = lambda a: np.block([[a.real, -a.imag], [a.imag, a.real]])
    cb = lambda a: jnp.asarray(a, F32).astype(BF16)
    fs = cb(blk(f[:, :lf]))
    ff = cb(np.concatenate([f.real, f.imag], axis=0))
    ci = cb(blk(np.conj(f)[:lf, :] / n))
    return pl.pallas_call(
        _hyena_small_body,
        out_shape=jax.ShapeDtypeStruct((bsz, lf, HYENA_W), F32),
        compiler_params=pltpu.CompilerParams(vmem_limit_bytes=VMEM_LIMIT),
        name="hyena_small",
    )(u, kf, l1, fs, ff, ci, d_skip)


def _rope(x, cos_ref, sa_ref, sb_ref):
    w = x.shape[-1]
    half = MLA_ROPE // 2
    return x * cos_ref[...] + pltpu.roll(x, w - half, 1) * sa_ref[...] + pltpu.roll(x, half, 1) * sb_ref[...]


def _kv_body(rope, lkv_ref, lkr_ref, g_ref, w_ref, kg_ref, cos_ref, sa_ref, sb_ref, k_ref, v_ref):
    kv = _dot(_rms(lkv_ref[0]) * g_ref[...], w_ref[...])
    kr = lkr_ref[0]
    for h in range(N_HEADS):
        base = h * (MLA_NOPE + DH)
        k = jnp.concatenate([kv[:, base:base + MLA_NOPE], kr], axis=-1)
        k = _rms(k, MLA_QK) * kg_ref[...]
        if rope:
            k = _rope(k, cos_ref, sa_ref, sb_ref)
        k_ref[0, h] = k.astype(BF16)
        v = kv[:, base + MLA_NOPE:base + MLA_NOPE + DH]
        v_ref[0, h] = jnp.concatenate([v, jnp.ones_like(v)], axis=-1).astype(BF16)


def kv_call(z, rope, g, w, kg, tabs):
    bn, t, _ = z.shape
    tm = min(512, t)
    tab = pl.BlockSpec((tm, MLA_QK_PAD), lambda b, i: (i, 0))
    return pl.pallas_call(
        functools.partial(_kv_body, rope),
        grid=(bn, t // tm),
        in_specs=[pl.BlockSpec((1, tm, MLA_KV_RANK), lambda b, i: (b, i, O_LKV // MLA_KV_RANK)),
                  pl.BlockSpec((1, tm, DH), lambda b, i: (b, i, O_LKR // DH)),
                  pl.BlockSpec((1, MLA_KV_RANK), lambda b, i: (0, 0)),
                  pl.BlockSpec(w.shape, lambda b, i: (0, 0)),
                  pl.BlockSpec((1, MLA_QK_PAD), lambda b, i: (0, 0)), tab, tab, tab],
        out_specs=[pl.BlockSpec((1, N_HEADS, tm, MLA_QK_PAD), lambda b, i: (b, 0, i, 0)),
                   pl.BlockSpec((1, N_HEADS, tm, 2 * DH), lambda b, i: (b, 0, i, 0))],
        out_shape=[jax.ShapeDtypeStruct((bn, N_HEADS, t, MLA_QK_PAD), BF16),
                   jax.ShapeDtypeStruct((bn, N_HEADS, t, 2 * DH), BF16)],
        compiler_params=_cparams(("parallel", "parallel")),
        name="mla_kv_rope" if rope else "mla_kv",
    )(z, z, g, w, kg, *tabs)


def _q_body(lq_ref, g_ref, w_ref, qg_ref, cos_ref, sa_ref, sb_ref, q_ref):
    q = _dot(_rms(lq_ref[0]) * g_ref[...], w_ref[...])
    for h in range(N_HEADS):
        qh = _rms(q[:, h * MLA_QK_PAD:(h + 1) * MLA_QK_PAD], MLA_QK) * qg_ref[...]
        qh = _rope(qh, cos_ref, sa_ref, sb_ref) * (MLA_QK ** -0.5 * LOG2E)
        q_ref[0, h] = qh.astype(BF16)


def q_call(z, g, w, qg, tabs):
    bn, t, _ = z.shape
    tm = min(512, t)
    tab = pl.BlockSpec((tm, MLA_QK_PAD), lambda b, i: (i, 0))
    return pl.pallas_call(
        _q_body,
        grid=(bn, t // tm),
        in_specs=[pl.BlockSpec((1, tm, MLA_Q_RANK), lambda b, i: (b, i, O_LQ // MLA_Q_RANK)),
                  pl.BlockSpec((1, MLA_Q_RANK), lambda b, i: (0, 0)),
                  pl.BlockSpec(w.shape, lambda b, i: (0, 0)),
                  pl.BlockSpec((1, MLA_QK_PAD), lambda b, i: (0, 0)), tab, tab, tab],
        out_specs=pl.BlockSpec((1, N_HEADS, tm, MLA_QK_PAD), lambda b, i: (b, 0, i, 0)),
        out_shape=jax.ShapeDtypeStruct((bn, N_HEADS, t, MLA_QK_PAD), BF16),
        compiler_params=_cparams(("parallel", "parallel")),
        name="mla_q",
    )(z, g, w, qg, *tabs)


def _attn_body(q_ref, kc_ref, vc_ref, kx_ref, vx_ref, o_ref):
    q = q_ref[0, 0]
    tiles = [(kc_ref, vc_ref, 0, kc_ref.shape[2])]
    tk = min(ATTN_TK, kx_ref.shape[2])
    tiles += [(kx_ref, vx_ref, s0, tk) for s0 in range(0, kx_ref.shape[2], tk)]
    scores = lambda t: lax.dot_general(q, t[0][0, 0, t[2]:t[2] + t[3], :], (((1,), (1,)), ((), ())),
                                       preferred_element_type=F32)
    m = jnp.full((q.shape[0], 1), -jnp.inf, F32)
    acc = jnp.zeros((q.shape[0], 2 * DH), F32)
    s_next = scores(tiles[0])
    for idx, t in enumerate(tiles):
        s = s_next
        if idx + 1 < len(tiles):
            s_next = scores(tiles[idx + 1])
        m_new = jnp.maximum(m, jnp.max(s, axis=1, keepdims=True))
        p = jnp.exp2(s - m_new).astype(BF16)
        acc = jnp.exp2(m - m_new) * acc + jnp.dot(p, t[1][0, 0, t[2]:t[2] + t[3], :], preferred_element_type=F32)
        m = m_new
    o_ref[0] = acc[:, :DH] / acc[:, DH:]


def attn_call(q, kc, vc, kx, vx):
    bn, nh, t, dq = q.shape
    tc = kc.shape[2]
    tq = ATTN_TQ
    whole = lambda n, w: pl.BlockSpec((1, 1, n, w), lambda b, h, i: (b, h, 0, 0))
    return pl.pallas_call(
        _attn_body,
        grid=(bn, nh, t // tq),
        in_specs=[pl.BlockSpec((1, 1, tq, dq), lambda b, h, i: (b, h, i, 0)),
                  whole(tc, dq), whole(tc, 2 * DH), whole(t, dq), whole(t, 2 * DH)],
        out_specs=pl.BlockSpec((1, tq, DH), lambda b, h, i: (b, i, h)),
        out_shape=jax.ShapeDtypeStruct((bn, t, nh * DH), F32),
        compiler_params=_cparams(("parallel", "parallel", "parallel")),
        name="mla_attention",
    )(q, kc, vc, kx, vx)


def rope_tables(seq):
    rows = seq // GRID_W
    row = jnp.repeat(jnp.arange(rows), GRID_W).astype(F32)
    col = jnp.tile(jnp.arange(GRID_W), rows).astype(F32)
    n_freq = MLA_ROPE // 4
    inv = ROPE_THETA ** (-jnp.arange(n_freq, dtype=F32) / n_freq)
    ang = jnp.concatenate([row[:, None] * inv, col[:, None] * inv], axis=-1)
    cos, sin = jnp.cos(ang), jnp.sin(ang)
    half = MLA_ROPE // 2
    ones = jnp.ones((seq, MLA_NOPE), F32)
    zn = jnp.zeros((seq, MLA_NOPE), F32)
    zh = jnp.zeros((seq, half), F32)
    zp = jnp.zeros((seq, MLA_QK_PAD - MLA_QK), F32)
    cos_t = jnp.concatenate([ones, cos, cos, zp], axis=-1)
    sin_a = jnp.concatenate([zn, -sin, zh, zp], axis=-1)
    sin_b = jnp.concatenate([zn, zh, sin, zp], axis=-1)
    return cos_t, sin_a, sin_b


def _padcols(w, n):
    return jnp.concatenate([w, jnp.zeros((w.shape[0], n - w.shape[1]), w.dtype)], axis=1)


def pack_even(w):
    q, k, v, o = (w[:, j * MIX_W:(j + 1) * MIX_W] for j in range(4))
    gates = w[:, 4 * MIX_W:4 * MIX_W + 16]
    hy = w[:, 4 * MIX_W + 16:]
    return jnp.concatenate([q, k, v, hy, o, _padcols(gates, DH)], axis=1).astype(BF16)


def pack_odd(w):
    q, k, v, gg = (w[:, j * MIX_W:(j + 1) * MIX_W] for j in range(4))
    c0 = 4 * MIX_W
    gates = w[:, c0:c0 + 16]
    lq = w[:, c0 + 16:c0 + 16 + MLA_Q_RANK]
    lkv = w[:, c0 + 16 + MLA_Q_RANK:c0 + 16 + MLA_Q_RANK + MLA_KV_RANK]
    lkr = w[:, c0 + 16 + MLA_Q_RANK + MLA_KV_RANK:]
    return jnp.concatenate([q, k, v, gg, lkv, lq, _padcols(lkr, DH), _padcols(gates, DH)], axis=1).astype(BF16)


def kernel(x, c, ctx, c_ctx, mod_w, mod_b, a_w_in, a_i_bias, a_f_bias, a_norm_g, b_conv_w, b_conv_b, b_f_w1, b_f_b1, b_f_w2, b_f_b2, b_f_w3, b_f_b3, b_f_w4, b_f_freq, b_d, ab_w_out, cd_w_in, c_conv_w, c_conv_b, c_a_log, c_dt_bias, c_norm_g, d_q_norm_g, d_w_q_up, d_kv_norm_g, d_w_kv_up, d_qn_g, d_kn_g, cd_w_out, ffn_w_up, ffn_conv_w, ffn_conv_b, ffn_w_down):
    bsz, seq, d = x.shape
    ctx_len = ctx.shape[1]
    cvec = jnp.concatenate([c, c_ctx[None], jnp.zeros((8 - bsz - 1, d), F32)], axis=0)
    mods = mods_call(cvec, mod_w, mod_b)
    mx = lambda l, j: mods[l, :bsz, None, j * d:(j + 1) * d]
    mc = lambda l, j: jnp.broadcast_to(mods[l, bsz, j * d:(j + 1) * d], (bsz, 1, d))

    def ffn(l, h, m):
        return ffn_call(h, m(l, 3), m(l, 4), m(l, 5), ffn_w_up[l], ffn_conv_w[l], ffn_conv_b[l], ffn_w_down[l])

    w_in = pack_even(a_w_in[0])
    zx = inproj_call(x, mx(0, 0), mx(0, 1), w_in)
    zc = inproj_call(ctx, mc(0, 0), mc(0, 1), w_in)
    bias = jnp.concatenate([a_i_bias[0].reshape(2, N_HEADS), a_f_bias[0].reshape(2, N_HEADS)], axis=-1)[:, None, :]
    zero_state = (jnp.zeros((bsz, N_HEADS, DH, DH), F32), jnp.zeros((bsz, N_HEADS, 1, DH), F32),
                  jnp.zeros((bsz, N_HEADS, 1, DH), F32))
    hc, hx = [], []
    for dd in range(2):
        h_ctx, st = mlstm_call(dd, zc, bias, zero_state)
        hc.append(h_ctx)
        hx.append(mlstm_call(dd, zx, bias, st)[0])
    filt = (b_f_w1[0], b_f_b1[0], b_f_w2[0], b_f_b2[0], b_f_w3[0], b_f_b3[0], b_f_w4[0], b_f_freq[0])
    ux = dwconv_call(zx, E_HY // (3 * HYENA_W), b_conv_w[0], b_conv_b[0], False)
    uc = dwconv_call(zc, E_HY // (3 * HYENA_W), b_conv_w[0], b_conv_b[0], False)
    yx = hyena_long(ux, *hy_filter_call(seq, *filt), b_d[0])
    yc = hyena_small(uc, *hy_filter_call(ctx_len, *filt), b_d[0])
    w_out = ab_w_out[0].astype(BF16)
    g = a_norm_g[0].reshape(1, MIX_W)
    x = outproj_call("sigmoid", hx[0], hx[1], zx, E_MO // MIX_W, yx, g, w_out, x, mx(0, 2))
    ctx = outproj_call("sigmoid", hc[0], hc[1], zc, E_MO // MIX_W, yc, g, w_out, ctx, mc(0, 2))
    x = ffn(0, x, mx)
    ctx = ffn(0, ctx, mc)

    w_in = pack_odd(cd_w_in[0])
    zx = inproj_call(x, mx(1, 0), mx(1, 1), w_in)
    zc = inproj_call(ctx, mc(1, 0), mc(1, 1), w_in)
    qkv_x = dwconv_call(zx, 0, c_conv_w[0], c_conv_b[0], True)
    qkv_c = dwconv_call(zc, 0, c_conv_w[0], c_conv_b[0], True)
    zeros4 = jnp.zeros((2, 1, N_HEADS), F32)
    dtb = jnp.concatenate([c_dt_bias[0][:, None, :], zeros4], axis=-1)
    alog = jnp.concatenate([c_a_log[0][:, None, :], zeros4], axis=-1)
    ox = []
    for dd in range(2):
        _, s_ctx = gdn_call(dd, qkv_c, zc[..., O_GATE:O_GATE + 16], dtb, alog, zero_state[0])
        ox.append(gdn_call(dd, qkv_x, zx[..., O_GATE:O_GATE + 16], dtb, alog, s_ctx)[0])
    tabs = rope_tables(seq)
    kvg = d_kv_norm_g[0].reshape(1, -1)
    w_kv = d_w_kv_up[0].astype(BF16)
    kn_g = _padcols(d_kn_g[0].reshape(1, -1), MLA_QK_PAD)
    kc, vc = kv_call(zc, False, kvg, w_kv, kn_g, tuple(t[:ctx_len] for t in tabs))
    kx, vx = kv_call(zx, True, kvg, w_kv, kn_g, tabs)
    w_q = jnp.concatenate([_padcols(d_w_q_up[0][:, h * MLA_QK:(h + 1) * MLA_QK], MLA_QK_PAD)
                           for h in range(N_HEADS)], axis=1).astype(BF16)
    qx = q_call(zx, d_q_norm_g[0].reshape(1, -1), w_q, _padcols(d_qn_g[0].reshape(1, -1), MLA_QK_PAD), tabs)
    ax = attn_call(qx, kc, vc, kx, vx)
    g = jnp.tile(c_norm_g[0], N_HEADS).reshape(1, MIX_W)
    x = outproj_call("silu", ox[0], ox[1], zx, O_GG // MIX_W, ax, g, cd_w_out[0].astype(BF16), x, mx(1, 2))
    x = ffn(1, x, mx)
    return x
```

```python
import functools
import math

import numpy as np
import jax
import jax.numpy as jnp
from jax import lax
from jax.experimental import pallas as pl
from jax.experimental.pallas import tpu as pltpu

F32 = jnp.float32
BF16 = jnp.bfloat16
HI = lax.Precision.HIGHEST
LOG2E = math.log2(math.e)

D_MODEL = 1024
GRID_W = 64
EPS = 1e-6
CHUNK = 64
ROPE_THETA = 10000.0
N_HEADS = 4
DH = 128
MIX_W = N_HEADS * DH
HYENA_W = 512
HYENA_EMB = 33
HYENA_BANDS = 16
HYENA_FFN = 64
HYENA_NFILT = 4 * HYENA_W
HYENA_FAST_DECAY = 0.3
HYENA_SLOW_DECAY = 1.5
HYENA_TARGET = 1e-2
MLA_NOPE = 128
MLA_ROPE = 64
MLA_QK = 192
MLA_QK_PAD = 256
MLA_Q_RANK = 384
MLA_KV_RANK = 256
D_FF = 2816
FF_TILE = 256
N_FF_TILES = D_FF // FF_TILE
FFT_N1 = 128
SCAN_TB = 256
ATTN_TQ = 512
ATTN_TK = 2048
VMEM_LIMIT = 56 * 1024 * 1024

E_QKV, E_HY, E_MO, E_GATE, E_N = 0, 1536, 3072, 3584, 3712
O_QKV, O_GG, O_LKV, O_LQ, O_LKR, O_GATE, O_N = 0, 1536, 2048, 2304, 2688, 2816, 2944


def _cparams(sem):
    return pltpu.CompilerParams(dimension_semantics=sem, vmem_limit_bytes=VMEM_LIMIT)


def _dot(a, b):
    return jnp.dot(a.astype(BF16), b.astype(BF16), preferred_element_type=F32)


def _dot_nt(a, b):
    return lax.dot_general(a.astype(BF16), b.astype(BF16), (((1,), (1,)), ((), ())), preferred_element_type=F32)


def _dot_tn(a, b):
    return lax.dot_general(a.astype(BF16), b.astype(BF16), (((0,), (0,)), ((), ())), preferred_element_type=F32)


def _dot_hi(a, b):
    return jnp.dot(a, b, precision=HI, preferred_element_type=F32)


def _rms(x, n=None):
    n = x.shape[-1] if n is None else n
    return x * lax.rsqrt(jnp.sum(x * x, axis=-1, keepdims=True) * (1.0 / n) + EPS)


def _sigmoid(x):
    return 0.5 + 0.5 * jnp.tanh(0.5 * x)


def _silu(x):
    return x * _sigmoid(x)


def _softplus(x):
    return jnp.maximum(x, 0.0) + jnp.log(1.0 + jnp.exp(-jnp.abs(x)))


def _log_sigmoid(x):
    return -_softplus(-x)


def _mods_body(c_ref, w_ref, b_ref, o_ref):
    o_ref[0] = _dot_hi(_silu(c_ref[...]), w_ref[0]) + b_ref[0]


def mods_call(cvec, mod_w, mod_b):
    depth, d, n = mod_w.shape
    tn = 1536
    return pl.pallas_call(
        _mods_body,
        grid=(depth, n // tn),
        in_specs=[pl.BlockSpec((8, d), lambda l, j: (0, 0)),
                  pl.BlockSpec((1, d, tn), lambda l, j: (l, 0, j)),
                  pl.BlockSpec((1, 1, tn), lambda l, j: (l, 0, j))],
        out_specs=pl.BlockSpec((1, 8, tn), lambda l, j: (l, 0, j)),
        out_shape=jax.ShapeDtypeStruct((depth, 8, n), F32),
        compiler_params=_cparams(("parallel", "parallel")),
        name="mods",
    )(cvec, mod_w, mod_b.reshape(depth, 1, n))


def _inproj_body(x_ref, sh_ref, sc_ref, w_ref, o_ref):
    h = _rms(x_ref[0]) * (1.0 + sc_ref[0]) + sh_ref[0]
    o_ref[0] = _dot(h, w_ref[...])


def inproj_call(x, shift, scale, w):
    bn, t, d = x.shape
    n = w.shape[1]
    tm = min(512, t)
    return pl.pallas_call(
        _inproj_body,
        grid=(bn, t // tm),
        in_specs=[pl.BlockSpec((1, tm, d), lambda b, i: (b, i, 0)),
                  pl.BlockSpec((1, 1, d), lambda b, i: (b, 0, 0)),
                  pl.BlockSpec((1, 1, d), lambda b, i: (b, 0, 0)),
                  pl.BlockSpec((d, n), lambda b, i: (0, 0))],
        out_specs=pl.BlockSpec((1, tm, n), lambda b, i: (b, i, 0)),
        out_shape=jax.ShapeDtypeStruct((bn, t, n), F32),
        compiler_params=_cparams(("parallel", "parallel")),
        name="inproj",
    )(x, shift, scale, w)


def _shift_rows(u, prev_row, next_row):
    tm = u.shape[0]
    rows = lax.broadcasted_iota(jnp.int32, (tm, 1), 0)
    um = jnp.where(rows == 0, prev_row, pltpu.roll(u, 1, 0))
    up = jnp.where(rows == tm - 1, next_row, pltpu.roll(u, tm - 1, 0))
    return um, up


def _dwconv_body(gdn, z_ref, zp_ref, zn_ref, w_ref, b_ref, o_ref):
    i, nt = pl.program_id(1), pl.num_programs(1)
    u = z_ref[0]
    prev_row = zp_ref[0][7:8, :] * (i > 0).astype(F32)
    next_row = zn_ref[0][0:1, :] * (i < nt - 1).astype(F32)
    um, up = _shift_rows(u, prev_row, next_row)
    y = um * w_ref[0:1, :] + u * w_ref[1:2, :] + up * w_ref[2:3, :] + b_ref[...]
    if not gdn:
        o_ref[0] = y
        return
    y = _silu(y)
    for h in range(2 * N_HEADS):
        s = y[:, h * DH:(h + 1) * DH]
        s = s * lax.rsqrt(jnp.sum(s * s, axis=-1, keepdims=True) + EPS)
        if h < N_HEADS:
            s = s * DH ** -0.5
        o_ref[0, :, h * DH:(h + 1) * DH] = s
    o_ref[0, :, 2 * MIX_W:] = y[:, 2 * MIX_W:]


def dwconv_call(z, col_block, w, b, gdn):
    bn, t, _ = z.shape
    c = w.shape[1]
    tm = min(512, t)
    r8 = tm // 8
    return pl.pallas_call(
        functools.partial(_dwconv_body, gdn),
        grid=(bn, t // tm),
        in_specs=[pl.BlockSpec((1, tm, c), lambda b_, i: (b_, i, col_block)),
                  pl.BlockSpec((1, 8, c), lambda b_, i: (b_, jnp.maximum(i * r8 - 1, 0), col_block)),
                  pl.BlockSpec((1, 8, c), lambda b_, i: (b_, jnp.minimum((i + 1) * r8, t // 8 - 1), col_block)),
                  pl.BlockSpec((3, c), lambda b_, i: (0, 0)),
                  pl.BlockSpec((1, c), lambda b_, i: (0, 0))],
        out_specs=pl.BlockSpec((1, tm, c), lambda b_, i: (b_, i, 0)),
        out_shape=jax.ShapeDtypeStruct((bn, t, c), F32),
        compiler_params=_cparams(("parallel", "parallel")),
        name="dwconv_gdn" if gdn else "dwconv",
    )(z, z, z, w, b.reshape(1, c))


def _block_masks(d, tb):
    sh = CHUNK.bit_length() - 1
    ii = lax.broadcasted_iota(jnp.int32, (tb, tb), 0)
    jj = lax.broadcasted_iota(jnp.int32, (tb, tb), 1)
    same = (ii >> sh) == (jj >> sh)
    vis = jnp.logical_and(same, (ii >= jj) if d == 0 else (ii <= jj))
    vis_t = jnp.logical_and(same, (jj >= ii) if d == 0 else (jj <= ii))
    return ii, jj, same, vis, vis_t


def _chunk_order(d, nc):
    return range(nc) if d == 0 else reversed(range(nc))


def _lockstep(gens):
    gens = list(gens)
    results = [None] * len(gens)
    live = list(range(len(gens)))
    while live:
        for i in list(live):
            try:
                next(gens[i])
            except StopIteration as stop:
                results[i] = stop.value
                live.remove(i)
    return results


def _mlstm_block(d, q, k, v, li_c, lf_c, li_r, lf_r, c_mat, n_vec, m):
    tb = q.shape[0]
    ii = lax.broadcasted_iota(jnp.int32, (tb, tb), 0)
    jj = lax.broadcasted_iota(jnp.int32, (tb, tb), 1)
    vis = (ii >= jj) if d == 0 else (ii <= jj)
    vis_t = (jj >= ii) if d == 0 else (jj <= ii)
    b_c = jnp.sum(vis.astype(F32) * lf_r, axis=1, keepdims=True)
    b_r = jnp.sum(vis_t.astype(F32) * lf_c, axis=0, keepdims=True)
    b_last = jnp.sum(lf_r, axis=1, keepdims=True)
    a_c = b_last - b_c + li_c
    m_loc = jnp.max(a_c, axis=0, keepdims=True)
    dmat = jnp.where(vis, b_c - b_r + li_r, -jnp.inf)
    m_in = jnp.max(dmat, axis=1, keepdims=True)
    yield
    s = _dot_nt(q, k) * jnp.exp(dmat - m_in)
    q_c = _dot_nt(q, c_mat)
    yield
    num_in = _dot(s, v)
    den_in = jnp.sum(s, axis=1, keepdims=True)
    m_inter = b_c + m
    m_t = jnp.maximum(m_inter, m_in)
    s_inter, s_in = jnp.exp(m_inter - m_t), jnp.exp(m_in - m_t)
    yield
    num = s_inter * q_c + s_in * num_in
    den = s_inter * jnp.sum(q * n_vec, axis=1, keepdims=True) + s_in * den_in
    h = num / jnp.maximum(jnp.abs(den), jnp.exp(-m_t))
    m_new = jnp.maximum(b_last + m, m_loc)
    f_old, f_loc = jnp.exp(b_last + m - m_new), jnp.exp(m_loc - m_new)
    kw = k * jnp.exp(a_c - m_loc)
    yield
    c_mat = f_old * c_mat + f_loc * _dot_tn(v, kw)
    n_vec = f_old * n_vec + f_loc * jnp.sum(kw, axis=0, keepdims=True)
    return h, c_mat, n_vec, m_new


def _mlstm_body(d, q_ref, k_ref, v_ref, gc_ref, gr_ref, bias_ref, c0_ref, n0_ref, m0_ref,
                h_ref, c_ref, n_ref, m_ref):
    @pl.when(pl.program_id(1) == 0)
    def _():
        c_ref[...] = c0_ref[...]
        n_ref[...] = n0_ref[...]
        m_ref[...] = m0_ref[...]

    gc = gc_ref[0, 0]
    gr = gr_ref[0, 0]

    def head(h):
        hs = slice(h * DH, (h + 1) * DH)
        bi = bias_ref[0, :, h:h + 1]
        bf = bias_ref[0, :, N_HEADS + h:N_HEADS + h + 1]
        li_c = gc[:, h:h + 1] + bi
        lf_c = _log_sigmoid(gc[:, N_HEADS + h:N_HEADS + h + 1] + bf)
        li_r = gr[h:h + 1, :] + bi
        lf_r = _log_sigmoid(gr[N_HEADS + h:N_HEADS + h + 1, :] + bf)
        return _mlstm_block(d, q_ref[0, :, hs], k_ref[0, :, hs] * DH ** -0.5, v_ref[0, :, hs],
                            li_c, lf_c, li_r, lf_r, c_ref[0, h], n_ref[0, h], m_ref[0, h][:, 0:1])

    for h, (hv, c_new, n_new, m_new) in enumerate(_lockstep(head(h) for h in range(N_HEADS))):
        h_ref[0, :, h * DH:(h + 1) * DH] = hv
        c_ref[0, h] = c_new
        n_ref[0, h] = n_new
        m_ref[0, h] = jnp.broadcast_to(m_new, (1, DH))


def _gdn_block(d, q, k, v, bt_c, la_c, la_r, s_mat):
    tb = q.shape[0]
    nc = tb // CHUNK
    sh = CHUNK.bit_length() - 1
    ii, jj, same, vis, vis_t = _block_masks(d, tb)
    g_c = jnp.sum(vis.astype(F32) * la_r, axis=1, keepdims=True)
    g_r = jnp.sum(vis_t.astype(F32) * la_c, axis=0, keepdims=True)
    gl_c = jnp.sum(same.astype(F32) * la_r, axis=1, keepdims=True)
    decay = jnp.exp(jnp.where(vis, g_c - g_r, -jnp.inf))
    kb = k * bt_c
    yield
    both = _dot_nt(jnp.concatenate([kb, q], axis=0), k)
    a_mat = jnp.where(ii == jj, 0.0, both[:tb] * decay)
    attn = both[tb:] * decay
    yield
    off = lambda lvl: jnp.logical_and((ii >> (lvl + 1)) == (jj >> (lvl + 1)), (ii >> lvl) != (jj >> lvl))
    t_mat = (ii == jj).astype(F32) - jnp.where(off(0), a_mat, 0.0)
    for lvl in range(1, sh):
        te = _dot(t_mat, jnp.where(off(lvl), a_mat, 0.0))
        yield
        t_mat = t_mat - _dot(te, t_mat)
        yield
    eg = jnp.exp(g_c)
    uw = _dot(t_mat, jnp.concatenate([v * bt_c, kb * eg], axis=1))
    u, w = uw[:, :DH], uw[:, DH:]
    kd = k * jnp.exp(gl_c - g_c)
    qg = q * eg
    yield
    v_new, o_state = [None] * nc, [None] * nc
    for c in _chunk_order(d, nc):
        sl = slice(c * CHUNK, (c + 1) * CHUNK)
        ws = _dot(jnp.concatenate([w[sl], qg[sl]], axis=0), s_mat)
        v_new[c] = u[sl] - ws[:CHUNK]
        o_state[c] = ws[CHUNK:]
        yield
        s_mat = s_mat * jnp.exp(gl_c[c * CHUNK:c * CHUNK + 1, :]) + _dot_tn(kd[sl], v_new[c])
        yield
    o = jnp.concatenate(o_state, axis=0) + _dot(attn, jnp.concatenate(v_new, axis=0))
    return o, s_mat


def _gdn_body(d, q_ref, k_ref, v_ref, gc_ref, gr_ref, dtb_ref, alog_ref, s0_ref, o_ref, s_ref):
    @pl.when(pl.program_id(1) == 0)
    def _():
        s_ref[...] = s0_ref[...]

    gc = gc_ref[0, 0]
    gr = gr_ref[0, 0]

    def head(h):
        hs = slice(h * DH, (h + 1) * DH)
        dtb = dtb_ref[0, :, h:h + 1]
        neg_a = -jnp.exp(alog_ref[0, :, h:h + 1])
        bt_c = _sigmoid(gc[:, h:h + 1])
        la_c = neg_a * _softplus(gc[:, N_HEADS + h:N_HEADS + h + 1] + dtb)
        la_r = neg_a * _softplus(gr[N_HEADS + h:N_HEADS + h + 1, :] + dtb)
        return _gdn_block(d, q_ref[0, :, hs], k_ref[0, :, hs], v_ref[0, :, hs], bt_c, la_c, la_r, s_ref[0, h])

    for h, (ov, s_new) in enumerate(_lockstep(head(h) for h in range(N_HEADS))):
        o_ref[0, :, h * DH:(h + 1) * DH] = ov
        s_ref[0, h] = s_new


def _gate_views(graw):
    bsz, t, _ = graw.shape
    gcol = graw.reshape(bsz, t, 2, 2, N_HEADS).transpose(0, 3, 1, 2, 4).reshape(bsz, 2, t, 8)
    return gcol, gcol.transpose(0, 1, 3, 2)


def _scan_specs(d, t, col0):
    tb = min(SCAN_TB, t)
    nb = t // tb
    blk = (lambda i: i) if d == 0 else (lambda i: nb - 1 - i)
    qkv = [pl.BlockSpec((1, tb, MIX_W), lambda b, i, j=j: (b, blk(i), col0 + j)) for j in range(3)]
    gates = [pl.BlockSpec((1, 1, tb, 8), lambda b, i: (b, d, blk(i), 0)),
             pl.BlockSpec((1, 1, 8, tb), lambda b, i: (b, d, 0, blk(i)))]
    par = pl.BlockSpec((1, 1, 8), lambda b, i: (d, 0, 0))
    out = pl.BlockSpec((1, tb, MIX_W), lambda b, i: (b, blk(i), 0))
    return nb, qkv, gates, par, out


def _state_spec(*tail):
    return pl.BlockSpec((1, N_HEADS) + tail, lambda b, i: (b, 0) + (0,) * len(tail))


def mlstm_call(d, z, bias, state):
    bsz, t, _ = z.shape
    nb, qkv, gates, par, out = _scan_specs(d, t, E_QKV // MIX_W)
    gcol, grow = _gate_views(z[..., E_GATE:E_GATE + 16])
    st_specs = [_state_spec(DH, DH), _state_spec(1, DH), _state_spec(1, DH)]
    st_shapes = [jax.ShapeDtypeStruct((bsz, N_HEADS, DH, DH), F32),
                 jax.ShapeDtypeStruct((bsz, N_HEADS, 1, DH), F32),
                 jax.ShapeDtypeStruct((bsz, N_HEADS, 1, DH), F32)]
    res = pl.pallas_call(
        functools.partial(_mlstm_body, d),
        grid=(bsz, nb),
        in_specs=qkv + gates + [par] + st_specs,
        out_specs=[out] + st_specs,
        out_shape=[jax.ShapeDtypeStruct((bsz, t, MIX_W), F32)] + st_shapes,
        compiler_params=_cparams(("parallel", "arbitrary")),
        name="mlstm_scan_fwd" if d == 0 else "mlstm_scan_bwd",
    )(z, z, z, gcol, grow, bias, *state)
    return res[0], tuple(res[1:])


def gdn_call(d, qkv_arr, graw, dtb, alog, state):
    bsz, t, _ = qkv_arr.shape
    nb, qkv, gates, par, out = _scan_specs(d, t, 0)
    gcol, grow = _gate_views(graw)
    st_spec = _state_spec(DH, DH)
    return pl.pallas_call(
        functools.partial(_gdn_body, d),
        grid=(bsz, nb),
        in_specs=qkv + gates + [par, par, st_spec],
        out_specs=[out, st_spec],
        out_shape=[jax.ShapeDtypeStruct((bsz, t, MIX_W), F32),
                   jax.ShapeDtypeStruct((bsz, N_HEADS, DH, DH), F32)],
        compiler_params=_cparams(("parallel", "arbitrary")),
        name="gdn_scan_fwd" if d == 0 else "gdn_scan_bwd",
    )(qkv_arr, qkv_arr, qkv_arr, gcol, grow, dtb, alog, state)


def _outproj_body(act, hf_ref, hb_ref, gate_ref, y_ref, g_ref, w_ref, x_ref, mg_ref, o_ref):
    hs = hf_ref[0] + hb_ref[0]
    hn = jnp.concatenate([_rms(hs[:, h * DH:(h + 1) * DH]) for h in range(N_HEADS)], axis=-1) * g_ref[...]
    gate = gate_ref[0]
    hn = hn * (_sigmoid(gate) if act == "sigmoid" else _silu(gate))
    out = _dot(hn, w_ref[0:MIX_W, :]) + _dot(y_ref[0], w_ref[MIX_W:, :])
    o_ref[0] = x_ref[0] + mg_ref[0] * out


def outproj_call(act, hf, hb, z, gate_block, y, g, w, x, mgate):
    bn, t, d = x.shape
    tm = min(512, t)
    return pl.pallas_call(
        functools.partial(_outproj_body, act),
        grid=(bn, t // tm),
        in_specs=[pl.BlockSpec((1, tm, MIX_W), lambda b, i: (b, i, 0)),
                  pl.BlockSpec((1, tm, MIX_W), lambda b, i: (b, i, 0)),
                  pl.BlockSpec((1, tm, MIX_W), lambda b, i: (b, i, gate_block)),
                  pl.BlockSpec((1, tm, MIX_W), lambda b, i: (b, i, 0)),
                  pl.BlockSpec((1, MIX_W), lambda b, i: (0, 0)),
                  pl.BlockSpec((2 * MIX_W, d), lambda b, i: (0, 0)),
                  pl.BlockSpec((1, tm, d), lambda b, i: (b, i, 0)),
                  pl.BlockSpec((1, 1, d), lambda b, i: (b, 0, 0))],
        out_specs=pl.BlockSpec((1, tm, d), lambda b, i: (b, i, 0)),
        out_shape=jax.ShapeDtypeStruct((bn, t, d), F32),
        compiler_params=_cparams(("parallel", "parallel")),
        name="outproj_" + act,
    )(hf, hb, z, y, g, w, x, mgate)


def _ffn_body(x_ref, xp_ref, xn_ref, sh_ref, sc_ref, mg_ref, wu_ref, cw_ref, wd_ref, o_ref):
    i, nt = pl.program_id(1), pl.num_programs(1)
    sc, sh = 1.0 + sc_ref[0], sh_ref[0]
    x = x_ref[0]
    tm = x.shape[0]
    mod = lambda v: _rms(v) * sc + sh
    hp = mod(xp_ref[0]) * (i > 0).astype(F32)
    hn = mod(xn_ref[0]) * (i < nt - 1).astype(F32)
    h = jnp.concatenate([hp, mod(x), hn], axis=0).astype(BF16)
    rows = tm + 16

    cols = lambda j, half: slice(half * D_FF + j * FF_TILE, half * D_FF + (j + 1) * FF_TILE)
    up = lambda j: tuple(jnp.dot(h, wu_ref[:, cols(j, half)], preferred_element_type=F32) for half in range(2))

    def conv(u, cw):
        y = pltpu.roll(u, 1, 0) * cw[0:1, :] + u * cw[1:2, :] + pltpu.roll(u, rows - 1, 0) * cw[2:3, :] + cw[3:4, :]
        return y[8:8 + tm]

    acc = jnp.zeros(x.shape, F32)
    u_next = up(0)
    for j in range(N_FF_TILES):
        ua, ub = u_next
        if j + 1 < N_FF_TILES:
            u_next = up(j + 1)
        a, b = conv(ua, cw_ref[:, cols(j, 0)]), conv(ub, cw_ref[:, cols(j, 1)])
        acc = acc + _dot(_silu(a) * b, wd_ref[j * FF_TILE:(j + 1) * FF_TILE, :])
    o_ref[0] = x + mg_ref[0] * acc


def ffn_call(x, shift, scale, mgate, w_up, conv_w, conv_b, w_down):
    bn, t, d = x.shape
    tm = min(512, t)
    r8 = tm // 8
    wu, wd = w_up.astype(BF16), w_down.astype(BF16)
    cw = jnp.concatenate([conv_w, conv_b[None]], axis=0)
    full = lambda a: pl.BlockSpec(a.shape, lambda b, i: (0,) * a.ndim)
    vec = pl.BlockSpec((1, 1, d), lambda b, i: (b, 0, 0))
    return pl.pallas_call(
        _ffn_body,
        grid=(bn, t // tm),
        in_specs=[pl.BlockSpec((1, tm, d), lambda b, i: (b, i, 0)),
                  pl.BlockSpec((1, 8, d), lambda b, i: (b, jnp.maximum(i * r8 - 1, 0), 0)),
                  pl.BlockSpec((1, 8, d), lambda b, i: (b, jnp.minimum((i + 1) * r8, t // 8 - 1), 0)),
                  vec, vec, vec, full(wu), full(cw), full(wd)],
        out_specs=pl.BlockSpec((1, tm, d), lambda b, i: (b, i, 0)),
        out_shape=jax.ShapeDtypeStruct((bn, t, d), F32),
        compiler_params=_cparams(("parallel", "parallel")),
        name="conv_ffn",
    )(x, x, x, shift, scale, mgate, wu, cw, wd)


def _filter_hidden(lf, n, w1t_ref, w1c_ref, w1s_ref, b1_ref, w2_ref, b2_ref, w3_ref, b3_ref, fr_ref, band_ref):
    t = jnp.where(n < lf, n, 2 * lf - n).astype(F32)
    t_norm = t / (lf - 1)
    ang = (2.0 * math.pi * t / lf) * band_ref[...]
    fr = fr_ref[...]
    pre = t_norm * w1t_ref[...] + _dot_hi(jnp.cos(ang), w1c_ref[...]) - _dot_hi(jnp.sin(ang), w1s_ref[...])
    hdn = jnp.sin(fr * (pre + b1_ref[...]))
    hdn = jnp.sin(fr * (_dot_hi(hdn, w2_ref[...]) + b2_ref[...]))
    hdn = jnp.sin(fr * (_dot_hi(hdn, w3_ref[...]) + b3_ref[...]))
    return hdn, t_norm


def _filter_values(lf, n, hdn, t_norm, w4, dl):
    hval = _dot(hdn, w4) * jnp.exp(-t_norm * jnp.abs(dl))
    return jnp.where(n == lf, 0.0, hval)


def _hy_filter_body(lf, *refs):
    mlp_refs, (w4_ref, dl_ref, o_ref, l1_ref) = refs[:-4], refs[-4:]
    r = pl.program_id(0)
    tr = o_ref.shape[0]
    n = (r * tr + lax.broadcasted_iota(jnp.int32, (tr, 1), 0))
    hdn, t_norm = _filter_hidden(lf, n, *mlp_refs)
    hval = _filter_values(lf, n, hdn, t_norm, w4_ref[0], dl_ref[0])
    o_ref[...] = hval

    @pl.when(r == 0)
    def _():
        l1_ref[...] = jnp.zeros_like(l1_ref)

    l1_ref[...] += jnp.sum(jnp.abs(hval), axis=0, keepdims=True)


def _hy_filter_lead_body(lf, *refs):
    mlp_refs, (w4_ref, dl_ref, f_ref, o_ref, l1_ref) = refs[:-5], refs[-5:]
    jb = pl.program_id(0)
    half = FFT_N1 // 2
    per_dir = LEAD_ROWS * half
    sh_half, sh_dir = half.bit_length() - 1, per_dir.bit_length() - 1
    ncol = l1_ref.shape[-1]
    r = lax.broadcasted_iota(jnp.int32, (2 * per_dir, 1), 0)
    dd, rem = r >> sh_dir, r & (per_dir - 1)
    n = FFT_N1 * ((dd << sh_half) + (rem & (half - 1))) + LEAD_ROWS * jb + (rem >> sh_half)
    hdn, t_norm = _filter_hidden(lf, n, *mlp_refs)
    vals = []
    for d in range(2):
        rows = slice(d * per_dir, (d + 1) * per_dir)
        vals.append(_filter_values(lf, n[rows], hdn[rows], t_norm[rows], w4_ref[d], dl_ref[d]))

    @pl.when(jb == 0)
    def _():
        l1_ref[...] = jnp.zeros_like(l1_ref)

    l1_ref[...] += sum(jnp.sum(jnp.abs(v), axis=0, keepdims=True) for v in vals)
    for j in range(LEAD_ROWS):
        slab = jnp.concatenate([v[j * half:(j + 1) * half] for v in vals], axis=0)
        o_ref[:, j * ncol:(j + 1) * ncol] = _dot(f_ref[...], slab).astype(o_ref.dtype)


def _filter_args(f_w1, f_b1, f_w2, f_b2, f_w3, f_b3, f_w4, f_freq):
    ncol = 2 * HYENA_W
    pad = lambda a: jnp.concatenate([a, jnp.zeros((DH - a.shape[0], a.shape[1]), F32)], axis=0)
    w4 = f_w4.reshape(HYENA_FFN, 2, 2, HYENA_W).transpose(2, 0, 1, 3).reshape(2, HYENA_FFN, ncol)
    deltas = jnp.linspace(math.log(HYENA_TARGET) / HYENA_SLOW_DECAY, math.log(HYENA_TARGET) / HYENA_FAST_DECAY,
                          HYENA_NFILT, dtype=F32)
    deltas = deltas.reshape(2, 2, HYENA_W).transpose(1, 0, 2).reshape(2, 1, ncol)
    bands = jnp.linspace(1e-4, HYENA_BANDS - 1, HYENA_BANDS, dtype=F32)
    bands = jnp.concatenate([bands, jnp.zeros((DH - HYENA_BANDS,), F32)]).reshape(1, DH)
    row = lambda a: a.reshape(1, -1)
    mlp = (f_w1[0:1], pad(f_w1[1:1 + HYENA_BANDS]), pad(f_w1[1 + HYENA_BANDS:]), row(f_b1), f_w2, row(f_b2),
           f_w3, row(f_b3), row(f_freq), bands)
    return mlp, w4, deltas


def hy_filter_call(lf, *filt):
    tr = min(512, lf)
    nt = 2 * lf // tr
    half = lf // tr
    ncol = 2 * HYENA_W
    mlp, w4, deltas = _filter_args(*filt)
    full = lambda a: pl.BlockSpec(a.shape, lambda r: (0,) * a.ndim)
    by_dir = lambda a: pl.BlockSpec((1,) + a.shape[1:], lambda r: (r // half, 0, 0))
    return pl.pallas_call(
        functools.partial(_hy_filter_body, lf),
        grid=(nt,),
        in_specs=[full(a) for a in mlp] + [by_dir(w4), by_dir(deltas)],
        out_specs=[pl.BlockSpec((tr, ncol), lambda r: (r, 0)), pl.BlockSpec((1, ncol), lambda r: (0, 0))],
        out_shape=[jax.ShapeDtypeStruct((2 * lf, ncol), F32), jax.ShapeDtypeStruct((1, ncol), F32)],
        compiler_params=_cparams(("arbitrary",)),
        name="hyena_filter",
    )(*mlp, w4, deltas)


def hy_filter_lead_call(lf, f_flt, *filt):
    ncol = 2 * HYENA_W
    mlp, w4, deltas = _filter_args(*filt)
    full = lambda a: pl.BlockSpec(a.shape, lambda j: (0,) * a.ndim)
    return pl.pallas_call(
        functools.partial(_hy_filter_lead_body, lf),
        grid=(FFT_N1 // LEAD_ROWS,),
        in_specs=[full(a) for a in mlp + (w4, deltas, f_flt)],
        out_specs=[pl.BlockSpec((f_flt.shape[0], LEAD_ROWS * ncol), lambda j: (0, j)),
                   pl.BlockSpec((1, ncol), lambda j: (0, 0))],
        out_shape=[jax.ShapeDtypeStruct((f_flt.shape[0], FFT_N1 * ncol), BF16), jax.ShapeDtypeStruct((1, ncol), F32)],
        compiler_params=_cparams(("arbitrary",)),
        name="hyena_filter_lead",
    )(*mlp, w4, deltas, f_flt)


def _dft_consts(lf):
    n1 = FFT_N1
    n = 2 * lf
    assert n == n1 * n1
    idx = np.arange(n1)
    f = np.exp(-2j * np.pi * np.outer(idx, idx) / n1)
    tw = np.exp(-2j * np.pi * np.outer(idx, idx) / n)
    fh = f[:, :n1 // 2]
    blk = lambda a: np.block([[a.real, -a.imag], [a.imag, a.real]])
    f_sig = blk(fh)
    f_flt = np.concatenate([f.real, f.imag], axis=0)
    c_inv = blk(np.conj(f)[:n1 // 2, :] / n)
    c = lambda a: jnp.asarray(a, dtype=F32)
    cb = lambda a: jnp.asarray(a, dtype=F32).astype(BF16)
    return dict(f_sig=cb(f_sig), f_flt=cb(f_flt), c_inv=cb(c_inv), fr=c(f.real), fi=c(f.imag),
                twr=c(tw.real.reshape(n1, 1, n1)), twi=c(tw.imag.reshape(n1, 1, n1)))


LEAD_ROWS = 8


def _lead_mm_body(m_ref, x_ref, o_ref):
    w = x_ref.shape[-1]
    for j in range(LEAD_ROWS):
        o_ref[:, j * w:(j + 1) * w] = _dot(m_ref[...], x_ref[:, j, :]).astype(o_ref.dtype)


def lead_mm_call(mat, x3, group, w, name):
    r, k = mat.shape
    n1 = x3.shape[1]
    return pl.pallas_call(
        _lead_mm_body,
        grid=(n1 // LEAD_ROWS,),
        in_specs=[pl.BlockSpec((r, k), lambda j: (0, 0)), pl.BlockSpec((k, LEAD_ROWS, w), lambda j: (0, j, group))],
        out_specs=pl.BlockSpec((r, LEAD_ROWS * w), lambda j: (0, j)),
        out_shape=jax.ShapeDtypeStruct((r, n1 * w), BF16),
        compiler_params=_cparams(("parallel",)),
        name=name,
    )(mat, x3)


def _stage2_mat(fr, fi, twr, twi):
    hr = fr * twr - fi * twi
    hi = fr * twi + fi * twr
    return hr, hi


def _blk(ar, ai):
    return jnp.concatenate([jnp.concatenate([ar, -ai], axis=1), jnp.concatenate([ai, ar], axis=1)], axis=0)


SPEC_K1 = 2


def _spec_filter_body(a_ref, l1_ref, fr_ref, fi_ref, twr_ref, twi_ref, o_ref):
    inv_l1 = 1.0 / l1_ref[...]
    for kk in range(SPEC_K1):
        hr, hi = _stage2_mat(fr_ref[...], fi_ref[...], twr_ref[kk], twi_ref[kk])
        n1 = hr.shape[0]
        a = a_ref[:, kk].reshape(2 * n1, a_ref.shape[-1])
        o_ref[kk] = (_dot(_blk(hr, hi), a) * inv_l1).astype(o_ref.dtype)


def _spec_conv_body(a_ref, k_ref, fr_ref, fi_ref, twr_ref, twi_ref, o_ref):
    for kk in range(SPEC_K1):
        hr, hi = _stage2_mat(fr_ref[...], fi_ref[...], twr_ref[kk], twi_ref[kk])
        n1 = hr.shape[0]
        x = _dot(_blk(hr, hi), a_ref[:, kk].reshape(2 * n1, a_ref.shape[-1]))
        xr, xi = x[:n1], x[n1:]
        kr, ki = k_ref[kk, :n1].astype(F32), k_ref[kk, n1:].astype(F32)
        y = jnp.concatenate([xr * kr - xi * ki, xr * ki + xi * kr], axis=0)
        o_ref[kk] = _dot(_blk(hr.T, -(hi.T)), y).astype(o_ref.dtype)


def spec_call(a2d, kspec, order, cst, l1=None):
    n1 = FFT_N1
    c = a2d.shape[1] // n1
    ct = 512 if kspec is not None else min(1024, c)
    a4 = a2d.reshape(2, n1, n1, c)
    sq = pl.BlockSpec((n1, n1), lambda k, j: (0, 0))
    tw = pl.BlockSpec((SPEC_K1, 1, n1), lambda k, j: (k, 0, 0))
    a_spec = pl.BlockSpec((2, SPEC_K1, n1, ct), lambda k, j: (0, k, 0, j))
    o_spec = pl.BlockSpec((SPEC_K1, 2 * n1, ct), lambda k, j: (k, 0, j))
    common = dict(grid=(n1 // SPEC_K1, c // ct), out_specs=o_spec,
                  out_shape=jax.ShapeDtypeStruct((n1, 2 * n1, c), BF16),
                  compiler_params=_cparams(("parallel", "parallel")))
    if kspec is None:
        l1_spec = pl.BlockSpec((1, ct), lambda k, j: (0, j))
        return pl.pallas_call(_spec_filter_body, in_specs=[a_spec, l1_spec, sq, sq, tw, tw], name="hyena_spec_filter",
                              **common)(a4, l1, cst["fr"], cst["fi"], cst["twr"], cst["twi"])
    k_spec = pl.BlockSpec((SPEC_K1, 2 * n1, ct), lambda k, j: (k, 0, order * (c // ct) + j))
    return pl.pallas_call(_spec_conv_body, in_specs=[a_spec, k_spec, sq, sq, tw, tw], name="hyena_spec_conv",
                          **common)(a4, kspec, cst["fr"], cst["fi"], cst["twr"], cst["twi"])


def _inv_gate_body(m_ref, br_ref, bi_ref, v_ref, x_ref, d_ref, o_ref):
    w = o_ref.shape[-1]
    y = _dot(m_ref[...], jnp.concatenate([br_ref[...], bi_ref[...]], axis=0))
    for j in range(LEAD_ROWS):
        o_ref[:, j, :] = x_ref[:, j, :] * (y[:, j * w:(j + 1) * w] + v_ref[:, j, :] * d_ref[...])


def inv_gate_call(b3, v3, gv, x3, gx, d_row, cst):
    n1 = FFT_N1
    w = HYENA_W
    nb = n1 // LEAD_ROWS
    b2d = b3.reshape(n1, 2 * n1 * w)
    tn = LEAD_ROWS * w
    rows3 = lambda g: pl.BlockSpec((n1, LEAD_ROWS, w), lambda j: (0, j, g))
    return pl.pallas_call(
        _inv_gate_body,
        grid=(nb,),
        in_specs=[pl.BlockSpec((n1, 2 * n1), lambda j: (0, 0)), pl.BlockSpec((n1, tn), lambda j: (0, j)),
                  pl.BlockSpec((n1, tn), lambda j: (0, nb + j)), rows3(gv), rows3(gx),
                  pl.BlockSpec((1, w), lambda j: (0, 0))],
        out_specs=rows3(0),
        out_shape=jax.ShapeDtypeStruct((n1, n1, w), F32),
        compiler_params=_cparams(("parallel",)),
        name="hyena_inv_gate",
    )(cst["c_inv"], b2d, b2d, v3, x3, d_row)


def hyena_long(u, filt, d_skip):
    bsz, lf, _ = u.shape
    assert bsz == 2
    n1 = FFT_N1
    w = HYENA_W
    cst = _dft_consts(lf)
    u3 = u.reshape(bsz * n1 // 2, n1, 3 * w)
    a_flt, l1 = hy_filter_lead_call(lf, cst["f_flt"], *filt)
    kspec = spec_call(a_flt, None, 0, cst, l1)
    y3, gy = u3, 0
    for o in range(2):
        a = lead_mm_call(cst["f_sig"], y3, gy, w, "hyena_lead_signal")
        b3 = spec_call(a, kspec, o, cst)
        y3, gy = inv_gate_call(b3, y3, gy, u3, o + 1, d_skip[o:o + 1], cst), 0
    return y3.reshape(bsz, lf, w)


def _hyena_small_body(u_ref, kf_ref, l1_ref, fs_ref, ff_ref, ci_ref, d_ref, o_ref):
    w = HYENA_W
    nk = ff_ref.shape[0] // 2
    kspec = _dot(ff_ref[...], kf_ref[...]) * (1.0 / l1_ref[...])
    ucat = jnp.concatenate([u_ref[0], u_ref[1]], axis=0)
    y = ucat[:, :w]
    for o in range(2):
        x = _dot(fs_ref[...], y)
        xr, xi = x[:nk], x[nk:]
        kr, ki = kspec[:nk, o * w:(o + 1) * w], kspec[nk:, o * w:(o + 1) * w]
        prod = jnp.concatenate([xr * kr - xi * ki, xr * ki + xi * kr], axis=0)
        conv = _dot(ci_ref[...], prod)
        y = ucat[:, (o + 1) * w:(o + 2) * w] * (conv + y * d_ref[o:o + 1, :])
    lf = u_ref.shape[1]
    o_ref[0] = y[:lf]
    o_ref[1] = y[lf:]


def hyena_small(u, kf, l1, d_skip):
    bsz, lf, _ = u.shape
    assert bsz == 2
    n = 2 * lf
    idx = np.arange(n)
    f = np.exp(-2j * np.pi * np.outer(idx, idx) / n)
    blk = lambda a: np.block([[a.real, -a.imag], [a.imag, a.real]])
    cb = lambda a: jnp.asarray(a, F32).astype(BF16)
    fs = cb(blk(f[:, :lf]))
    ff = cb(np.concatenate([f.real, f.imag], axis=0))
    ci = cb(blk(np.conj(f)[:lf, :] / n))
    return pl.pallas_call(
        _hyena_small_body,
        out_shape=jax.ShapeDtypeStruct((bsz, lf, HYENA_W), F32),
        compiler_params=pltpu.CompilerParams(vmem_limit_bytes=VMEM_LIMIT),
        name="hyena_small",
    )(u, kf, l1, fs, ff, ci, d_skip)


def _rope(x, cos_ref, sa_ref, sb_ref):
    w = x.shape[-1]
    half = MLA_ROPE // 2
    return x * cos_ref[...] + pltpu.roll(x, w - half, 1) * sa_ref[...] + pltpu.roll(x, half, 1) * sb_ref[...]


def _kv_body(rope, lkv_ref, lkr_ref, g_ref, w_ref, kg_ref, cos_ref, sa_ref, sb_ref, k_ref, v_ref):
    kv = _dot(_rms(lkv_ref[0]) * g_ref[...], w_ref[...])
    kr = lkr_ref[0]
    for h in range(N_HEADS):
        base = h * (MLA_NOPE + DH)
        k = jnp.concatenate([kv[:, base:base + MLA_NOPE], kr], axis=-1)
        k = _rms(k, MLA_QK) * kg_ref[...]
        if rope:
            k = _rope(k, cos_ref, sa_ref, sb_ref)
        k_ref[0, h] = k.astype(BF16)
        v = kv[:, base + MLA_NOPE:base + MLA_NOPE + DH]
        v_ref[0, h] = jnp.concatenate([v, jnp.ones_like(v)], axis=-1).astype(BF16)


def kv_call(z, rope, g, w, kg, tabs):
    bn, t, _ = z.shape
    tm = min(512, t)
    tab = pl.BlockSpec((tm, MLA_QK_PAD), lambda b, i: (i, 0))
    return pl.pallas_call(
        functools.partial(_kv_body, rope),
        grid=(bn, t // tm),
        in_specs=[pl.BlockSpec((1, tm, MLA_KV_RANK), lambda b, i: (b, i, O_LKV // MLA_KV_RANK)),
                  pl.BlockSpec((1, tm, DH), lambda b, i: (b, i, O_LKR // DH)),
                  pl.BlockSpec((1, MLA_KV_RANK), lambda b, i: (0, 0)),
                  pl.BlockSpec(w.shape, lambda b, i: (0, 0)),
                  pl.BlockSpec((1, MLA_QK_PAD), lambda b, i: (0, 0)), tab, tab, tab],
        out_specs=[pl.BlockSpec((1, N_HEADS, tm, MLA_QK_PAD), lambda b, i: (b, 0, i, 0)),
                   pl.BlockSpec((1, N_HEADS, tm, 2 * DH), lambda b, i: (b, 0, i, 0))],
        out_shape=[jax.ShapeDtypeStruct((bn, N_HEADS, t, MLA_QK_PAD), BF16),
                   jax.ShapeDtypeStruct((bn, N_HEADS, t, 2 * DH), BF16)],
        compiler_params=_cparams(("parallel", "parallel")),
        name="mla_kv_rope" if rope else "mla_kv",
    )(z, z, g, w, kg, *tabs)


def _q_body(lq_ref, g_ref, w_ref, qg_ref, cos_ref, sa_ref, sb_ref, q_ref):
    q = _dot(_rms(lq_ref[0]) * g_ref[...], w_ref[...])
    for h in range(N_HEADS):
        qh = _rms(q[:, h * MLA_QK_PAD:(h + 1) * MLA_QK_PAD], MLA_QK) * qg_ref[...]
        qh = _rope(qh, cos_ref, sa_ref, sb_ref) * (MLA_QK ** -0.5 * LOG2E)
        q_ref[0, h] = qh.astype(BF16)


def q_call(z, g, w, qg, tabs):
    bn, t, _ = z.shape
    tm = min(512, t)
    tab = pl.BlockSpec((tm, MLA_QK_PAD), lambda b, i: (i, 0))
    return pl.pallas_call(
        _q_body,
        grid=(bn, t // tm),
        in_specs=[pl.BlockSpec((1, tm, MLA_Q_RANK), lambda b, i: (b, i, O_LQ // MLA_Q_RANK)),
                  pl.BlockSpec((1, MLA_Q_RANK), lambda b, i: (0, 0)),
                  pl.BlockSpec(w.shape, lambda b, i: (0, 0)),
                  pl.BlockSpec((1, MLA_QK_PAD), lambda b, i: (0, 0)), tab, tab, tab],
        out_specs=pl.BlockSpec((1, N_HEADS, tm, MLA_QK_PAD), lambda b, i: (b, 0, i, 0)),
        out_shape=jax.ShapeDtypeStruct((bn, N_HEADS, t, MLA_QK_PAD), BF16),
        compiler_params=_cparams(("parallel", "parallel")),
        name="mla_q",
    )(z, g, w, qg, *tabs)


def _attn_body(q_ref, kc_ref, vc_ref, kx_ref, vx_ref, o_ref):
    q = q_ref[0, 0]
    tiles = [(kc_ref, vc_ref, 0, kc_ref.shape[2])]
    tk = min(ATTN_TK, kx_ref.shape[2])
    tiles += [(kx_ref, vx_ref, s0, tk) for s0 in range(0, kx_ref.shape[2], tk)]
    scores = lambda t: lax.dot_general(q, t[0][0, 0, t[2]:t[2] + t[3], :], (((1,), (1,)), ((), ())),
                                       preferred_element_type=F32)
    m = jnp.full((q.shape[0], 1), -jnp.inf, F32)
    acc = jnp.zeros((q.shape[0], 2 * DH), F32)
    s_next = scores(tiles[0])
    for idx, t in enumerate(tiles):
        s = s_next
        if idx + 1 < len(tiles):
            s_next = scores(tiles[idx + 1])
        m_new = jnp.maximum(m, jnp.max(s, axis=1, keepdims=True))
        p = jnp.exp2(s - m_new).astype(BF16)
        acc = jnp.exp2(m - m_new) * acc + jnp.dot(p, t[1][0, 0, t[2]:t[2] + t[3], :], preferred_element_type=F32)
        m = m_new
    o_ref[0] = acc[:, :DH] / acc[:, DH:]


def attn_call(q, kc, vc, kx, vx):
    bn, nh, t, dq = q.shape
    tc = kc.shape[2]
    tq = ATTN_TQ
    whole = lambda n, w: pl.BlockSpec((1, 1, n, w), lambda b, h, i: (b, h, 0, 0))
    return pl.pallas_call(
        _attn_body,
        grid=(bn, nh, t // tq),
        in_specs=[pl.BlockSpec((1, 1, tq, dq), lambda b, h, i: (b, h, i, 0)),
                  whole(tc, dq), whole(tc, 2 * DH), whole(t, dq), whole(t, 2 * DH)],
        out_specs=pl.BlockSpec((1, tq, DH), lambda b, h, i: (b, i, h)),
        out_shape=jax.ShapeDtypeStruct((bn, t, nh * DH), F32),
        compiler_params=_cparams(("parallel", "parallel", "parallel")),
        name="mla_attention",
    )(q, kc, vc, kx, vx)


def rope_tables(seq):
    rows = seq // GRID_W
    row = jnp.repeat(jnp.arange(rows), GRID_W).astype(F32)
    col = jnp.tile(jnp.arange(GRID_W), rows).astype(F32)
    n_freq = MLA_ROPE // 4
    inv = ROPE_THETA ** (-jnp.arange(n_freq, dtype=F32) / n_freq)
    ang = jnp.concatenate([row[:, None] * inv, col[:, None] * inv], axis=-1)
    cos, sin = jnp.cos(ang), jnp.sin(ang)
    half = MLA_ROPE // 2
    ones = jnp.ones((seq, MLA_NOPE), F32)
    zn = jnp.zeros((seq, MLA_NOPE), F32)
    zh = jnp.zeros((seq, half), F32)
    zp = jnp.zeros((seq, MLA_QK_PAD - MLA_QK), F32)
    cos_t = jnp.concatenate([ones, cos, cos, zp], axis=-1)
    sin_a = jnp.concatenate([zn, -sin, zh, zp], axis=-1)
    sin_b = jnp.concatenate([zn, zh, sin, zp], axis=-1)
    return cos_t, sin_a, sin_b


def _padcols(w, n):
    return jnp.concatenate([w, jnp.zeros((w.shape[0], n - w.shape[1]), w.dtype)], axis=1)


def pack_even(w):
    q, k, v, o = (w[:, j * MIX_W:(j + 1) * MIX_W] for j in range(4))
    gates = w[:, 4 * MIX_W:4 * MIX_W + 16]
    hy = w[:, 4 * MIX_W + 16:]
    return jnp.concatenate([q, k, v, hy, o, _padcols(gates, DH)], axis=1).astype(BF16)


def pack_odd(w):
    q, k, v, gg = (w[:, j * MIX_W:(j + 1) * MIX_W] for j in range(4))
    c0 = 4 * MIX_W
    gates = w[:, c0:c0 + 16]
    lq = w[:, c0 + 16:c0 + 16 + MLA_Q_RANK]
    lkv = w[:, c0 + 16 + MLA_Q_RANK:c0 + 16 + MLA_Q_RANK + MLA_KV_RANK]
    lkr = w[:, c0 + 16 + MLA_Q_RANK + MLA_KV_RANK:]
    return jnp.concatenate([q, k, v, gg, lkv, lq, _padcols(lkr, DH), _padcols(gates, DH)], axis=1).astype(BF16)


def kernel(x, c, ctx, c_ctx, mod_w, mod_b, a_w_in, a_i_bias, a_f_bias, a_norm_g, b_conv_w, b_conv_b, b_f_w1, b_f_b1, b_f_w2, b_f_b2, b_f_w3, b_f_b3, b_f_w4, b_f_freq, b_d, ab_w_out, cd_w_in, c_conv_w, c_conv_b, c_a_log, c_dt_bias, c_norm_g, d_q_norm_g, d_w_q_up, d_kv_norm_g, d_w_kv_up, d_qn_g, d_kn_g, cd_w_out, ffn_w_up, ffn_conv_w, ffn_conv_b, ffn_w_down):
    bsz, seq, d = x.shape
    ctx_len = ctx.shape[1]
    cvec = jnp.concatenate([c, c_ctx[None], jnp.zeros((8 - bsz - 1, d), F32)], axis=0)
    mods = mods_call(cvec, mod_w, mod_b)
    mx = lambda l, j: mods[l, :bsz, None, j * d:(j + 1) * d]
    mc = lambda l, j: jnp.broadcast_to(mods[l, bsz, j * d:(j + 1) * d], (bsz, 1, d))

    def ffn(l, h, m):
        return ffn_call(h, m(l, 3), m(l, 4), m(l, 5), ffn_w_up[l], ffn_conv_w[l], ffn_conv_b[l], ffn_w_down[l])

    w_in = pack_even(a_w_in[0])
    zx = inproj_call(x, mx(0, 0), mx(0, 1), w_in)
    zc = inproj_call(ctx, mc(0, 0), mc(0, 1), w_in)
    bias = jnp.concatenate([a_i_bias[0].reshape(2, N_HEADS), a_f_bias[0].reshape(2, N_HEADS)], axis=-1)[:, None, :]
    zero_state = (jnp.zeros((bsz, N_HEADS, DH, DH), F32), jnp.zeros((bsz, N_HEADS, 1, DH), F32),
                  jnp.zeros((bsz, N_HEADS, 1, DH), F32))
    hc, hx = [], []
    for dd in range(2):
        h_ctx, st = mlstm_call(dd, zc, bias, zero_state)
        hc.append(h_ctx)
        hx.append(mlstm_call(dd, zx, bias, st)[0])
    filt = (b_f_w1[0], b_f_b1[0], b_f_w2[0], b_f_b2[0], b_f_w3[0], b_f_b3[0], b_f_w4[0], b_f_freq[0])
    ux = dwconv_call(zx, E_HY // (3 * HYENA_W), b_conv_w[0], b_conv_b[0], False)
    uc = dwconv_call(zc, E_HY // (3 * HYENA_W), b_conv_w[0], b_conv_b[0], False)
    yx = hyena_long(ux, filt, b_d[0])
    yc = hyena_small(uc, *hy_filter_call(ctx_len, *filt), b_d[0])
    w_out = ab_w_out[0].astype(BF16)
    g = a_norm_g[0].reshape(1, MIX_W)
    x = outproj_call("sigmoid", hx[0], hx[1], zx, E_MO // MIX_W, yx, g, w_out, x, mx(0, 2))
    ctx = outproj_call("sigmoid", hc[0], hc[1], zc, E_MO // MIX_W, yc, g, w_out, ctx, mc(0, 2))
    x = ffn(0, x, mx)
    ctx = ffn(0, ctx, mc)

    w_in = pack_odd(cd_w_in[0])
    zx = inproj_call(x, mx(1, 0), mx(1, 1), w_in)
    zc = inproj_call(ctx, mc(1, 0), mc(1, 1), w_in)
    qkv_x = dwconv_call(zx, 0, c_conv_w[0], c_conv_b[0], True)
    qkv_c = dwconv_call(zc, 0, c_conv_w[0], c_conv_b[0], True)
    zeros4 = jnp.zeros((2, 1, N_HEADS), F32)
    dtb = jnp.concatenate([c_dt_bias[0][:, None, :], zeros4], axis=-1)
    alog = jnp.concatenate([c_a_log[0][:, None, :], zeros4], axis=-1)
    ox = []
    for dd in range(2):
        _, s_ctx = gdn_call(dd, qkv_c, zc[..., O_GATE:O_GATE + 16], dtb, alog, zero_state[0])
        ox.append(gdn_call(dd, qkv_x, zx[..., O_GATE:O_GATE + 16], dtb, alog, s_ctx)[0])
    tabs = rope_tables(seq)
    kvg = d_kv_norm_g[0].reshape(1, -1)
    w_kv = d_w_kv_up[0].astype(BF16)
    kn_g = _padcols(d_kn_g[0].reshape(1, -1), MLA_QK_PAD)
    kc, vc = kv_call(zc, False, kvg, w_kv, kn_g, tuple(t[:ctx_len] for t in tabs))
    kx, vx = kv_call(zx, True, kvg, w_kv, kn_g, tabs)
    w_q = jnp.concatenate([_padcols(d_w_q_up[0][:, h * MLA_QK:(h + 1) * MLA_QK], MLA_QK_PAD)
                           for h in range(N_HEADS)], axis=1).astype(BF16)
    qx = q_call(zx, d_q_norm_g[0].reshape(1, -1), w_q, _padcols(d_qn_g[0].reshape(1, -1), MLA_QK_PAD), tabs)
    ax = attn_call(qx, kc, vc, kx, vx)
    g = jnp.tile(c_norm_g[0], N_HEADS).reshape(1, MIX_W)
    x = outproj_call("silu", ox[0], ox[1], zx, O_GG // MIX_W, ax, g, cd_w_out[0].astype(BF16), x, mx(1, 2))
    x = ffn(1, x, mx)
    return x
```

```python
import functools
import math

import numpy as np
import jax
import jax.numpy as jnp
from jax import lax
from jax.experimental import pallas as pl
from jax.experimental.pallas import tpu as pltpu

F32 = jnp.float32
BF16 = jnp.bfloat16
HI = lax.Precision.HIGHEST
LOG2E = math.log2(math.e)

D_MODEL = 1024
GRID_W = 64
EPS = 1e-6
CHUNK = 64
ROPE_THETA = 10000.0
N_HEADS = 4
DH = 128
MIX_W = N_HEADS * DH
HYENA_W = 512
HYENA_EMB = 33
HYENA_BANDS = 16
HYENA_FFN = 64
HYENA_NFILT = 4 * HYENA_W
HYENA_FAST_DECAY = 0.3
HYENA_SLOW_DECAY = 1.5
HYENA_TARGET = 1e-2
MLA_NOPE = 128
MLA_ROPE = 64
MLA_QK = 192
MLA_QK_PAD = 256
MLA_Q_RANK = 384
MLA_KV_RANK = 256
D_FF = 2816
FF_TILE = 256
N_FF_TILES = D_FF // FF_TILE
FFT_N1 = 128
SCAN_TB = 256
ATTN_TQ = 512
ATTN_TK = 2048
VMEM_LIMIT = 56 * 1024 * 1024

E_QKV, E_MO, E_GATE, E_N = 0, 1536, 2048, 2176
O_GG, O_LKV, O_LQ, O_LKR, O_GATE, O_N = 0, 512, 768, 1152, 1280, 1408


def _cparams(sem):
    return pltpu.CompilerParams(dimension_semantics=sem, vmem_limit_bytes=VMEM_LIMIT)


def _dot(a, b):
    return jnp.dot(a.astype(BF16), b.astype(BF16), preferred_element_type=F32)


def _dot_nt(a, b):
    return lax.dot_general(a.astype(BF16), b.astype(BF16), (((1,), (1,)), ((), ())), preferred_element_type=F32)


def _dot_tn(a, b):
    return lax.dot_general(a.astype(BF16), b.astype(BF16), (((0,), (0,)), ((), ())), preferred_element_type=F32)


def _dot_hi(a, b):
    return jnp.dot(a, b, precision=HI, preferred_element_type=F32)


def _rms(x, n=None):
    n = x.shape[-1] if n is None else n
    return x * lax.rsqrt(jnp.sum(x * x, axis=-1, keepdims=True) * (1.0 / n) + EPS)


def _sigmoid(x):
    return 0.5 + 0.5 * jnp.tanh(0.5 * x)


def _silu(x):
    return x * _sigmoid(x)


def _softplus(x):
    return jnp.maximum(x, 0.0) + jnp.log(1.0 + jnp.exp(-jnp.abs(x)))


def _log_sigmoid(x):
    return -_softplus(-x)


def _mods_body(c_ref, w_ref, b_ref, o_ref):
    o_ref[0] = _dot_hi(_silu(c_ref[...]), w_ref[0]) + b_ref[0]


def mods_call(cvec, mod_w, mod_b):
    depth, d, n = mod_w.shape
    tn = 1536
    return pl.pallas_call(
        _mods_body,
        grid=(depth, n // tn),
        in_specs=[pl.BlockSpec((8, d), lambda l, j: (0, 0)),
                  pl.BlockSpec((1, d, tn), lambda l, j: (l, 0, j)),
                  pl.BlockSpec((1, 1, tn), lambda l, j: (l, 0, j))],
        out_specs=pl.BlockSpec((1, 8, tn), lambda l, j: (l, 0, j)),
        out_shape=jax.ShapeDtypeStruct((depth, 8, n), F32),
        compiler_params=_cparams(("parallel", "parallel")),
        name="mods",
    )(cvec, mod_w, mod_b.reshape(depth, 1, n))


def _inproj_body(gdn, x_ref, xp_ref, xn_ref, sh_ref, sc_ref, wr_ref, wc_ref, cw_ref, cb_ref, z_ref, u_ref):
    i, nt = pl.program_id(1), pl.num_programs(1)
    sc, sh = 1.0 + sc_ref[0], sh_ref[0]
    tm = x_ref.shape[1]
    mod = lambda v: _rms(v) * sc + sh
    hp = mod(xp_ref[0]) * (i > 0).astype(F32)
    hn = mod(xn_ref[0]) * (i < nt - 1).astype(F32)
    h = jnp.concatenate([hp, mod(x_ref[0]), hn], axis=0).astype(BF16)
    rows = tm + 16
    z_ref[0] = jnp.dot(h[8:8 + tm], wr_ref[...], preferred_element_type=F32)
    c = jnp.dot(h, wc_ref[...], preferred_element_type=F32)
    y = pltpu.roll(c, 1, 0) * cw_ref[0:1, :] + c * cw_ref[1:2, :] + pltpu.roll(c, rows - 1, 0) * cw_ref[2:3, :]
    y = (y + cb_ref[...])[8:8 + tm]
    if not gdn:
        u_ref[0] = y
        return
    y = _silu(y)
    for hd in range(2 * N_HEADS):
        s = y[:, hd * DH:(hd + 1) * DH]
        s = s * lax.rsqrt(jnp.sum(s * s, axis=-1, keepdims=True) + EPS)
        if hd < N_HEADS:
            s = s * DH ** -0.5
        u_ref[0, :, hd * DH:(hd + 1) * DH] = s
    u_ref[0, :, 2 * MIX_W:] = y[:, 2 * MIX_W:]


def inproj_call(gdn, x, shift, scale, w_rest, w_conv, conv_w, conv_b):
    bn, t, d = x.shape
    n, c = w_rest.shape[1], w_conv.shape[1]
    tm = min(512, t)
    r8 = tm // 8
    vec = pl.BlockSpec((1, 1, d), lambda b, i: (b, 0, 0))
    full = lambda a: pl.BlockSpec(a.shape, lambda b, i: (0,) * a.ndim)
    cb = conv_b.reshape(1, c)
    return pl.pallas_call(
        functools.partial(_inproj_body, gdn),
        grid=(bn, t // tm),
        in_specs=[pl.BlockSpec((1, tm, d), lambda b, i: (b, i, 0)),
                  pl.BlockSpec((1, 8, d), lambda b, i: (b, jnp.maximum(i * r8 - 1, 0), 0)),
                  pl.BlockSpec((1, 8, d), lambda b, i: (b, jnp.minimum((i + 1) * r8, t // 8 - 1), 0)),
                  vec, vec, full(w_rest), full(w_conv), full(conv_w), full(cb)],
        out_specs=[pl.BlockSpec((1, tm, n), lambda b, i: (b, i, 0)), pl.BlockSpec((1, tm, c), lambda b, i: (b, i, 0))],
        out_shape=[jax.ShapeDtypeStruct((bn, t, n), F32), jax.ShapeDtypeStruct((bn, t, c), F32)],
        compiler_params=_cparams(("parallel", "parallel")),
        name="inproj_gdn" if gdn else "inproj_hyena",
    )(x, x, x, shift, scale, w_rest, w_conv, conv_w, cb)


def _block_masks(d, tb):
    sh = CHUNK.bit_length() - 1
    ii = lax.broadcasted_iota(jnp.int32, (tb, tb), 0)
    jj = lax.broadcasted_iota(jnp.int32, (tb, tb), 1)
    same = (ii >> sh) == (jj >> sh)
    vis = jnp.logical_and(same, (ii >= jj) if d == 0 else (ii <= jj))
    vis_t = jnp.logical_and(same, (jj >= ii) if d == 0 else (jj <= ii))
    return ii, jj, same, vis, vis_t


def _chunk_order(d, nc):
    return range(nc) if d == 0 else reversed(range(nc))


def _lockstep(gens):
    gens = list(gens)
    results = [None] * len(gens)
    live = list(range(len(gens)))
    while live:
        for i in list(live):
            try:
                next(gens[i])
            except StopIteration as stop:
                results[i] = stop.value
                live.remove(i)
    return results


def _mlstm_block(d, q, k, v, li_c, lf_c, li_r, lf_r, c_mat, n_vec, m):
    tb = q.shape[0]
    ii = lax.broadcasted_iota(jnp.int32, (tb, tb), 0)
    jj = lax.broadcasted_iota(jnp.int32, (tb, tb), 1)
    vis = (ii >= jj) if d == 0 else (ii <= jj)
    vis_t = (jj >= ii) if d == 0 else (jj <= ii)
    b_c = jnp.sum(vis.astype(F32) * lf_r, axis=1, keepdims=True)
    b_r = jnp.sum(vis_t.astype(F32) * lf_c, axis=0, keepdims=True)
    b_last = jnp.sum(lf_r, axis=1, keepdims=True)
    a_c = b_last - b_c + li_c
    m_loc = jnp.max(a_c, axis=0, keepdims=True)
    dmat = jnp.where(vis, b_c - b_r + li_r, -jnp.inf)
    m_in = jnp.max(dmat, axis=1, keepdims=True)
    yield
    s = _dot_nt(q, k) * jnp.exp(dmat - m_in)
    q_c = _dot_nt(q, c_mat)
    yield
    num_in = _dot(s, v)
    den_in = jnp.sum(s, axis=1, keepdims=True)
    m_inter = b_c + m
    m_t = jnp.maximum(m_inter, m_in)
    s_inter, s_in = jnp.exp(m_inter - m_t), jnp.exp(m_in - m_t)
    yield
    num = s_inter * q_c + s_in * num_in
    den = s_inter * jnp.sum(q * n_vec, axis=1, keepdims=True) + s_in * den_in
    h = num / jnp.maximum(jnp.abs(den), jnp.exp(-m_t))
    m_new = jnp.maximum(b_last + m, m_loc)
    f_old, f_loc = jnp.exp(b_last + m - m_new), jnp.exp(m_loc - m_new)
    kw = k * jnp.exp(a_c - m_loc)
    yield
    c_mat = f_old * c_mat + f_loc * _dot_tn(v, kw)
    n_vec = f_old * n_vec + f_loc * jnp.sum(kw, axis=0, keepdims=True)
    return h, c_mat, n_vec, m_new


def _mlstm_body(d, q_ref, k_ref, v_ref, gc_ref, gr_ref, bias_ref, c0_ref, n0_ref, m0_ref,
                h_ref, c_ref, n_ref, m_ref):
    @pl.when(pl.program_id(1) == 0)
    def _():
        c_ref[...] = c0_ref[...]
        n_ref[...] = n0_ref[...]
        m_ref[...] = m0_ref[...]

    gc = gc_ref[0, 0]
    gr = gr_ref[0, 0]

    def head(h):
        hs = slice(h * DH, (h + 1) * DH)
        bi = bias_ref[0, :, h:h + 1]
        bf = bias_ref[0, :, N_HEADS + h:N_HEADS + h + 1]
        li_c = gc[:, h:h + 1] + bi
        lf_c = _log_sigmoid(gc[:, N_HEADS + h:N_HEADS + h + 1] + bf)
        li_r = gr[h:h + 1, :] + bi
        lf_r = _log_sigmoid(gr[N_HEADS + h:N_HEADS + h + 1, :] + bf)
        return _mlstm_block(d, q_ref[0, :, hs], k_ref[0, :, hs] * DH ** -0.5, v_ref[0, :, hs],
                            li_c, lf_c, li_r, lf_r, c_ref[0, h], n_ref[0, h], m_ref[0, h][:, 0:1])

    for h, (hv, c_new, n_new, m_new) in enumerate(_lockstep(head(h) for h in range(N_HEADS))):
        h_ref[0, :, h * DH:(h + 1) * DH] = hv
        c_ref[0, h] = c_new
        n_ref[0, h] = n_new
        m_ref[0, h] = jnp.broadcast_to(m_new, (1, DH))


def _gdn_block(d, q, k, v, bt_c, la_c, la_r, s_mat):
    tb = q.shape[0]
    nc = tb // CHUNK
    sh = CHUNK.bit_length() - 1
    ii, jj, same, vis, vis_t = _block_masks(d, tb)
    g_c = jnp.sum(vis.astype(F32) * la_r, axis=1, keepdims=True)
    g_r = jnp.sum(vis_t.astype(F32) * la_c, axis=0, keepdims=True)
    gl_c = jnp.sum(same.astype(F32) * la_r, axis=1, keepdims=True)
    decay = jnp.exp(jnp.where(vis, g_c - g_r, -jnp.inf))
    kb = k * bt_c
    yield
    both = _dot_nt(jnp.concatenate([kb, q], axis=0), k)
    a_mat = jnp.where(ii == jj, 0.0, both[:tb] * decay)
    attn = both[tb:] * decay
    yield
    off = lambda lvl: jnp.logical_and((ii >> (lvl + 1)) == (jj >> (lvl + 1)), (ii >> lvl) != (jj >> lvl))
    t_mat = (ii == jj).astype(F32) - jnp.where(off(0), a_mat, 0.0)
    for lvl in range(1, sh):
        te = _dot(t_mat, jnp.where(off(lvl), a_mat, 0.0))
        yield
        t_mat = t_mat - _dot(te, t_mat)
        yield
    eg = jnp.exp(g_c)
    uw = _dot(t_mat, jnp.concatenate([v * bt_c, kb * eg], axis=1))
    u, w = uw[:, :DH], uw[:, DH:]
    kd = k * jnp.exp(gl_c - g_c)
    qg = q * eg
    yield
    v_new, o_state = [None] * nc, [None] * nc
    for c in _chunk_order(d, nc):
        sl = slice(c * CHUNK, (c + 1) * CHUNK)
        ws = _dot(jnp.concatenate([w[sl], qg[sl]], axis=0), s_mat)
        v_new[c] = u[sl] - ws[:CHUNK]
        o_state[c] = ws[CHUNK:]
        yield
        s_mat = s_mat * jnp.exp(gl_c[c * CHUNK:c * CHUNK + 1, :]) + _dot_tn(kd[sl], v_new[c])
        yield
    o = jnp.concatenate(o_state, axis=0) + _dot(attn, jnp.concatenate(v_new, axis=0))
    return o, s_mat


def _gdn_body(d, q_ref, k_ref, v_ref, gc_ref, gr_ref, dtb_ref, alog_ref, s0_ref, o_ref, s_ref):
    @pl.when(pl.program_id(1) == 0)
    def _():
        s_ref[...] = s0_ref[...]

    gc = gc_ref[0, 0]
    gr = gr_ref[0, 0]

    def head(h):
        hs = slice(h * DH, (h + 1) * DH)
        dtb = dtb_ref[0, :, h:h + 1]
        neg_a = -jnp.exp(alog_ref[0, :, h:h + 1])
        bt_c = _sigmoid(gc[:, h:h + 1])
        la_c = neg_a * _softplus(gc[:, N_HEADS + h:N_HEADS + h + 1] + dtb)
        la_r = neg_a * _softplus(gr[N_HEADS + h:N_HEADS + h + 1, :] + dtb)
        return _gdn_block(d, q_ref[0, :, hs], k_ref[0, :, hs], v_ref[0, :, hs], bt_c, la_c, la_r, s_ref[0, h])

    for h, (ov, s_new) in enumerate(_lockstep(head(h) for h in range(N_HEADS))):
        o_ref[0, :, h * DH:(h + 1) * DH] = ov
        s_ref[0, h] = s_new


def _gate_views(graw):
    bsz, t, _ = graw.shape
    gcol = graw.reshape(bsz, t, 2, 2, N_HEADS).transpose(0, 3, 1, 2, 4).reshape(bsz, 2, t, 8)
    return gcol, gcol.transpose(0, 1, 3, 2)


def _scan_specs(d, t, col0):
    tb = min(SCAN_TB, t)
    nb = t // tb
    blk = (lambda i: i) if d == 0 else (lambda i: nb - 1 - i)
    qkv = [pl.BlockSpec((1, tb, MIX_W), lambda b, i, j=j: (b, blk(i), col0 + j)) for j in range(3)]
    gates = [pl.BlockSpec((1, 1, tb, 8), lambda b, i: (b, d, blk(i), 0)),
             pl.BlockSpec((1, 1, 8, tb), lambda b, i: (b, d, 0, blk(i)))]
    par = pl.BlockSpec((1, 1, 8), lambda b, i: (d, 0, 0))
    out = pl.BlockSpec((1, tb, MIX_W), lambda b, i: (b, blk(i), 0))
    return nb, qkv, gates, par, out


def _state_spec(*tail):
    return pl.BlockSpec((1, N_HEADS) + tail, lambda b, i: (b, 0) + (0,) * len(tail))


def mlstm_call(d, z, bias, state):
    bsz, t, _ = z.shape
    nb, qkv, gates, par, out = _scan_specs(d, t, E_QKV // MIX_W)
    gcol, grow = _gate_views(z[..., E_GATE:E_GATE + 16])
    st_specs = [_state_spec(DH, DH), _state_spec(1, DH), _state_spec(1, DH)]
    st_shapes = [jax.ShapeDtypeStruct((bsz, N_HEADS, DH, DH), F32),
                 jax.ShapeDtypeStruct((bsz, N_HEADS, 1, DH), F32),
                 jax.ShapeDtypeStruct((bsz, N_HEADS, 1, DH), F32)]
    res = pl.pallas_call(
        functools.partial(_mlstm_body, d),
        grid=(bsz, nb),
        in_specs=qkv + gates + [par] + st_specs,
        out_specs=[out] + st_specs,
        out_shape=[jax.ShapeDtypeStruct((bsz, t, MIX_W), F32)] + st_shapes,
        compiler_params=_cparams(("parallel", "arbitrary")),
        name="mlstm_scan_fwd" if d == 0 else "mlstm_scan_bwd",
    )(z, z, z, gcol, grow, bias, *state)
    return res[0], tuple(res[1:])


def gdn_call(d, qkv_arr, graw, dtb, alog, state):
    bsz, t, _ = qkv_arr.shape
    nb, qkv, gates, par, out = _scan_specs(d, t, 0)
    gcol, grow = _gate_views(graw)
    st_spec = _state_spec(DH, DH)
    return pl.pallas_call(
        functools.partial(_gdn_body, d),
        grid=(bsz, nb),
        in_specs=qkv + gates + [par, par, st_spec],
        out_specs=[out, st_spec],
        out_shape=[jax.ShapeDtypeStruct((bsz, t, MIX_W), F32),
                   jax.ShapeDtypeStruct((bsz, N_HEADS, DH, DH), F32)],
        compiler_params=_cparams(("parallel", "arbitrary")),
        name="gdn_scan_fwd" if d == 0 else "gdn_scan_bwd",
    )(qkv_arr, qkv_arr, qkv_arr, gcol, grow, dtb, alog, state)


def _outproj_body(act, hf_ref, hb_ref, gate_ref, y_ref, g_ref, w_ref, x_ref, mg_ref, o_ref):
    hs = hf_ref[0] + hb_ref[0]
    hn = jnp.concatenate([_rms(hs[:, h * DH:(h + 1) * DH]) for h in range(N_HEADS)], axis=-1) * g_ref[...]
    gate = gate_ref[0]
    hn = hn * (_sigmoid(gate) if act == "sigmoid" else _silu(gate))
    out = _dot(hn, w_ref[0:MIX_W, :]) + _dot(y_ref[0], w_ref[MIX_W:, :])
    o_ref[0] = x_ref[0] + mg_ref[0] * out


def outproj_call(act, hf, hb, z, gate_block, y, g, w, x, mgate):
    bn, t, d = x.shape
    tm = min(512, t)
    return pl.pallas_call(
        functools.partial(_outproj_body, act),
        grid=(bn, t // tm),
        in_specs=[pl.BlockSpec((1, tm, MIX_W), lambda b, i: (b, i, 0)),
                  pl.BlockSpec((1, tm, MIX_W), lambda b, i: (b, i, 0)),
                  pl.BlockSpec((1, tm, MIX_W), lambda b, i: (b, i, gate_block)),
                  pl.BlockSpec((1, tm, MIX_W), lambda b, i: (b, i, 0)),
                  pl.BlockSpec((1, MIX_W), lambda b, i: (0, 0)),
                  pl.BlockSpec((2 * MIX_W, d), lambda b, i: (0, 0)),
                  pl.BlockSpec((1, tm, d), lambda b, i: (b, i, 0)),
                  pl.BlockSpec((1, 1, d), lambda b, i: (b, 0, 0))],
        out_specs=pl.BlockSpec((1, tm, d), lambda b, i: (b, i, 0)),
        out_shape=jax.ShapeDtypeStruct((bn, t, d), F32),
        compiler_params=_cparams(("parallel", "parallel")),
        name="outproj_" + act,
    )(hf, hb, z, y, g, w, x, mgate)


def _ffn_body(x_ref, xp_ref, xn_ref, sh_ref, sc_ref, mg_ref, wu_ref, cw_ref, wd_ref, o_ref):
    i, nt = pl.program_id(1), pl.num_programs(1)
    sc, sh = 1.0 + sc_ref[0], sh_ref[0]
    x = x_ref[0]
    tm = x.shape[0]
    mod = lambda v: _rms(v) * sc + sh
    hp = mod(xp_ref[0]) * (i > 0).astype(F32)
    hn = mod(xn_ref[0]) * (i < nt - 1).astype(F32)
    h = jnp.concatenate([hp, mod(x), hn], axis=0).astype(BF16)
    rows = tm + 16

    cols = lambda j, half: slice(half * D_FF + j * FF_TILE, half * D_FF + (j + 1) * FF_TILE)
    up = lambda j: tuple(jnp.dot(h, wu_ref[:, cols(j, half)], preferred_element_type=F32) for half in range(2))

    def conv(u, cw):
        y = pltpu.roll(u, 1, 0) * cw[0:1, :] + u * cw[1:2, :] + pltpu.roll(u, rows - 1, 0) * cw[2:3, :] + cw[3:4, :]
        return y[8:8 + tm]

    acc = jnp.zeros(x.shape, F32)
    u_next = up(0)
    for j in range(N_FF_TILES):
        ua, ub = u_next
        if j + 1 < N_FF_TILES:
            u_next = up(j + 1)
        a, b = conv(ua, cw_ref[:, cols(j, 0)]), conv(ub, cw_ref[:, cols(j, 1)])
        acc = acc + _dot(_silu(a) * b, wd_ref[j * FF_TILE:(j + 1) * FF_TILE, :])
    o_ref[0] = x + mg_ref[0] * acc


def ffn_call(x, shift, scale, mgate, w_up, conv_w, conv_b, w_down):
    bn, t, d = x.shape
    tm = min(512, t)
    r8 = tm // 8
    wu, wd = w_up.astype(BF16), w_down.astype(BF16)
    cw = jnp.concatenate([conv_w, conv_b[None]], axis=0)
    full = lambda a: pl.BlockSpec(a.shape, lambda b, i: (0,) * a.ndim)
    vec = pl.BlockSpec((1, 1, d), lambda b, i: (b, 0, 0))
    return pl.pallas_call(
        _ffn_body,
        grid=(bn, t // tm),
        in_specs=[pl.BlockSpec((1, tm, d), lambda b, i: (b, i, 0)),
                  pl.BlockSpec((1, 8, d), lambda b, i: (b, jnp.maximum(i * r8 - 1, 0), 0)),
                  pl.BlockSpec((1, 8, d), lambda b, i: (b, jnp.minimum((i + 1) * r8, t // 8 - 1), 0)),
                  vec, vec, vec, full(wu), full(cw), full(wd)],
        out_specs=pl.BlockSpec((1, tm, d), lambda b, i: (b, i, 0)),
        out_shape=jax.ShapeDtypeStruct((bn, t, d), F32),
        compiler_params=_cparams(("parallel", "parallel")),
        name="conv_ffn",
    )(x, x, x, shift, scale, mgate, wu, cw, wd)


def _filter_hidden(lf, n, w1t_ref, w1c_ref, w1s_ref, b1_ref, w2_ref, b2_ref, w3_ref, b3_ref, fr_ref, band_ref):
    t = jnp.where(n < lf, n, 2 * lf - n).astype(F32)
    t_norm = t / (lf - 1)
    ang = (2.0 * math.pi * t / lf) * band_ref[...]
    fr = fr_ref[...]
    pre = t_norm * w1t_ref[...] + _dot_hi(jnp.cos(ang), w1c_ref[...]) - _dot_hi(jnp.sin(ang), w1s_ref[...])
    hdn = jnp.sin(fr * (pre + b1_ref[...]))
    hdn = jnp.sin(fr * (_dot_hi(hdn, w2_ref[...]) + b2_ref[...]))
    hdn = jnp.sin(fr * (_dot_hi(hdn, w3_ref[...]) + b3_ref[...]))
    return hdn, t_norm


def _filter_values(lf, n, hdn, t_norm, w4, dl):
    hval = _dot(hdn, w4) * jnp.exp(-t_norm * jnp.abs(dl))
    return jnp.where(n == lf, 0.0, hval)


def _hy_filter_body(lf, *refs):
    mlp_refs, (w4_ref, dl_ref, o_ref, l1_ref) = refs[:-4], refs[-4:]
    r = pl.program_id(0)
    tr = o_ref.shape[0]
    n = (r * tr + lax.broadcasted_iota(jnp.int32, (tr, 1), 0))
    hdn, t_norm = _filter_hidden(lf, n, *mlp_refs)
    hval = _filter_values(lf, n, hdn, t_norm, w4_ref[0], dl_ref[0])
    o_ref[...] = hval

    @pl.when(r == 0)
    def _():
        l1_ref[...] = jnp.zeros_like(l1_ref)

    l1_ref[...] += jnp.sum(jnp.abs(hval), axis=0, keepdims=True)


def _hy_filter_lead_body(lf, *refs):
    mlp_refs, (w4_ref, dl_ref, f_ref, o_ref, l1_ref) = refs[:-5], refs[-5:]
    jb = pl.program_id(0)
    half = FFT_N1 // 2
    per_dir = LEAD_ROWS * half
    sh_half, sh_dir = half.bit_length() - 1, per_dir.bit_length() - 1
    ncol = l1_ref.shape[-1]
    r = lax.broadcasted_iota(jnp.int32, (2 * per_dir, 1), 0)
    dd, rem = r >> sh_dir, r & (per_dir - 1)
    n = FFT_N1 * ((dd << sh_half) + (rem & (half - 1))) + LEAD_ROWS * jb + (rem >> sh_half)
    hdn, t_norm = _filter_hidden(lf, n, *mlp_refs)
    vals = []
    for d in range(2):
        rows = slice(d * per_dir, (d + 1) * per_dir)
        vals.append(_filter_values(lf, n[rows], hdn[rows], t_norm[rows], w4_ref[d], dl_ref[d]))

    @pl.when(jb == 0)
    def _():
        l1_ref[...] = jnp.zeros_like(l1_ref)

    l1_ref[...] += sum(jnp.sum(jnp.abs(v), axis=0, keepdims=True) for v in vals)
    for j in range(LEAD_ROWS):
        slab = jnp.concatenate([v[j * half:(j + 1) * half] for v in vals], axis=0)
        o_ref[:, j * ncol:(j + 1) * ncol] = _dot(f_ref[...], slab).astype(o_ref.dtype)


def _filter_args(f_w1, f_b1, f_w2, f_b2, f_w3, f_b3, f_w4, f_freq):
    ncol = 2 * HYENA_W
    pad = lambda a: jnp.concatenate([a, jnp.zeros((DH - a.shape[0], a.shape[1]), F32)], axis=0)
    w4 = f_w4.reshape(HYENA_FFN, 2, 2, HYENA_W).transpose(2, 0, 1, 3).reshape(2, HYENA_FFN, ncol)
    deltas = jnp.linspace(math.log(HYENA_TARGET) / HYENA_SLOW_DECAY, math.log(HYENA_TARGET) / HYENA_FAST_DECAY,
                          HYENA_NFILT, dtype=F32)
    deltas = deltas.reshape(2, 2, HYENA_W).transpose(1, 0, 2).reshape(2, 1, ncol)
    bands = jnp.linspace(1e-4, HYENA_BANDS - 1, HYENA_BANDS, dtype=F32)
    bands = jnp.concatenate([bands, jnp.zeros((DH - HYENA_BANDS,), F32)]).reshape(1, DH)
    row = lambda a: a.reshape(1, -1)
    mlp = (f_w1[0:1], pad(f_w1[1:1 + HYENA_BANDS]), pad(f_w1[1 + HYENA_BANDS:]), row(f_b1), f_w2, row(f_b2),
           f_w3, row(f_b3), row(f_freq), bands)
    return mlp, w4, deltas


def hy_filter_call(lf, *filt):
    tr = min(512, lf)
    nt = 2 * lf // tr
    half = lf // tr
    ncol = 2 * HYENA_W
    mlp, w4, deltas = _filter_args(*filt)
    full = lambda a: pl.BlockSpec(a.shape, lambda r: (0,) * a.ndim)
    by_dir = lambda a: pl.BlockSpec((1,) + a.shape[1:], lambda r: (r // half, 0, 0))
    return pl.pallas_call(
        functools.partial(_hy_filter_body, lf),
        grid=(nt,),
        in_specs=[full(a) for a in mlp] + [by_dir(w4), by_dir(deltas)],
        out_specs=[pl.BlockSpec((tr, ncol), lambda r: (r, 0)), pl.BlockSpec((1, ncol), lambda r: (0, 0))],
        out_shape=[jax.ShapeDtypeStruct((2 * lf, ncol), F32), jax.ShapeDtypeStruct((1, ncol), F32)],
        compiler_params=_cparams(("arbitrary",)),
        name="hyena_filter",
    )(*mlp, w4, deltas)


def hy_filter_lead_call(lf, f_flt, *filt):
    ncol = 2 * HYENA_W
    mlp, w4, deltas = _filter_args(*filt)
    full = lambda a: pl.BlockSpec(a.shape, lambda j: (0,) * a.ndim)
    return pl.pallas_call(
        functools.partial(_hy_filter_lead_body, lf),
        grid=(FFT_N1 // LEAD_ROWS,),
        in_specs=[full(a) for a in mlp + (w4, deltas, f_flt)],
        out_specs=[pl.BlockSpec((f_flt.shape[0], LEAD_ROWS * ncol), lambda j: (0, j)),
                   pl.BlockSpec((1, ncol), lambda j: (0, 0))],
        out_shape=[jax.ShapeDtypeStruct((f_flt.shape[0], FFT_N1 * ncol), BF16), jax.ShapeDtypeStruct((1, ncol), F32)],
        compiler_params=_cparams(("arbitrary",)),
        name="hyena_filter_lead",
    )(*mlp, w4, deltas, f_flt)


def _dft_consts(lf):
    n1 = FFT_N1
    n = 2 * lf
    assert n == n1 * n1
    idx = np.arange(n1)
    f = np.exp(-2j * np.pi * np.outer(idx, idx) / n1)
    tw = np.exp(-2j * np.pi * np.outer(idx, idx) / n)
    fh = f[:, :n1 // 2]
    blk = lambda a: np.block([[a.real, -a.imag], [a.imag, a.real]])
    f_sig = blk(fh)
    f_flt = np.concatenate([f.real, f.imag], axis=0)
    c_inv = blk(np.conj(f)[:n1 // 2, :] / n)
    c = lambda a: jnp.asarray(a, dtype=F32)
    cb = lambda a: jnp.asarray(a, dtype=F32).astype(BF16)
    return dict(f_sig=cb(f_sig), f_flt=cb(f_flt), c_inv=cb(c_inv), fr=c(f.real), fi=c(f.imag),
                twr=c(tw.real.reshape(n1, 1, n1)), twi=c(tw.imag.reshape(n1, 1, n1)))


LEAD_ROWS = 8


def _lead_mm_body(m_ref, x_ref, o_ref):
    w = x_ref.shape[-1]
    for j in range(LEAD_ROWS):
        o_ref[:, j * w:(j + 1) * w] = _dot(m_ref[...], x_ref[:, j, :]).astype(o_ref.dtype)


def lead_mm_call(mat, x3, group, w, name):
    r, k = mat.shape
    n1 = x3.shape[1]
    return pl.pallas_call(
        _lead_mm_body,
        grid=(n1 // LEAD_ROWS,),
        in_specs=[pl.BlockSpec((r, k), lambda j: (0, 0)), pl.BlockSpec((k, LEAD_ROWS, w), lambda j: (0, j, group))],
        out_specs=pl.BlockSpec((r, LEAD_ROWS * w), lambda j: (0, j)),
        out_shape=jax.ShapeDtypeStruct((r, n1 * w), BF16),
        compiler_params=_cparams(("parallel",)),
        name=name,
    )(mat, x3)


def _stage2_mat(fr, fi, twr, twi):
    hr = fr * twr - fi * twi
    hi = fr * twi + fi * twr
    return hr, hi


def _blk(ar, ai):
    return jnp.concatenate([jnp.concatenate([ar, -ai], axis=1), jnp.concatenate([ai, ar], axis=1)], axis=0)


SPEC_K1 = 2


def _spec_filter_body(a_ref, l1_ref, fr_ref, fi_ref, twr_ref, twi_ref, o_ref):
    inv_l1 = 1.0 / l1_ref[...]
    for kk in range(SPEC_K1):
        hr, hi = _stage2_mat(fr_ref[...], fi_ref[...], twr_ref[kk], twi_ref[kk])
        n1 = hr.shape[0]
        a = a_ref[:, kk].reshape(2 * n1, a_ref.shape[-1])
        o_ref[kk] = (_dot(_blk(hr, hi), a) * inv_l1).astype(o_ref.dtype)


def _spec_conv_body(a_ref, k_ref, fr_ref, fi_ref, twr_ref, twi_ref, o_ref):
    for kk in range(SPEC_K1):
        hr, hi = _stage2_mat(fr_ref[...], fi_ref[...], twr_ref[kk], twi_ref[kk])
        n1 = hr.shape[0]
        x = _dot(_blk(hr, hi), a_ref[:, kk].reshape(2 * n1, a_ref.shape[-1]))
        xr, xi = x[:n1], x[n1:]
        kr, ki = k_ref[kk, :n1].astype(F32), k_ref[kk, n1:].astype(F32)
        y = jnp.concatenate([xr * kr - xi * ki, xr * ki + xi * kr], axis=0)
        o_ref[kk] = _dot(_blk(hr.T, -(hi.T)), y).astype(o_ref.dtype)


def spec_call(a2d, kspec, order, cst, l1=None):
    n1 = FFT_N1
    c = a2d.shape[1] // n1
    ct = 512 if kspec is not None else min(1024, c)
    a4 = a2d.reshape(2, n1, n1, c)
    sq = pl.BlockSpec((n1, n1), lambda k, j: (0, 0))
    tw = pl.BlockSpec((SPEC_K1, 1, n1), lambda k, j: (k, 0, 0))
    a_spec = pl.BlockSpec((2, SPEC_K1, n1, ct), lambda k, j: (0, k, 0, j))
    o_spec = pl.BlockSpec((SPEC_K1, 2 * n1, ct), lambda k, j: (k, 0, j))
    common = dict(grid=(n1 // SPEC_K1, c // ct), out_specs=o_spec,
                  out_shape=jax.ShapeDtypeStruct((n1, 2 * n1, c), BF16),
                  compiler_params=_cparams(("parallel", "parallel")))
    if kspec is None:
        l1_spec = pl.BlockSpec((1, ct), lambda k, j: (0, j))
        return pl.pallas_call(_spec_filter_body, in_specs=[a_spec, l1_spec, sq, sq, tw, tw], name="hyena_spec_filter",
                              **common)(a4, l1, cst["fr"], cst["fi"], cst["twr"], cst["twi"])
    k_spec = pl.BlockSpec((SPEC_K1, 2 * n1, ct), lambda k, j: (k, 0, order * (c // ct) + j))
    return pl.pallas_call(_spec_conv_body, in_specs=[a_spec, k_spec, sq, sq, tw, tw], name="hyena_spec_conv",
                          **common)(a4, kspec, cst["fr"], cst["fi"], cst["twr"], cst["twi"])


def _inv_gate_body(m_ref, br_ref, bi_ref, v_ref, x_ref, d_ref, o_ref):
    w = o_ref.shape[-1]
    y = _dot(m_ref[...], jnp.concatenate([br_ref[...], bi_ref[...]], axis=0))
    for j in range(LEAD_ROWS):
        o_ref[:, j, :] = x_ref[:, j, :] * (y[:, j * w:(j + 1) * w] + v_ref[:, j, :] * d_ref[...])


def inv_gate_call(b3, v3, gv, x3, gx, d_row, cst):
    n1 = FFT_N1
    w = HYENA_W
    nb = n1 // LEAD_ROWS
    b2d = b3.reshape(n1, 2 * n1 * w)
    tn = LEAD_ROWS * w
    rows3 = lambda g: pl.BlockSpec((n1, LEAD_ROWS, w), lambda j: (0, j, g))
    return pl.pallas_call(
        _inv_gate_body,
        grid=(nb,),
        in_specs=[pl.BlockSpec((n1, 2 * n1), lambda j: (0, 0)), pl.BlockSpec((n1, tn), lambda j: (0, j)),
                  pl.BlockSpec((n1, tn), lambda j: (0, nb + j)), rows3(gv), rows3(gx),
                  pl.BlockSpec((1, w), lambda j: (0, 0))],
        out_specs=rows3(0),
        out_shape=jax.ShapeDtypeStruct((n1, n1, w), F32),
        compiler_params=_cparams(("parallel",)),
        name="hyena_inv_gate",
    )(cst["c_inv"], b2d, b2d, v3, x3, d_row)


def hyena_long(u, filt, d_skip):
    bsz, lf, _ = u.shape
    assert bsz == 2
    n1 = FFT_N1
    w = HYENA_W
    cst = _dft_consts(lf)
    u3 = u.reshape(bsz * n1 // 2, n1, 3 * w)
    a_flt, l1 = hy_filter_lead_call(lf, cst["f_flt"], *filt)
    kspec = spec_call(a_flt, None, 0, cst, l1)
    y3, gy = u3, 0
    for o in range(2):
        a = lead_mm_call(cst["f_sig"], y3, gy, w, "hyena_lead_signal")
        b3 = spec_call(a, kspec, o, cst)
        y3, gy = inv_gate_call(b3, y3, gy, u3, o + 1, d_skip[o:o + 1], cst), 0
    return y3.reshape(bsz, lf, w)


def _hyena_small_body(u_ref, kf_ref, l1_ref, fs_ref, ff_ref, ci_ref, d_ref, o_ref):
    w = HYENA_W
    nk = ff_ref.shape[0] // 2
    kspec = _dot(ff_ref[...], kf_ref[...]) * (1.0 / l1_ref[...])
    ucat = jnp.concatenate([u_ref[0], u_ref[1]], axis=0)
    y = ucat[:, :w]
    for o in range(2):
        x = _dot(fs_ref[...], y)
        xr, xi = x[:nk], x[nk:]
        kr, ki = kspec[:nk, o * w:(o + 1) * w], kspec[nk:, o * w:(o + 1) * w]
        prod = jnp.concatenate([xr * kr - xi * ki, xr * ki + xi * kr], axis=0)
        conv = _dot(ci_ref[...], prod)
        y = ucat[:, (o + 1) * w:(o + 2) * w] * (conv + y * d_ref[o:o + 1, :])
    lf = u_ref.shape[1]
    o_ref[0] = y[:lf]
    o_ref[1] = y[lf:]


def hyena_small(u, kf, l1, d_skip):
    bsz, lf, _ = u.shape
    assert bsz == 2
    n = 2 * lf
    idx = np.arange(n)
    f = np.exp(-2j * np.pi * np.outer(idx, idx) / n)
    blk = lambda a: np.block([[a.real, -a.imag], [a.imag, a.real]])
    cb = lambda a: jnp.asarray(a, F32).astype(BF16)
    fs = cb(blk(f[:, :lf]))
    ff = cb(np.concatenate([f.real, f.imag], axis=0))
    ci = cb(blk(np.conj(f)[:lf, :] / n))
    return pl.pallas_call(
        _hyena_small_body,
        out_shape=jax.ShapeDtypeStruct((bsz, lf, HYENA_W), F32),
        compiler_params=pltpu.CompilerParams(vmem_limit_bytes=VMEM_LIMIT),
        name="hyena_small",
    )(u, kf, l1, fs, ff, ci, d_skip)


def _rope(x, cos_ref, sa_ref, sb_ref):
    w = x.shape[-1]
    half = MLA_ROPE // 2
    return x * cos_ref[...] + pltpu.roll(x, w - half, 1) * sa_ref[...] + pltpu.roll(x, half, 1) * sb_ref[...]


def _kv_body(rope, lkv_ref, lkr_ref, g_ref, w_ref, kg_ref, cos_ref, sa_ref, sb_ref, k_ref, v_ref):
    kv = _dot(_rms(lkv_ref[0]) * g_ref[...], w_ref[...])
    kr = lkr_ref[0]
    for h in range(N_HEADS):
        base = h * (MLA_NOPE + DH)
        k = jnp.concatenate([kv[:, base:base + MLA_NOPE], kr], axis=-1)
        k = _rms(k, MLA_QK) * kg_ref[...]
        if rope:
            k = _rope(k, cos_ref, sa_ref, sb_ref)
        k_ref[0, h] = k.astype(BF16)
        v = kv[:, base + MLA_NOPE:base + MLA_NOPE + DH]
        v_ref[0, h] = jnp.concatenate([v, jnp.ones_like(v)], axis=-1).astype(BF16)


def kv_call(z, rope, g, w, kg, tabs):
    bn, t, _ = z.shape
    tm = min(512, t)
    tab = pl.BlockSpec((tm, MLA_QK_PAD), lambda b, i: (i, 0))
    return pl.pallas_call(
        functools.partial(_kv_body, rope),
        grid=(bn, t // tm),
        in_specs=[pl.BlockSpec((1, tm, MLA_KV_RANK), lambda b, i: (b, i, O_LKV // MLA_KV_RANK)),
                  pl.BlockSpec((1, tm, DH), lambda b, i: (b, i, O_LKR // DH)),
                  pl.BlockSpec((1, MLA_KV_RANK), lambda b, i: (0, 0)),
                  pl.BlockSpec(w.shape, lambda b, i: (0, 0)),
                  pl.BlockSpec((1, MLA_QK_PAD), lambda b, i: (0, 0)), tab, tab, tab],
        out_specs=[pl.BlockSpec((1, N_HEADS, tm, MLA_QK_PAD), lambda b, i: (b, 0, i, 0)),
                   pl.BlockSpec((1, N_HEADS, tm, 2 * DH), lambda b, i: (b, 0, i, 0))],
        out_shape=[jax.ShapeDtypeStruct((bn, N_HEADS, t, MLA_QK_PAD), BF16),
                   jax.ShapeDtypeStruct((bn, N_HEADS, t, 2 * DH), BF16)],
        compiler_params=_cparams(("parallel", "parallel")),
        name="mla_kv_rope" if rope else "mla_kv",
    )(z, z, g, w, kg, *tabs)


def _q_body(lq_ref, g_ref, w_ref, qg_ref, cos_ref, sa_ref, sb_ref, q_ref):
    q = _dot(_rms(lq_ref[0]) * g_ref[...], w_ref[...])
    for h in range(N_HEADS):
        qh = _rms(q[:, h * MLA_QK_PAD:(h + 1) * MLA_QK_PAD], MLA_QK) * qg_ref[...]
        qh = _rope(qh, cos_ref, sa_ref, sb_ref) * (MLA_QK ** -0.5 * LOG2E)
        q_ref[0, h] = qh.astype(BF16)


def q_call(z, g, w, qg, tabs):
    bn, t, _ = z.shape
    tm = min(512, t)
    tab = pl.BlockSpec((tm, MLA_QK_PAD), lambda b, i: (i, 0))
    return pl.pallas_call(
        _q_body,
        grid=(bn, t // tm),
        in_specs=[pl.BlockSpec((1, tm, MLA_Q_RANK), lambda b, i: (b, i, O_LQ // MLA_Q_RANK)),
                  pl.BlockSpec((1, MLA_Q_RANK), lambda b, i: (0, 0)),
                  pl.BlockSpec(w.shape, lambda b, i: (0, 0)),
                  pl.BlockSpec((1, MLA_QK_PAD), lambda b, i: (0, 0)), tab, tab, tab],
        out_specs=pl.BlockSpec((1, N_HEADS, tm, MLA_QK_PAD), lambda b, i: (b, 0, i, 0)),
        out_shape=jax.ShapeDtypeStruct((bn, N_HEADS, t, MLA_QK_PAD), BF16),
        compiler_params=_cparams(("parallel", "parallel")),
        name="mla_q",
    )(z, g, w, qg, *tabs)


def _attn_body(q_ref, kc_ref, vc_ref, kx_ref, vx_ref, o_ref):
    q = q_ref[0, 0]
    tiles = [(kc_ref, vc_ref, 0, kc_ref.shape[2])]
    tk = min(ATTN_TK, kx_ref.shape[2])
    tiles += [(kx_ref, vx_ref, s0, tk) for s0 in range(0, kx_ref.shape[2], tk)]
    scores = lambda t: lax.dot_general(q, t[0][0, 0, t[2]:t[2] + t[3], :], (((1,), (1,)), ((), ())),
                                       preferred_element_type=F32)
    m = jnp.full((q.shape[0], 1), -jnp.inf, F32)
    acc = jnp.zeros((q.shape[0], 2 * DH), F32)
    s_next = scores(tiles[0])
    for idx, t in enumerate(tiles):
        s = s_next
        if idx + 1 < len(tiles):
            s_next = scores(tiles[idx + 1])
        m_new = jnp.maximum(m, jnp.max(s, axis=1, keepdims=True))
        p = jnp.exp2(s - m_new).astype(BF16)
        acc = jnp.exp2(m - m_new) * acc + jnp.dot(p, t[1][0, 0, t[2]:t[2] + t[3], :], preferred_element_type=F32)
        m = m_new
    o_ref[0] = acc[:, :DH] / acc[:, DH:]


def attn_call(q, kc, vc, kx, vx):
    bn, nh, t, dq = q.shape
    tc = kc.shape[2]
    tq = ATTN_TQ
    whole = lambda n, w: pl.BlockSpec((1, 1, n, w), lambda b, h, i: (b, h, 0, 0))
    return pl.pallas_call(
        _attn_body,
        grid=(bn, nh, t // tq),
        in_specs=[pl.BlockSpec((1, 1, tq, dq), lambda b, h, i: (b, h, i, 0)),
                  whole(tc, dq), whole(tc, 2 * DH), whole(t, dq), whole(t, 2 * DH)],
        out_specs=pl.BlockSpec((1, tq, DH), lambda b, h, i: (b, i, h)),
        out_shape=jax.ShapeDtypeStruct((bn, t, nh * DH), F32),
        compiler_params=_cparams(("parallel", "parallel", "parallel")),
        name="mla_attention",
    )(q, kc, vc, kx, vx)


def rope_tables(seq):
    rows = seq // GRID_W
    row = jnp.repeat(jnp.arange(rows), GRID_W).astype(F32)
    col = jnp.tile(jnp.arange(GRID_W), rows).astype(F32)
    n_freq = MLA_ROPE // 4
    inv = ROPE_THETA ** (-jnp.arange(n_freq, dtype=F32) / n_freq)
    ang = jnp.concatenate([row[:, None] * inv, col[:, None] * inv], axis=-1)
    cos, sin = jnp.cos(ang), jnp.sin(ang)
    half = MLA_ROPE // 2
    ones = jnp.ones((seq, MLA_NOPE), F32)
    zn = jnp.zeros((seq, MLA_NOPE), F32)
    zh = jnp.zeros((seq, half), F32)
    zp = jnp.zeros((seq, MLA_QK_PAD - MLA_QK), F32)
    cos_t = jnp.concatenate([ones, cos, cos, zp], axis=-1)
    sin_a = jnp.concatenate([zn, -sin, zh, zp], axis=-1)
    sin_b = jnp.concatenate([zn, zh, sin, zp], axis=-1)
    return cos_t, sin_a, sin_b


def _padcols(w, n):
    return jnp.concatenate([w, jnp.zeros((w.shape[0], n - w.shape[1]), w.dtype)], axis=1)


def pack_even(w):
    q, k, v, o = (w[:, j * MIX_W:(j + 1) * MIX_W] for j in range(4))
    gates = w[:, 4 * MIX_W:4 * MIX_W + 16]
    hy = w[:, 4 * MIX_W + 16:]
    return jnp.concatenate([q, k, v, o, _padcols(gates, DH)], axis=1).astype(BF16), hy.astype(BF16)


def pack_odd(w):
    q, k, v, gg = (w[:, j * MIX_W:(j + 1) * MIX_W] for j in range(4))
    c0 = 4 * MIX_W
    gates = w[:, c0:c0 + 16]
    lq = w[:, c0 + 16:c0 + 16 + MLA_Q_RANK]
    lkv = w[:, c0 + 16 + MLA_Q_RANK:c0 + 16 + MLA_Q_RANK + MLA_KV_RANK]
    lkr = w[:, c0 + 16 + MLA_Q_RANK + MLA_KV_RANK:]
    rest = jnp.concatenate([gg, lkv, lq, _padcols(lkr, DH), _padcols(gates, DH)], axis=1)
    return rest.astype(BF16), jnp.concatenate([q, k, v], axis=1).astype(BF16)


def kernel(x, c, ctx, c_ctx, mod_w, mod_b, a_w_in, a_i_bias, a_f_bias, a_norm_g, b_conv_w, b_conv_b, b_f_w1, b_f_b1, b_f_w2, b_f_b2, b_f_w3, b_f_b3, b_f_w4, b_f_freq, b_d, ab_w_out, cd_w_in, c_conv_w, c_conv_b, c_a_log, c_dt_bias, c_norm_g, d_q_norm_g, d_w_q_up, d_kv_norm_g, d_w_kv_up, d_qn_g, d_kn_g, cd_w_out, ffn_w_up, ffn_conv_w, ffn_conv_b, ffn_w_down):
    bsz, seq, d = x.shape
    ctx_len = ctx.shape[1]
    cvec = jnp.concatenate([c, c_ctx[None], jnp.zeros((8 - bsz - 1, d), F32)], axis=0)
    mods = mods_call(cvec, mod_w, mod_b)
    mx = lambda l, j: mods[l, :bsz, None, j * d:(j + 1) * d]
    mc = lambda l, j: jnp.broadcast_to(mods[l, bsz, j * d:(j + 1) * d], (bsz, 1, d))

    def ffn(l, h, m):
        return ffn_call(h, m(l, 3), m(l, 4), m(l, 5), ffn_w_up[l], ffn_conv_w[l], ffn_conv_b[l], ffn_w_down[l])

    w_in = pack_even(a_w_in[0])
    zx, ux = inproj_call(False, x, mx(0, 0), mx(0, 1), *w_in, b_conv_w[0], b_conv_b[0])
    zc, uc = inproj_call(False, ctx, mc(0, 0), mc(0, 1), *w_in, b_conv_w[0], b_conv_b[0])
    bias = jnp.concatenate([a_i_bias[0].reshape(2, N_HEADS), a_f_bias[0].reshape(2, N_HEADS)], axis=-1)[:, None, :]
    zero_state = (jnp.zeros((bsz, N_HEADS, DH, DH), F32), jnp.zeros((bsz, N_HEADS, 1, DH), F32),
                  jnp.zeros((bsz, N_HEADS, 1, DH), F32))
    hc, hx = [], []
    for dd in range(2):
        h_ctx, st = mlstm_call(dd, zc, bias, zero_state)
        hc.append(h_ctx)
        hx.append(mlstm_call(dd, zx, bias, st)[0])
    filt = (b_f_w1[0], b_f_b1[0], b_f_w2[0], b_f_b2[0], b_f_w3[0], b_f_b3[0], b_f_w4[0], b_f_freq[0])
    yx = hyena_long(ux, filt, b_d[0])
    yc = hyena_small(uc, *hy_filter_call(ctx_len, *filt), b_d[0])
    w_out = ab_w_out[0].astype(BF16)
    g = a_norm_g[0].reshape(1, MIX_W)
    x = outproj_call("sigmoid", hx[0], hx[1], zx, E_MO // MIX_W, yx, g, w_out, x, mx(0, 2))
    ctx = outproj_call("sigmoid", hc[0], hc[1], zc, E_MO // MIX_W, yc, g, w_out, ctx, mc(0, 2))
    x = ffn(0, x, mx)
    ctx = ffn(0, ctx, mc)

    w_in = pack_odd(cd_w_in[0])
    zx, qkv_x = inproj_call(True, x, mx(1, 0), mx(1, 1), *w_in, c_conv_w[0], c_conv_b[0])
    zc, qkv_c = inproj_call(True, ctx, mc(1, 0), mc(1, 1), *w_in, c_conv_w[0], c_conv_b[0])
    zeros4 = jnp.zeros((2, 1, N_HEADS), F32)
    dtb = jnp.concatenate([c_dt_bias[0][:, None, :], zeros4], axis=-1)
    alog = jnp.concatenate([c_a_log[0][:, None, :], zeros4], axis=-1)
    ox = []
    for dd in range(2):
        _, s_ctx = gdn_call(dd, qkv_c, zc[..., O_GATE:O_GATE + 16], dtb, alog, zero_state[0])
        ox.append(gdn_call(dd, qkv_x, zx[..., O_GATE:O_GATE + 16], dtb, alog, s_ctx)[0])
    tabs = rope_tables(seq)
    kvg = d_kv_norm_g[0].reshape(1, -1)
    w_kv = d_w_kv_up[0].astype(BF16)
    kn_g = _padcols(d_kn_g[0].reshape(1, -1), MLA_QK_PAD)
    kc, vc = kv_call(zc, False, kvg, w_kv, kn_g, tuple(t[:ctx_len] for t in tabs))
    kx, vx = kv_call(zx, True, kvg, w_kv, kn_g, tabs)
    w_q = jnp.concatenate([_padcols(d_w_q_up[0][:, h * MLA_QK:(h + 1) * MLA_QK], MLA_QK_PAD)
                           for h in range(N_HEADS)], axis=1).astype(BF16)
    qx = q_call(zx, d_q_norm_g[0].reshape(1, -1), w_q, _padcols(d_qn_g[0].reshape(1, -1), MLA_QK_PAD), tabs)
    ax = attn_call(qx, kc, vc, kx, vx)
    g = jnp.tile(c_norm_g[0], N_HEADS).reshape(1, MIX_W)
    x = outproj_call("silu", ox[0], ox[1], zx, O_GG // MIX_W, ax, g, cd_w_out[0].astype(BF16), x, mx(1, 2))
    x = ffn(1, x, mx)
    return x
```

```python
import functools
import math

import numpy as np
import jax
import jax.numpy as jnp
from jax import lax
from jax.experimental import pallas as pl
from jax.experimental.pallas import tpu as pltpu

F32 = jnp.float32
BF16 = jnp.bfloat16
HI = lax.Precision.HIGHEST
LOG2E = math.log2(math.e)

D_MODEL = 1024
GRID_W = 64
EPS = 1e-6
CHUNK = 64
ROPE_THETA = 10000.0
N_HEADS = 4
DH = 128
MIX_W = N_HEADS * DH
HYENA_W = 512
HYENA_EMB = 33
HYENA_BANDS = 16
HYENA_FFN = 64
HYENA_NFILT = 4 * HYENA_W
HYENA_FAST_DECAY = 0.3
HYENA_SLOW_DECAY = 1.5
HYENA_TARGET = 1e-2
MLA_NOPE = 128
MLA_ROPE = 64
MLA_QK = 192
MLA_QK_PAD = 256
MLA_Q_RANK = 384
MLA_KV_RANK = 256
D_FF = 2816
FF_TILE = 256
N_FF_TILES = D_FF // FF_TILE
FFT_N1 = 128
SCAN_TB = 256
ATTN_TQ = 512
ATTN_TK = 2048
VMEM_LIMIT = 56 * 1024 * 1024

E_QKV, E_MO, E_GATE, E_N = 0, 1536, 2048, 2176
O_GG, O_LKV, O_LQ, O_LKR, O_GATE, O_N = 0, 512, 768, 1152, 1280, 1408


def _cparams(sem):
    return pltpu.CompilerParams(dimension_semantics=sem, vmem_limit_bytes=VMEM_LIMIT)


def _dot(a, b):
    return jnp.dot(a.astype(BF16), b.astype(BF16), preferred_element_type=F32)


def _dot_nt(a, b):
    return lax.dot_general(a.astype(BF16), b.astype(BF16), (((1,), (1,)), ((), ())), preferred_element_type=F32)


def _dot_tn(a, b):
    return lax.dot_general(a.astype(BF16), b.astype(BF16), (((0,), (0,)), ((), ())), preferred_element_type=F32)


def _dot_hi(a, b):
    return jnp.dot(a, b, precision=HI, preferred_element_type=F32)


def _rms(x, n=None):
    n = x.shape[-1] if n is None else n
    return x * lax.rsqrt(jnp.sum(x * x, axis=-1, keepdims=True) * (1.0 / n) + EPS)


def _sigmoid(x):
    return 0.5 + 0.5 * jnp.tanh(0.5 * x)


def _silu(x):
    return x * _sigmoid(x)


def _softplus(x):
    return jnp.maximum(x, 0.0) + jnp.log(1.0 + jnp.exp(-jnp.abs(x)))


def _log_sigmoid(x):
    return -_softplus(-x)


def _mods_body(c_ref, w_ref, b_ref, o_ref):
    o_ref[0] = _dot_hi(_silu(c_ref[...]), w_ref[0]) + b_ref[0]


def mods_call(cvec, mod_w, mod_b):
    depth, d, n = mod_w.shape
    tn = 3072
    return pl.pallas_call(
        _mods_body,
        grid=(depth, n // tn),
        in_specs=[pl.BlockSpec((8, d), lambda l, j: (0, 0)),
                  pl.BlockSpec((1, d, tn), lambda l, j: (l, 0, j)),
                  pl.BlockSpec((1, 1, tn), lambda l, j: (l, 0, j))],
        out_specs=pl.BlockSpec((1, 8, tn), lambda l, j: (l, 0, j)),
        out_shape=jax.ShapeDtypeStruct((depth, 8, n), F32),
        compiler_params=_cparams(("parallel", "parallel")),
        name="mods",
    )(cvec, mod_w, mod_b.reshape(depth, 1, n))


def _inproj_body(gdn, x_ref, xp_ref, xn_ref, sh_ref, sc_ref, wr_ref, wc_ref, cw_ref, cb_ref, z_ref, u_ref):
    i, nt = pl.program_id(1), pl.num_programs(1)
    sc, sh = 1.0 + sc_ref[0], sh_ref[0]
    tm = x_ref.shape[1]
    mod = lambda v: _rms(v) * sc + sh
    hp = mod(xp_ref[0]) * (i > 0).astype(F32)
    hn = mod(xn_ref[0]) * (i < nt - 1).astype(F32)
    h = jnp.concatenate([hp, mod(x_ref[0]), hn], axis=0).astype(BF16)
    rows = tm + 16
    z_ref[0] = jnp.dot(h[8:8 + tm], wr_ref[...], preferred_element_type=F32)
    c = jnp.dot(h, wc_ref[...], preferred_element_type=F32)
    y = pltpu.roll(c, 1, 0) * cw_ref[0:1, :] + c * cw_ref[1:2, :] + pltpu.roll(c, rows - 1, 0) * cw_ref[2:3, :]
    y = (y + cb_ref[...])[8:8 + tm]
    if not gdn:
        u_ref[0] = y
        return
    y = _silu(y)
    for hd in range(2 * N_HEADS):
        s = y[:, hd * DH:(hd + 1) * DH]
        s = s * lax.rsqrt(jnp.sum(s * s, axis=-1, keepdims=True) + EPS)
        if hd < N_HEADS:
            s = s * DH ** -0.5
        u_ref[0, :, hd * DH:(hd + 1) * DH] = s
    u_ref[0, :, 2 * MIX_W:] = y[:, 2 * MIX_W:]


def inproj_call(gdn, x, shift, scale, w_rest, w_conv, conv_w, conv_b):
    bn, t, d = x.shape
    n, c = w_rest.shape[1], w_conv.shape[1]
    tm = min(512, t)
    r8 = tm // 8
    vec = pl.BlockSpec((1, 1, d), lambda b, i: (b, 0, 0))
    full = lambda a: pl.BlockSpec(a.shape, lambda b, i: (0,) * a.ndim)
    cb = conv_b.reshape(1, c)
    return pl.pallas_call(
        functools.partial(_inproj_body, gdn),
        grid=(bn, t // tm),
        in_specs=[pl.BlockSpec((1, tm, d), lambda b, i: (b, i, 0)),
                  pl.BlockSpec((1, 8, d), lambda b, i: (b, jnp.maximum(i * r8 - 1, 0), 0)),
                  pl.BlockSpec((1, 8, d), lambda b, i: (b, jnp.minimum((i + 1) * r8, t // 8 - 1), 0)),
                  vec, vec, full(w_rest), full(w_conv), full(conv_w), full(cb)],
        out_specs=[pl.BlockSpec((1, tm, n), lambda b, i: (b, i, 0)), pl.BlockSpec((1, tm, c), lambda b, i: (b, i, 0))],
        out_shape=[jax.ShapeDtypeStruct((bn, t, n), F32), jax.ShapeDtypeStruct((bn, t, c), F32)],
        compiler_params=_cparams(("parallel", "parallel")),
        name="inproj_gdn" if gdn else "inproj_hyena",
    )(x, x, x, shift, scale, w_rest, w_conv, conv_w, cb)


def _block_masks(d, tb):
    sh = CHUNK.bit_length() - 1
    ii = lax.broadcasted_iota(jnp.int32, (tb, tb), 0)
    jj = lax.broadcasted_iota(jnp.int32, (tb, tb), 1)
    same = (ii >> sh) == (jj >> sh)
    vis = jnp.logical_and(same, (ii >= jj) if d == 0 else (ii <= jj))
    vis_t = jnp.logical_and(same, (jj >= ii) if d == 0 else (jj <= ii))
    return ii, jj, same, vis, vis_t


def _chunk_order(d, nc):
    return range(nc) if d == 0 else reversed(range(nc))


def _lockstep(gens):
    gens = list(gens)
    results = [None] * len(gens)
    live = list(range(len(gens)))
    while live:
        for i in list(live):
            try:
                next(gens[i])
            except StopIteration as stop:
                results[i] = stop.value
                live.remove(i)
    return results


def _mlstm_block(d, q, k, v, li_c, lf_c, li_r, lf_r, c_mat, n_vec, m):
    tb = q.shape[0]
    ii = lax.broadcasted_iota(jnp.int32, (tb, tb), 0)
    jj = lax.broadcasted_iota(jnp.int32, (tb, tb), 1)
    vis = (ii >= jj) if d == 0 else (ii <= jj)
    vis_t = (jj >= ii) if d == 0 else (jj <= ii)
    b_c = jnp.sum(vis.astype(F32) * lf_r, axis=1, keepdims=True)
    b_r = jnp.sum(vis_t.astype(F32) * lf_c, axis=0, keepdims=True)
    b_last = jnp.sum(lf_r, axis=1, keepdims=True)
    a_c = b_last - b_c + li_c
    m_loc = jnp.max(a_c, axis=0, keepdims=True)
    dmat = jnp.where(vis, b_c - b_r + li_r, -jnp.inf)
    m_in = jnp.max(dmat, axis=1, keepdims=True)
    yield
    s = _dot_nt(q, k) * jnp.exp(dmat - m_in)
    q_c = _dot_nt(q, c_mat)
    yield
    num_in = _dot(s, v)
    den_in = jnp.sum(s, axis=1, keepdims=True)
    m_inter = b_c + m
    m_t = jnp.maximum(m_inter, m_in)
    s_inter, s_in = jnp.exp(m_inter - m_t), jnp.exp(m_in - m_t)
    yield
    num = s_inter * q_c + s_in * num_in
    den = s_inter * jnp.sum(q * n_vec, axis=1, keepdims=True) + s_in * den_in
    h = num / jnp.maximum(jnp.abs(den), jnp.exp(-m_t))
    m_new = jnp.maximum(b_last + m, m_loc)
    f_old, f_loc = jnp.exp(b_last + m - m_new), jnp.exp(m_loc - m_new)
    kw = k * jnp.exp(a_c - m_loc)
    yield
    c_mat = f_old * c_mat + f_loc * _dot_tn(v, kw)
    n_vec = f_old * n_vec + f_loc * jnp.sum(kw, axis=0, keepdims=True)
    return h, c_mat, n_vec, m_new


def _mlstm_body(d, q_ref, k_ref, v_ref, gc_ref, gr_ref, bias_ref, c0_ref, n0_ref, m0_ref,
                h_ref, c_ref, n_ref, m_ref):
    @pl.when(pl.program_id(1) == 0)
    def _():
        c_ref[...] = c0_ref[...]
        n_ref[...] = n0_ref[...]
        m_ref[...] = m0_ref[...]

    gc = gc_ref[0, 0]
    gr = gr_ref[0, 0]

    def head(h):
        hs = slice(h * DH, (h + 1) * DH)
        bi = bias_ref[0, :, h:h + 1]
        bf = bias_ref[0, :, N_HEADS + h:N_HEADS + h + 1]
        li_c = gc[:, h:h + 1] + bi
        lf_c = _log_sigmoid(gc[:, N_HEADS + h:N_HEADS + h + 1] + bf)
        li_r = gr[h:h + 1, :] + bi
        lf_r = _log_sigmoid(gr[N_HEADS + h:N_HEADS + h + 1, :] + bf)
        return _mlstm_block(d, q_ref[0, :, hs], k_ref[0, :, hs] * DH ** -0.5, v_ref[0, :, hs],
                            li_c, lf_c, li_r, lf_r, c_ref[0, h], n_ref[0, h], m_ref[0, h][:, 0:1])

    for h, (hv, c_new, n_new, m_new) in enumerate(_lockstep(head(h) for h in range(N_HEADS))):
        h_ref[0, :, h * DH:(h + 1) * DH] = hv
        c_ref[0, h] = c_new
        n_ref[0, h] = n_new
        m_ref[0, h] = jnp.broadcast_to(m_new, (1, DH))


def _gdn_block(d, q, k, v, bt_c, la_c, la_r, s_mat):
    tb = q.shape[0]
    nc = tb // CHUNK
    sh = CHUNK.bit_length() - 1
    ii, jj, same, vis, vis_t = _block_masks(d, tb)
    g_c = jnp.sum(vis.astype(F32) * la_r, axis=1, keepdims=True)
    g_r = jnp.sum(vis_t.astype(F32) * la_c, axis=0, keepdims=True)
    gl_c = jnp.sum(same.astype(F32) * la_r, axis=1, keepdims=True)
    decay = jnp.exp(jnp.where(vis, g_c - g_r, -jnp.inf))
    kb = k * bt_c
    yield
    both = _dot_nt(jnp.concatenate([kb, q], axis=0), k)
    a_mat = jnp.where(ii == jj, 0.0, both[:tb] * decay)
    attn = both[tb:] * decay
    yield
    off = lambda lvl: jnp.logical_and((ii >> (lvl + 1)) == (jj >> (lvl + 1)), (ii >> lvl) != (jj >> lvl))
    t_mat = (ii == jj).astype(F32) - jnp.where(off(0), a_mat, 0.0)
    for lvl in range(1, sh):
        te = _dot(t_mat, jnp.where(off(lvl), a_mat, 0.0))
        yield
        t_mat = t_mat - _dot(te, t_mat)
        yield
    eg = jnp.exp(g_c)
    uw = _dot(t_mat, jnp.concatenate([v * bt_c, kb * eg], axis=1))
    u, w = uw[:, :DH], uw[:, DH:]
    kd = k * jnp.exp(gl_c - g_c)
    qg = q * eg
    yield
    v_new, o_state = [None] * nc, [None] * nc
    for c in _chunk_order(d, nc):
        sl = slice(c * CHUNK, (c + 1) * CHUNK)
        ws = _dot(jnp.concatenate([w[sl], qg[sl]], axis=0), s_mat)
        v_new[c] = u[sl] - ws[:CHUNK]
        o_state[c] = ws[CHUNK:]
        yield
        s_mat = s_mat * jnp.exp(gl_c[c * CHUNK:c * CHUNK + 1, :]) + _dot_tn(kd[sl], v_new[c])
        yield
    o = jnp.concatenate(o_state, axis=0) + _dot(attn, jnp.concatenate(v_new, axis=0))
    return o, s_mat


def _gdn_body(d, q_ref, k_ref, v_ref, gc_ref, gr_ref, dtb_ref, alog_ref, s0_ref, o_ref, s_ref):
    @pl.when(pl.program_id(1) == 0)
    def _():
        s_ref[...] = s0_ref[...]

    gc = gc_ref[0, 0]
    gr = gr_ref[0, 0]

    def head(h):
        hs = slice(h * DH, (h + 1) * DH)
        dtb = dtb_ref[0, :, h:h + 1]
        neg_a = -jnp.exp(alog_ref[0, :, h:h + 1])
        bt_c = _sigmoid(gc[:, h:h + 1])
        la_c = neg_a * _softplus(gc[:, N_HEADS + h:N_HEADS + h + 1] + dtb)
        la_r = neg_a * _softplus(gr[N_HEADS + h:N_HEADS + h + 1, :] + dtb)
        return _gdn_block(d, q_ref[0, :, hs], k_ref[0, :, hs], v_ref[0, :, hs], bt_c, la_c, la_r, s_ref[0, h])

    for h, (ov, s_new) in enumerate(_lockstep(head(h) for h in range(N_HEADS))):
        o_ref[0, :, h * DH:(h + 1) * DH] = ov
        s_ref[0, h] = s_new


def _gate_views(graw):
    bsz, t, _ = graw.shape
    gcol = graw.reshape(bsz, t, 2, 2, N_HEADS).transpose(0, 3, 1, 2, 4).reshape(bsz, 2, t, 8)
    return gcol, gcol.transpose(0, 1, 3, 2)


def _scan_specs(d, t, col0):
    tb = min(SCAN_TB, t)
    nb = t // tb
    blk = (lambda i: i) if d == 0 else (lambda i: nb - 1 - i)
    qkv = [pl.BlockSpec((1, tb, MIX_W), lambda b, i, j=j: (b, blk(i), col0 + j)) for j in range(3)]
    gates = [pl.BlockSpec((1, 1, tb, 8), lambda b, i: (b, d, blk(i), 0)),
             pl.BlockSpec((1, 1, 8, tb), lambda b, i: (b, d, 0, blk(i)))]
    par = pl.BlockSpec((1, 1, 8), lambda b, i: (d, 0, 0))
    out = pl.BlockSpec((1, tb, MIX_W), lambda b, i: (b, blk(i), 0))
    return nb, qkv, gates, par, out


def _state_spec(*tail):
    return pl.BlockSpec((1, N_HEADS) + tail, lambda b, i: (b, 0) + (0,) * len(tail))


def mlstm_call(d, z, bias, state):
    bsz, t, _ = z.shape
    nb, qkv, gates, par, out = _scan_specs(d, t, E_QKV // MIX_W)
    gcol, grow = _gate_views(z[..., E_GATE:E_GATE + 16])
    st_specs = [_state_spec(DH, DH), _state_spec(1, DH), _state_spec(1, DH)]
    st_shapes = [jax.ShapeDtypeStruct((bsz, N_HEADS, DH, DH), F32),
                 jax.ShapeDtypeStruct((bsz, N_HEADS, 1, DH), F32),
                 jax.ShapeDtypeStruct((bsz, N_HEADS, 1, DH), F32)]
    res = pl.pallas_call(
        functools.partial(_mlstm_body, d),
        grid=(bsz, nb),
        in_specs=qkv + gates + [par] + st_specs,
        out_specs=[out] + st_specs,
        out_shape=[jax.ShapeDtypeStruct((bsz, t, MIX_W), F32)] + st_shapes,
        compiler_params=_cparams(("parallel", "arbitrary")),
        name="mlstm_scan_fwd" if d == 0 else "mlstm_scan_bwd",
    )(z, z, z, gcol, grow, bias, *state)
    return res[0], tuple(res[1:])


def gdn_call(d, qkv_arr, graw, dtb, alog, state):
    bsz, t, _ = qkv_arr.shape
    nb, qkv, gates, par, out = _scan_specs(d, t, 0)
    gcol, grow = _gate_views(graw)
    st_spec = _state_spec(DH, DH)
    return pl.pallas_call(
        functools.partial(_gdn_body, d),
        grid=(bsz, nb),
        in_specs=qkv + gates + [par, par, st_spec],
        out_specs=[out, st_spec],
        out_shape=[jax.ShapeDtypeStruct((bsz, t, MIX_W), F32),
                   jax.ShapeDtypeStruct((bsz, N_HEADS, DH, DH), F32)],
        compiler_params=_cparams(("parallel", "arbitrary")),
        name="gdn_scan_fwd" if d == 0 else "gdn_scan_bwd",
    )(qkv_arr, qkv_arr, qkv_arr, gcol, grow, dtb, alog, state)


def _outproj_body(act, hf_ref, hb_ref, gate_ref, y_ref, g_ref, w_ref, x_ref, mg_ref, o_ref):
    hs = hf_ref[0] + hb_ref[0]
    hn = jnp.concatenate([_rms(hs[:, h * DH:(h + 1) * DH]) for h in range(N_HEADS)], axis=-1) * g_ref[...]
    gate = gate_ref[0]
    hn = hn * (_sigmoid(gate) if act == "sigmoid" else _silu(gate))
    out = _dot(hn, w_ref[0:MIX_W, :]) + _dot(y_ref[0], w_ref[MIX_W:, :])
    o_ref[0] = x_ref[0] + mg_ref[0] * out


def outproj_call(act, hf, hb, z, gate_block, y, g, w, x, mgate):
    bn, t, d = x.shape
    tm = min(512, t)
    return pl.pallas_call(
        functools.partial(_outproj_body, act),
        grid=(bn, t // tm),
        in_specs=[pl.BlockSpec((1, tm, MIX_W), lambda b, i: (b, i, 0)),
                  pl.BlockSpec((1, tm, MIX_W), lambda b, i: (b, i, 0)),
                  pl.BlockSpec((1, tm, MIX_W), lambda b, i: (b, i, gate_block)),
                  pl.BlockSpec((1, tm, MIX_W), lambda b, i: (b, i, 0)),
                  pl.BlockSpec((1, MIX_W), lambda b, i: (0, 0)),
                  pl.BlockSpec((2 * MIX_W, d), lambda b, i: (0, 0)),
                  pl.BlockSpec((1, tm, d), lambda b, i: (b, i, 0)),
                  pl.BlockSpec((1, 1, d), lambda b, i: (b, 0, 0))],
        out_specs=pl.BlockSpec((1, tm, d), lambda b, i: (b, i, 0)),
        out_shape=jax.ShapeDtypeStruct((bn, t, d), F32),
        compiler_params=_cparams(("parallel", "parallel")),
        name="outproj_" + act,
    )(hf, hb, z, y, g, w, x, mgate)


def _ffn_body(x_ref, xp_ref, xn_ref, sh_ref, sc_ref, mg_ref, wu_ref, cw_ref, wd_ref, o_ref):
    i, nt = pl.program_id(1), pl.num_programs(1)
    sc, sh = 1.0 + sc_ref[0], sh_ref[0]
    x = x_ref[0]
    tm = x.shape[0]
    mod = lambda v: _rms(v) * sc + sh
    hp = mod(xp_ref[0]) * (i > 0).astype(F32)
    hn = mod(xn_ref[0]) * (i < nt - 1).astype(F32)
    h = jnp.concatenate([hp, mod(x), hn], axis=0).astype(BF16)
    rows = tm + 16

    cols = lambda j, half: slice(half * D_FF + j * FF_TILE, half * D_FF + (j + 1) * FF_TILE)
    up = lambda j: tuple(jnp.dot(h, wu_ref[:, cols(j, half)], preferred_element_type=F32) for half in range(2))

    def conv(u, cw):
        y = pltpu.roll(u, 1, 0) * cw[0:1, :] + u * cw[1:2, :] + pltpu.roll(u, rows - 1, 0) * cw[2:3, :] + cw[3:4, :]
        return y[8:8 + tm]

    acc = jnp.zeros(x.shape, F32)
    u_next = up(0)
    for j in range(N_FF_TILES):
        ua, ub = u_next
        if j + 1 < N_FF_TILES:
            u_next = up(j + 1)
        a, b = conv(ua, cw_ref[:, cols(j, 0)]), conv(ub, cw_ref[:, cols(j, 1)])
        acc = acc + _dot(_silu(a) * b, wd_ref[j * FF_TILE:(j + 1) * FF_TILE, :])
    o_ref[0] = x + mg_ref[0] * acc


def ffn_call(x, shift, scale, mgate, w_up, conv_w, conv_b, w_down):
    bn, t, d = x.shape
    tm = min(512, t)
    r8 = tm // 8
    wu, wd = w_up.astype(BF16), w_down.astype(BF16)
    cw = jnp.concatenate([conv_w, conv_b[None]], axis=0)
    full = lambda a: pl.BlockSpec(a.shape, lambda b, i: (0,) * a.ndim)
    vec = pl.BlockSpec((1, 1, d), lambda b, i: (b, 0, 0))
    return pl.pallas_call(
        _ffn_body,
        grid=(bn, t // tm),
        in_specs=[pl.BlockSpec((1, tm, d), lambda b, i: (b, i, 0)),
                  pl.BlockSpec((1, 8, d), lambda b, i: (b, jnp.maximum(i * r8 - 1, 0), 0)),
                  pl.BlockSpec((1, 8, d), lambda b, i: (b, jnp.minimum((i + 1) * r8, t // 8 - 1), 0)),
                  vec, vec, vec, full(wu), full(cw), full(wd)],
        out_specs=pl.BlockSpec((1, tm, d), lambda b, i: (b, i, 0)),
        out_shape=jax.ShapeDtypeStruct((bn, t, d), F32),
        compiler_params=_cparams(("parallel", "parallel")),
        name="conv_ffn",
    )(x, x, x, shift, scale, mgate, wu, cw, wd)


def _filter_hidden(lf, n, w1t_ref, w1c_ref, w1s_ref, b1_ref, w2_ref, b2_ref, w3_ref, b3_ref, fr_ref, band_ref):
    t = jnp.where(n < lf, n, 2 * lf - n).astype(F32)
    t_norm = t / (lf - 1)
    ang = (2.0 * math.pi * t / lf) * band_ref[...]
    fr = fr_ref[...]
    pre = t_norm * w1t_ref[...] + _dot_hi(jnp.cos(ang), w1c_ref[...]) - _dot_hi(jnp.sin(ang), w1s_ref[...])
    hdn = jnp.sin(fr * (pre + b1_ref[...]))
    hdn = jnp.sin(fr * (_dot_hi(hdn, w2_ref[...]) + b2_ref[...]))
    hdn = jnp.sin(fr * (_dot_hi(hdn, w3_ref[...]) + b3_ref[...]))
    return hdn, t_norm


def _filter_values(lf, n, hdn, t_norm, w4, dl):
    hval = _dot(hdn, w4) * jnp.exp(-t_norm * jnp.abs(dl))
    return jnp.where(n == lf, 0.0, hval)


def _hy_filter_body(lf, *refs):
    mlp_refs, (w4_ref, dl_ref, o_ref, l1_ref) = refs[:-4], refs[-4:]
    r = pl.program_id(0)
    tr = o_ref.shape[0]
    n = (r * tr + lax.broadcasted_iota(jnp.int32, (tr, 1), 0))
    hdn, t_norm = _filter_hidden(lf, n, *mlp_refs)
    hval = _filter_values(lf, n, hdn, t_norm, w4_ref[0], dl_ref[0])
    o_ref[...] = hval

    @pl.when(r == 0)
    def _():
        l1_ref[...] = jnp.zeros_like(l1_ref)

    l1_ref[...] += jnp.sum(jnp.abs(hval), axis=0, keepdims=True)


def _hy_filter_lead_body(lf, *refs):
    mlp_refs, (w4_ref, dl_ref, f_ref, o_ref, l1_ref) = refs[:-5], refs[-5:]
    jb = pl.program_id(0)
    half = FFT_N1 // 2
    per_dir = LEAD_ROWS * half
    sh_half, sh_dir = half.bit_length() - 1, per_dir.bit_length() - 1
    ncol = l1_ref.shape[-1]
    r = lax.broadcasted_iota(jnp.int32, (2 * per_dir, 1), 0)
    dd, rem = r >> sh_dir, r & (per_dir - 1)
    n = FFT_N1 * ((dd << sh_half) + (rem & (half - 1))) + LEAD_ROWS * jb + (rem >> sh_half)
    hdn, t_norm = _filter_hidden(lf, n, *mlp_refs)
    vals = []
    for d in range(2):
        rows = slice(d * per_dir, (d + 1) * per_dir)
        vals.append(_filter_values(lf, n[rows], hdn[rows], t_norm[rows], w4_ref[d], dl_ref[d]))

    @pl.when(jb == 0)
    def _():
        l1_ref[...] = jnp.zeros_like(l1_ref)

    l1_ref[...] += sum(jnp.sum(jnp.abs(v), axis=0, keepdims=True) for v in vals)
    for j in range(LEAD_ROWS):
        slab = jnp.concatenate([v[j * half:(j + 1) * half] for v in vals], axis=0)
        o_ref[:, j * ncol:(j + 1) * ncol] = _dot(f_ref[...], slab).astype(o_ref.dtype)


def _filter_args(f_w1, f_b1, f_w2, f_b2, f_w3, f_b3, f_w4, f_freq):
    ncol = 2 * HYENA_W
    pad = lambda a: jnp.concatenate([a, jnp.zeros((DH - a.shape[0], a.shape[1]), F32)], axis=0)
    w4 = f_w4.reshape(HYENA_FFN, 2, 2, HYENA_W).transpose(2, 0, 1, 3).reshape(2, HYENA_FFN, ncol)
    deltas = jnp.linspace(math.log(HYENA_TARGET) / HYENA_SLOW_DECAY, math.log(HYENA_TARGET) / HYENA_FAST_DECAY,
                          HYENA_NFILT, dtype=F32)
    deltas = deltas.reshape(2, 2, HYENA_W).transpose(1, 0, 2).reshape(2, 1, ncol)
    bands = jnp.linspace(1e-4, HYENA_BANDS - 1, HYENA_BANDS, dtype=F32)
    bands = jnp.concatenate([bands, jnp.zeros((DH - HYENA_BANDS,), F32)]).reshape(1, DH)
    row = lambda a: a.reshape(1, -1)
    mlp = (f_w1[0:1], pad(f_w1[1:1 + HYENA_BANDS]), pad(f_w1[1 + HYENA_BANDS:]), row(f_b1), f_w2, row(f_b2),
           f_w3, row(f_b3), row(f_freq), bands)
    return mlp, w4, deltas


def hy_filter_call(lf, *filt):
    tr = min(512, lf)
    nt = 2 * lf // tr
    half = lf // tr
    ncol = 2 * HYENA_W
    mlp, w4, deltas = _filter_args(*filt)
    full = lambda a: pl.BlockSpec(a.shape, lambda r: (0,) * a.ndim)
    by_dir = lambda a: pl.BlockSpec((1,) + a.shape[1:], lambda r: (r // half, 0, 0))
    return pl.pallas_call(
        functools.partial(_hy_filter_body, lf),
        grid=(nt,),
        in_specs=[full(a) for a in mlp] + [by_dir(w4), by_dir(deltas)],
        out_specs=[pl.BlockSpec((tr, ncol), lambda r: (r, 0)), pl.BlockSpec((1, ncol), lambda r: (0, 0))],
        out_shape=[jax.ShapeDtypeStruct((2 * lf, ncol), F32), jax.ShapeDtypeStruct((1, ncol), F32)],
        compiler_params=_cparams(("arbitrary",)),
        name="hyena_filter",
    )(*mlp, w4, deltas)


def hy_filter_lead_call(lf, f_flt, *filt):
    ncol = 2 * HYENA_W
    mlp, w4, deltas = _filter_args(*filt)
    full = lambda a: pl.BlockSpec(a.shape, lambda j: (0,) * a.ndim)
    return pl.pallas_call(
        functools.partial(_hy_filter_lead_body, lf),
        grid=(FFT_N1 // LEAD_ROWS,),
        in_specs=[full(a) for a in mlp + (w4, deltas, f_flt)],
        out_specs=[pl.BlockSpec((f_flt.shape[0], LEAD_ROWS * ncol), lambda j: (0, j)),
                   pl.BlockSpec((1, ncol), lambda j: (0, 0))],
        out_shape=[jax.ShapeDtypeStruct((f_flt.shape[0], FFT_N1 * ncol), BF16), jax.ShapeDtypeStruct((1, ncol), F32)],
        compiler_params=_cparams(("arbitrary",)),
        name="hyena_filter_lead",
    )(*mlp, w4, deltas, f_flt)


def _dft_consts(lf):
    n1 = FFT_N1
    n = 2 * lf
    assert n == n1 * n1
    idx = np.arange(n1)
    f = np.exp(-2j * np.pi * np.outer(idx, idx) / n1)
    tw = np.exp(-2j * np.pi * np.outer(idx, idx) / n)
    fh = f[:, :n1 // 2]
    blk = lambda a: np.block([[a.real, -a.imag], [a.imag, a.real]])
    f_sig = blk(fh)
    f_flt = np.concatenate([f.real, f.imag], axis=0)
    c_inv = blk(np.conj(f)[:n1 // 2, :] / n)
    c = lambda a: jnp.asarray(a, dtype=F32)
    cb = lambda a: jnp.asarray(a, dtype=F32).astype(BF16)
    return dict(f_sig=cb(f_sig), f_flt=cb(f_flt), c_inv=cb(c_inv), fr=c(f.real), fi=c(f.imag),
                twr=c(tw.real.reshape(n1, 1, n1)), twi=c(tw.imag.reshape(n1, 1, n1)))


LEAD_ROWS = 8


def _lead_mm_body(m_ref, x_ref, o_ref):
    w = x_ref.shape[-1]
    for j in range(LEAD_ROWS):
        o_ref[:, j * w:(j + 1) * w] = _dot(m_ref[...], x_ref[:, j, :]).astype(o_ref.dtype)


def lead_mm_call(mat, x3, group, w, name):
    r, k = mat.shape
    n1 = x3.shape[1]
    return pl.pallas_call(
        _lead_mm_body,
        grid=(n1 // LEAD_ROWS,),
        in_specs=[pl.BlockSpec((r, k), lambda j: (0, 0)), pl.BlockSpec((k, LEAD_ROWS, w), lambda j: (0, j, group))],
        out_specs=pl.BlockSpec((r, LEAD_ROWS * w), lambda j: (0, j)),
        out_shape=jax.ShapeDtypeStruct((r, n1 * w), BF16),
        compiler_params=_cparams(("parallel",)),
        name=name,
    )(mat, x3)


def _stage2_mat(fr, fi, twr, twi):
    hr = fr * twr - fi * twi
    hi = fr * twi + fi * twr
    return hr, hi


def _blk(ar, ai):
    return jnp.concatenate([jnp.concatenate([ar, -ai], axis=1), jnp.concatenate([ai, ar], axis=1)], axis=0)


SPEC_K1 = 4


def _spec_filter_body(a_ref, l1_ref, fr_ref, fi_ref, twr_ref, twi_ref, o_ref):
    inv_l1 = 1.0 / l1_ref[...]
    for kk in range(SPEC_K1):
        hr, hi = _stage2_mat(fr_ref[...], fi_ref[...], twr_ref[kk], twi_ref[kk])
        n1 = hr.shape[0]
        a = a_ref[:, kk].reshape(2 * n1, a_ref.shape[-1])
        o_ref[kk] = (_dot(_blk(hr, hi), a) * inv_l1).astype(o_ref.dtype)


def _spec_conv_body(a_ref, k_ref, fr_ref, fi_ref, twr_ref, twi_ref, o_ref):
    for kk in range(SPEC_K1):
        hr, hi = _stage2_mat(fr_ref[...], fi_ref[...], twr_ref[kk], twi_ref[kk])
        n1 = hr.shape[0]
        x = _dot(_blk(hr, hi), a_ref[:, kk].reshape(2 * n1, a_ref.shape[-1]))
        xr, xi = x[:n1], x[n1:]
        kr, ki = k_ref[kk, :n1].astype(F32), k_ref[kk, n1:].astype(F32)
        y = jnp.concatenate([xr * kr - xi * ki, xr * ki + xi * kr], axis=0)
        o_ref[kk] = _dot(_blk(hr.T, -(hi.T)), y).astype(o_ref.dtype)


def spec_call(a2d, kspec, order, cst, l1=None):
    n1 = FFT_N1
    c = a2d.shape[1] // n1
    ct = 512 if kspec is not None else min(1024, c)
    a4 = a2d.reshape(2, n1, n1, c)
    sq = pl.BlockSpec((n1, n1), lambda k, j: (0, 0))
    tw = pl.BlockSpec((SPEC_K1, 1, n1), lambda k, j: (k, 0, 0))
    a_spec = pl.BlockSpec((2, SPEC_K1, n1, ct), lambda k, j: (0, k, 0, j))
    o_spec = pl.BlockSpec((SPEC_K1, 2 * n1, ct), lambda k, j: (k, 0, j))
    common = dict(grid=(n1 // SPEC_K1, c // ct), out_specs=o_spec,
                  out_shape=jax.ShapeDtypeStruct((n1, 2 * n1, c), BF16),
                  compiler_params=_cparams(("parallel", "parallel")))
    if kspec is None:
        l1_spec = pl.BlockSpec((1, ct), lambda k, j: (0, j))
        return pl.pallas_call(_spec_filter_body, in_specs=[a_spec, l1_spec, sq, sq, tw, tw], name="hyena_spec_filter",
                              **common)(a4, l1, cst["fr"], cst["fi"], cst["twr"], cst["twi"])
    k_spec = pl.BlockSpec((SPEC_K1, 2 * n1, ct), lambda k, j: (k, 0, order * (c // ct) + j))
    return pl.pallas_call(_spec_conv_body, in_specs=[a_spec, k_spec, sq, sq, tw, tw], name="hyena_spec_conv",
                          **common)(a4, kspec, cst["fr"], cst["fi"], cst["twr"], cst["twi"])


def _inv_gate_body(m_ref, br_ref, bi_ref, v_ref, x_ref, d_ref, o_ref):
    w = o_ref.shape[-1]
    y = _dot(m_ref[...], jnp.concatenate([br_ref[...], bi_ref[...]], axis=0))
    for j in range(LEAD_ROWS):
        o_ref[:, j, :] = x_ref[:, j, :] * (y[:, j * w:(j + 1) * w] + v_ref[:, j, :] * d_ref[...])


def inv_gate_call(b3, v3, gv, x3, gx, d_row, cst):
    n1 = FFT_N1
    w = HYENA_W
    nb = n1 // LEAD_ROWS
    b2d = b3.reshape(n1, 2 * n1 * w)
    tn = LEAD_ROWS * w
    rows3 = lambda g: pl.BlockSpec((n1, LEAD_ROWS, w), lambda j: (0, j, g))
    return pl.pallas_call(
        _inv_gate_body,
        grid=(nb,),
        in_specs=[pl.BlockSpec((n1, 2 * n1), lambda j: (0, 0)), pl.BlockSpec((n1, tn), lambda j: (0, j)),
                  pl.BlockSpec((n1, tn), lambda j: (0, nb + j)), rows3(gv), rows3(gx),
                  pl.BlockSpec((1, w), lambda j: (0, 0))],
        out_specs=rows3(0),
        out_shape=jax.ShapeDtypeStruct((n1, n1, w), F32),
        compiler_params=_cparams(("parallel",)),
        name="hyena_inv_gate",
    )(cst["c_inv"], b2d, b2d, v3, x3, d_row)


def hyena_long(u, filt, d_skip):
    bsz, lf, _ = u.shape
    assert bsz == 2
    n1 = FFT_N1
    w = HYENA_W
    cst = _dft_consts(lf)
    u3 = u.reshape(bsz * n1 // 2, n1, 3 * w)
    a_flt, l1 = hy_filter_lead_call(lf, cst["f_flt"], *filt)
    kspec = spec_call(a_flt, None, 0, cst, l1)
    y3, gy = u3, 0
    for o in range(2):
        a = lead_mm_call(cst["f_sig"], y3, gy, w, "hyena_lead_signal")
        b3 = spec_call(a, kspec, o, cst)
        y3, gy = inv_gate_call(b3, y3, gy, u3, o + 1, d_skip[o:o + 1], cst), 0
    return y3.reshape(bsz, lf, w)


def _hyena_small_body(u_ref, kf_ref, l1_ref, fs_ref, ff_ref, ci_ref, d_ref, o_ref):
    w = HYENA_W
    nk = ff_ref.shape[0] // 2
    kspec = _dot(ff_ref[...], kf_ref[...]) * (1.0 / l1_ref[...])
    ucat = jnp.concatenate([u_ref[0], u_ref[1]], axis=0)
    y = ucat[:, :w]
    for o in range(2):
        x = _dot(fs_ref[...], y)
        xr, xi = x[:nk], x[nk:]
        kr, ki = kspec[:nk, o * w:(o + 1) * w], kspec[nk:, o * w:(o + 1) * w]
        prod = jnp.concatenate([xr * kr - xi * ki, xr * ki + xi * kr], axis=0)
        conv = _dot(ci_ref[...], prod)
        y = ucat[:, (o + 1) * w:(o + 2) * w] * (conv + y * d_ref[o:o + 1, :])
    lf = u_ref.shape[1]
    o_ref[0] = y[:lf]
    o_ref[1] = y[lf:]


def hyena_small(u, kf, l1, d_skip):
    bsz, lf, _ = u.shape
    assert bsz == 2
    n = 2 * lf
    idx = np.arange(n)
    f = np.exp(-2j * np.pi * np.outer(idx, idx) / n)
    blk = lambda a: np.block([[a.real, -a.imag], [a.imag, a.real]])
    cb = lambda a: jnp.asarray(a, F32).astype(BF16)
    fs = cb(blk(f[:, :lf]))
    ff = cb(np.concatenate([f.real, f.imag], axis=0))
    ci = cb(blk(np.conj(f)[:lf, :] / n))
    return pl.pallas_call(
        _hyena_small_body,
        out_shape=jax.ShapeDtypeStruct((bsz, lf, HYENA_W), F32),
        compiler_params=pltpu.CompilerParams(vmem_limit_bytes=VMEM_LIMIT),
        name="hyena_small",
    )(u, kf, l1, fs, ff, ci, d_skip)


def _rope(x, cos_ref, sa_ref, sb_ref):
    w = x.shape[-1]
    half = MLA_ROPE // 2
    return x * cos_ref[...] + pltpu.roll(x, w - half, 1) * sa_ref[...] + pltpu.roll(x, half, 1) * sb_ref[...]


def _kv_body(rope, lkv_ref, lkr_ref, g_ref, w_ref, kg_ref, cos_ref, sa_ref, sb_ref, k_ref, v_ref):
    kv = _dot(_rms(lkv_ref[0]) * g_ref[...], w_ref[...])
    kr = lkr_ref[0]
    for h in range(N_HEADS):
        base = h * (MLA_NOPE + DH)
        k = jnp.concatenate([kv[:, base:base + MLA_NOPE], kr], axis=-1)
        k = _rms(k, MLA_QK) * kg_ref[...]
        if rope:
            k = _rope(k, cos_ref, sa_ref, sb_ref)
        k_ref[0, h] = k.astype(BF16)
        v = kv[:, base + MLA_NOPE:base + MLA_NOPE + DH]
        v_ref[0, h] = jnp.concatenate([v, jnp.ones_like(v)], axis=-1).astype(BF16)


def kv_call(z, rope, g, w, kg, tabs):
    bn, t, _ = z.shape
    tm = min(512, t)
    tab = pl.BlockSpec((tm, MLA_QK_PAD), lambda b, i: (i, 0))
    return pl.pallas_call(
        functools.partial(_kv_body, rope),
        grid=(bn, t // tm),
        in_specs=[pl.BlockSpec((1, tm, MLA_KV_RANK), lambda b, i: (b, i, O_LKV // MLA_KV_RANK)),
                  pl.BlockSpec((1, tm, DH), lambda b, i: (b, i, O_LKR // DH)),
                  pl.BlockSpec((1, MLA_KV_RANK), lambda b, i: (0, 0)),
                  pl.BlockSpec(w.shape, lambda b, i: (0, 0)),
                  pl.BlockSpec((1, MLA_QK_PAD), lambda b, i: (0, 0)), tab, tab, tab],
        out_specs=[pl.BlockSpec((1, N_HEADS, tm, MLA_QK_PAD), lambda b, i: (b, 0, i, 0)),
                   pl.BlockSpec((1, N_HEADS, tm, 2 * DH), lambda b, i: (b, 0, i, 0))],
        out_shape=[jax.ShapeDtypeStruct((bn, N_HEADS, t, MLA_QK_PAD), BF16),
                   jax.ShapeDtypeStruct((bn, N_HEADS, t, 2 * DH), BF16)],
        compiler_params=_cparams(("parallel", "parallel")),
        name="mla_kv_rope" if rope else "mla_kv",
    )(z, z, g, w, kg, *tabs)


def _q_body(lq_ref, g_ref, w_ref, qg_ref, cos_ref, sa_ref, sb_ref, q_ref):
    q = _dot(_rms(lq_ref[0]) * g_ref[...], w_ref[...])
    for h in range(N_HEADS):
        qh = _rms(q[:, h * MLA_QK_PAD:(h + 1) * MLA_QK_PAD], MLA_QK) * qg_ref[...]
        qh = _rope(qh, cos_ref, sa_ref, sb_ref) * (MLA_QK ** -0.5 * LOG2E)
        q_ref[0, h] = qh.astype(BF16)


def q_call(z, g, w, qg, tabs):
    bn, t, _ = z.shape
    tm = min(512, t)
    tab = pl.BlockSpec((tm, MLA_QK_PAD), lambda b, i: (i, 0))
    return pl.pallas_call(
        _q_body,
        grid=(bn, t // tm),
        in_specs=[pl.BlockSpec((1, tm, MLA_Q_RANK), lambda b, i: (b, i, O_LQ // MLA_Q_RANK)),
                  pl.BlockSpec((1, MLA_Q_RANK), lambda b, i: (0, 0)),
                  pl.BlockSpec(w.shape, lambda b, i: (0, 0)),
                  pl.BlockSpec((1, MLA_QK_PAD), lambda b, i: (0, 0)), tab, tab, tab],
        out_specs=pl.BlockSpec((1, N_HEADS, tm, MLA_QK_PAD), lambda b, i: (b, 0, i, 0)),
        out_shape=jax.ShapeDtypeStruct((bn, N_HEADS, t, MLA_QK_PAD), BF16),
        compiler_params=_cparams(("parallel", "parallel")),
        name="mla_q",
    )(z, g, w, qg, *tabs)


def _attn_body(q_ref, kc_ref, vc_ref, kx_ref, vx_ref, o_ref):
    q = q_ref[0, 0]
    tiles = [(kc_ref, vc_ref, 0, kc_ref.shape[2])]
    tk = min(ATTN_TK, kx_ref.shape[2])
    tiles += [(kx_ref, vx_ref, s0, tk) for s0 in range(0, kx_ref.shape[2], tk)]
    scores = lambda t: lax.dot_general(q, t[0][0, 0, t[2]:t[2] + t[3], :], (((1,), (1,)), ((), ())),
                                       preferred_element_type=F32)
    m = jnp.full((q.shape[0], 1), -jnp.inf, F32)
    acc = jnp.zeros((q.shape[0], 2 * DH), F32)
    s_next = scores(tiles[0])
    for idx, t in enumerate(tiles):
        s = s_next
        if idx + 1 < len(tiles):
            s_next = scores(tiles[idx + 1])
        m_new = jnp.maximum(m, jnp.max(s, axis=1, keepdims=True))
        p = jnp.exp2(s - m_new).astype(BF16)
        acc = jnp.exp2(m - m_new) * acc + jnp.dot(p, t[1][0, 0, t[2]:t[2] + t[3], :], preferred_element_type=F32)
        m = m_new
    o_ref[0] = acc[:, :DH] / acc[:, DH:]


def attn_call(q, kc, vc, kx, vx):
    bn, nh, t, dq = q.shape
    tc = kc.shape[2]
    tq = ATTN_TQ
    whole = lambda n, w: pl.BlockSpec((1, 1, n, w), lambda b, h, i: (b, h, 0, 0))
    return pl.pallas_call(
        _attn_body,
        grid=(bn, nh, t // tq),
        in_specs=[pl.BlockSpec((1, 1, tq, dq), lambda b, h, i: (b, h, i, 0)),
                  whole(tc, dq), whole(tc, 2 * DH), whole(t, dq), whole(t, 2 * DH)],
        out_specs=pl.BlockSpec((1, tq, DH), lambda b, h, i: (b, i, h)),
        out_shape=jax.ShapeDtypeStruct((bn, t, nh * DH), F32),
        compiler_params=_cparams(("parallel", "parallel", "parallel")),
        name="mla_attention",
    )(q, kc, vc, kx, vx)


def rope_tables(seq):
    rows = seq // GRID_W
    row = jnp.repeat(jnp.arange(rows), GRID_W).astype(F32)
    col = jnp.tile(jnp.arange(GRID_W), rows).astype(F32)
    n_freq = MLA_ROPE // 4
    inv = ROPE_THETA ** (-jnp.arange(n_freq, dtype=F32) / n_freq)
    ang = jnp.concatenate([row[:, None] * inv, col[:, None] * inv], axis=-1)
    cos, sin = jnp.cos(ang), jnp.sin(ang)
    half = MLA_ROPE // 2
    ones = jnp.ones((seq, MLA_NOPE), F32)
    zn = jnp.zeros((seq, MLA_NOPE), F32)
    zh = jnp.zeros((seq, half), F32)
    zp = jnp.zeros((seq, MLA_QK_PAD - MLA_QK), F32)
    cos_t = jnp.concatenate([ones, cos, cos, zp], axis=-1)
    sin_a = jnp.concatenate([zn, -sin, zh, zp], axis=-1)
    sin_b = jnp.concatenate([zn, zh, sin, zp], axis=-1)
    return cos_t, sin_a, sin_b


def _padcols(w, n):
    return jnp.concatenate([w, jnp.zeros((w.shape[0], n - w.shape[1]), w.dtype)], axis=1)


def pack_even(w):
    q, k, v, o = (w[:, j * MIX_W:(j + 1) * MIX_W] for j in range(4))
    gates = w[:, 4 * MIX_W:4 * MIX_W + 16]
    hy = w[:, 4 * MIX_W + 16:]
    return jnp.concatenate([q, k, v, o, _padcols(gates, DH)], axis=1).astype(BF16), hy.astype(BF16)


def pack_odd(w):
    q, k, v, gg = (w[:, j * MIX_W:(j + 1) * MIX_W] for j in range(4))
    c0 = 4 * MIX_W
    gates = w[:, c0:c0 + 16]
    lq = w[:, c0 + 16:c0 + 16 + MLA_Q_RANK]
    lkv = w[:, c0 + 16 + MLA_Q_RANK:c0 + 16 + MLA_Q_RANK + MLA_KV_RANK]
    lkr = w[:, c0 + 16 + MLA_Q_RANK + MLA_KV_RANK:]
    rest = jnp.concatenate([gg, lkv, lq, _padcols(lkr, DH), _padcols(gates, DH)], axis=1)
    return rest.astype(BF16), jnp.concatenate([q, k, v], axis=1).astype(BF16)


def kernel(x, c, ctx, c_ctx, mod_w, mod_b, a_w_in, a_i_bias, a_f_bias, a_norm_g, b_conv_w, b_conv_b, b_f_w1, b_f_b1, b_f_w2, b_f_b2, b_f_w3, b_f_b3, b_f_w4, b_f_freq, b_d, ab_w_out, cd_w_in, c_conv_w, c_conv_b, c_a_log, c_dt_bias, c_norm_g, d_q_norm_g, d_w_q_up, d_kv_norm_g, d_w_kv_up, d_qn_g, d_kn_g, cd_w_out, ffn_w_up, ffn_conv_w, ffn_conv_b, ffn_w_down):
    bsz, seq, d = x.shape
    ctx_len = ctx.shape[1]
    cvec = jnp.concatenate([c, c_ctx[None], jnp.zeros((8 - bsz - 1, d), F32)], axis=0)
    mods = mods_call(cvec, mod_w, mod_b)
    mx = lambda l, j: mods[l, :bsz, None, j * d:(j + 1) * d]
    mc = lambda l, j: jnp.broadcast_to(mods[l, bsz, j * d:(j + 1) * d], (bsz, 1, d))

    def ffn(l, h, m):
        return ffn_call(h, m(l, 3), m(l, 4), m(l, 5), ffn_w_up[l], ffn_conv_w[l], ffn_conv_b[l], ffn_w_down[l])

    w_in = pack_even(a_w_in[0])
    zx, ux = inproj_call(False, x, mx(0, 0), mx(0, 1), *w_in, b_conv_w[0], b_conv_b[0])
    zc, uc = inproj_call(False, ctx, mc(0, 0), mc(0, 1), *w_in, b_conv_w[0], b_conv_b[0])
    bias = jnp.concatenate([a_i_bias[0].reshape(2, N_HEADS), a_f_bias[0].reshape(2, N_HEADS)], axis=-1)[:, None, :]
    zero_state = (jnp.zeros((bsz, N_HEADS, DH, DH), F32), jnp.zeros((bsz, N_HEADS, 1, DH), F32),
                  jnp.zeros((bsz, N_HEADS, 1, DH), F32))
    hc, hx = [], []
    for dd in range(2):
        h_ctx, st = mlstm_call(dd, zc, bias, zero_state)
        hc.append(h_ctx)
        hx.append(mlstm_call(dd, zx, bias, st)[0])
    filt = (b_f_w1[0], b_f_b1[0], b_f_w2[0], b_f_b2[0], b_f_w3[0], b_f_b3[0], b_f_w4[0], b_f_freq[0])
    yx = hyena_long(ux, filt, b_d[0])
    yc = hyena_small(uc, *hy_filter_call(ctx_len, *filt), b_d[0])
    w_out = ab_w_out[0].astype(BF16)
    g = a_norm_g[0].reshape(1, MIX_W)
    x = outproj_call("sigmoid", hx[0], hx[1], zx, E_MO // MIX_W, yx, g, w_out, x, mx(0, 2))
    ctx = outproj_call("sigmoid", hc[0], hc[1], zc, E_MO // MIX_W, yc, g, w_out, ctx, mc(0, 2))
    x = ffn(0, x, mx)
    ctx = ffn(0, ctx, mc)

    w_in = pack_odd(cd_w_in[0])
    zx, qkv_x = inproj_call(True, x, mx(1, 0), mx(1, 1), *w_in, c_conv_w[0], c_conv_b[0])
    zc, qkv_c = inproj_call(True, ctx, mc(1, 0), mc(1, 1), *w_in, c_conv_w[0], c_conv_b[0])
    zeros4 = jnp.zeros((2, 1, N_HEADS), F32)
    dtb = jnp.concatenate([c_dt_bias[0][:, None, :], zeros4], axis=-1)
    alog = jnp.concatenate([c_a_log[0][:, None, :], zeros4], axis=-1)
    ox = []
    for dd in range(2):
        _, s_ctx = gdn_call(dd, qkv_c, zc[..., O_GATE:O_GATE + 16], dtb, alog, zero_state[0])
        ox.append(gdn_call(dd, qkv_x, zx[..., O_GATE:O_GATE + 16], dtb, alog, s_ctx)[0])
    tabs = rope_tables(seq)
    kvg = d_kv_norm_g[0].reshape(1, -1)
    w_kv = d_w_kv_up[0].astype(BF16)
    kn_g = _padcols(d_kn_g[0].reshape(1, -1), MLA_QK_PAD)
    kc, vc = kv_call(zc, False, kvg, w_kv, kn_g, tuple(t[:ctx_len] for t in tabs))
    kx, vx = kv_call(zx, True, kvg, w_kv, kn_g, tabs)
    w_q = jnp.concatenate([_padcols(d_w_q_up[0][:, h * MLA_QK:(h + 1) * MLA_QK], MLA_QK_PAD)
                           for h in range(N_HEADS)], axis=1).astype(BF16)
    qx = q_call(zx, d_q_norm_g[0].reshape(1, -1), w_q, _padcols(d_qn_g[0].reshape(1, -1), MLA_QK_PAD), tabs)
    ax = attn_call(qx, kc, vc, kx, vx)
    g = jnp.tile(c_norm_g[0], N_HEADS).reshape(1, MIX_W)
    x = outproj_call("silu", ox[0], ox[1], zx, O_GG // MIX_W, ax, g, cd_w_out[0].astype(BF16), x, mx(1, 2))
    x = ffn(1, x, mx)
    return x
```
